```python
import jax, jax.numpy as jnp
from jax import lax
import numpy as np

D_MODEL = 2048
BATCH = 2
SEQ = 16384
DEPTH = 2
DEC_BATCH = 4
DEC_SEQ = 2048
PAST_LEN = 128

HEAD_DIM = 128
N_HEADS = 8
N_KV_HEADS = 2
GQA_GROUP = N_HEADS // N_KV_HEADS
WINDOW = 128
ATT_BLOCK = 128
ROPE_THETA = 10000.0
Q_W = N_HEADS * HEAD_DIM
KV_W = N_KV_HEADS * HEAD_DIM
POOL_WINDOWS = (2, 4, 8, 16)
N_POOL_GROUPS = len(POOL_WINDOWS)
POOL_W = D_MODEL // 2
POOL_GW = POOL_W // N_POOL_GROUPS
AB_IN_W = Q_W + 2 * KV_W + POOL_W
AB_OUT_W = Q_W + POOL_W
CHUNK = 128
SGU_W = D_MODEL
SGU_HEADS = 8
SGU_HD = SGU_W // SGU_HEADS
N_GROUPS = 4
EXP_PER_GROUP = 8
N_EXPERTS = N_GROUPS * EXP_PER_GROUP
TOP_K = 2
EXPERT_FF = D_MODEL // 2
EXPERT_BLOCK = 256
N_EVEN = (DEPTH + 1) // 2
N_ODD = DEPTH // 2
EPS = 1e-6

kernel_name = "hybrid_bidir_encoder_attn_pool_gmlp_hmoe"


def _rmsnorm(x, g):
    xf = x.astype(jnp.float32)
    r = lax.rsqrt(jnp.mean(xf * xf, axis=-1, keepdims=True) + EPS)
    return (xf * r * g.astype(jnp.float32)).astype(x.dtype)


def _layernorm(x, g, b):
    xf = x.astype(jnp.float32)
    xc = xf - jnp.mean(xf, axis=-1, keepdims=True)
    r = lax.rsqrt(jnp.mean(xc * xc, axis=-1, keepdims=True) + EPS)
    return (xc * r * g.astype(jnp.float32) + b.astype(jnp.float32)).astype(x.dtype)


def _rope(x):
    S = x.shape[1]
    half = HEAD_DIM // 2
    inv = ROPE_THETA ** (-jnp.arange(half, dtype=jnp.float32) / half)
    ang = jnp.arange(S, dtype=jnp.float32)[:, None] * inv[None, :]
    cos = jnp.cos(ang)[None, :, None, :]
    sin = jnp.sin(ang)[None, :, None, :]
    x1 = x[..., :half].astype(jnp.float32)
    x2 = x[..., half:].astype(jnp.float32)
    return jnp.concatenate([x1 * cos - x2 * sin, x2 * cos + x1 * sin], axis=-1).astype(x.dtype)


def _banded_attention(q, k, v, sink):
    B, S, _, _ = q.shape
    nb = S // ATT_BLOCK
    qb = q.reshape(B, nb, ATT_BLOCK, N_KV_HEADS, GQA_GROUP, HEAD_DIM)

    def neigh(t):
        tp = jnp.pad(t, ((0, 0), (ATT_BLOCK, ATT_BLOCK), (0, 0), (0, 0)))
        tp = tp.reshape(B, nb + 2, ATT_BLOCK, N_KV_HEADS, HEAD_DIM)
        return jnp.concatenate([tp[:, :-2], tp[:, 1:-1], tp[:, 2:]], axis=2)

    kb = neigh(k)
    vb = neigh(v)
    s = jnp.einsum('bnqkgd,bnjkd->bnkgqj', qb, kb).astype(jnp.float32) * (HEAD_DIM ** -0.5)
    qpos = np.arange(nb)[:, None] * ATT_BLOCK + np.arange(ATT_BLOCK)[None, :]
    kpos = np.arange(nb)[:, None] * ATT_BLOCK - ATT_BLOCK + np.arange(3 * ATT_BLOCK)[None, :]
    valid = ((kpos[:, None, :] >= 0) & (kpos[:, None, :] < S)
             & (np.abs(qpos[:, :, None] - kpos[:, None, :]) <= WINDOW))
    s = jnp.where(valid[None, :, None, None, :, :], s, -jnp.inf)
    sink_l = sink.astype(jnp.float32).reshape(N_KV_HEADS, GQA_GROUP)[None, None, :, :, None, None]
    m = jnp.maximum(jnp.max(s, axis=-1, keepdims=True), sink_l)
    p = jnp.exp(s - m)
    denom = jnp.sum(p, axis=-1, keepdims=True) + jnp.exp(sink_l - m)
    o = jnp.einsum('bnkgqj,bnjkd->bnqkgd', (p / denom).astype(v.dtype), vb)
    return o.reshape(B, S, Q_W)


def _multiscale_pool(p, pool_w, pool_scale):
    B, S, _ = p.shape
    pf = p.astype(jnp.float32)
    cs = jnp.pad(jnp.cumsum(pf, axis=1), ((0, 0), (1, 0), (0, 0)))
    pos = np.arange(S)
    outs = []
    for g, w in enumerate(POOL_WINDOWS):
        lo = np.clip(pos - w // 2, 0, S)
        hi = np.clip(pos + w // 2, 0, S)
        cnt = (hi - lo).astype(np.float32)
        csg = cs[:, :, g * POOL_GW:(g + 1) * POOL_GW]
        mean = (csg[:, hi] - csg[:, lo]) / cnt[None, :, None]
        outs.append(mean - pf[:, :, g * POOL_GW:(g + 1) * POOL_GW])
    d = jnp.stack(outs, axis=2).astype(p.dtype)
    y = jnp.einsum('bsgc,gcd->bsgd', d, pool_w).reshape(B, S, POOL_W)
    return y * pool_scale


def _attn_pool_mixer(h, w_in, q_g, k_g, sink, pool_w, pool_scale, w_out):
    B, S, _ = h.shape
    proj = h @ w_in
    q = proj[..., :Q_W].reshape(B, S, N_HEADS, HEAD_DIM)
    k = proj[..., Q_W:Q_W + KV_W].reshape(B, S, N_KV_HEADS, HEAD_DIM)
    v = proj[..., Q_W + KV_W:Q_W + 2 * KV_W].reshape(B, S, N_KV_HEADS, HEAD_DIM)
    pch = proj[..., Q_W + 2 * KV_W:]
    q = _rope(_rmsnorm(q, q_g))
    k = _rope(_rmsnorm(k, k_g))
    a = _banded_attention(q, k, v, sink)
    m = _multiscale_pool(pch, pool_w, pool_scale)
    return jnp.concatenate([a, m], axis=-1) @ w_out


def _gmlp_mixer(h, w_in, ln_g, ln_b, sgu_w, sgu_b, w_out):
    B, S, _ = h.shape
    z = jax.nn.gelu(h @ w_in, approximate=False)
    u = z[..., :SGU_W]
    v = _layernorm(z[..., SGU_W:], ln_g, ln_b)
    vc = v.reshape(B, S // CHUNK, CHUNK, SGU_HEADS, SGU_HD)
    s = jnp.einsum('hpq,bnqhc->bnphc', sgu_w, vc) + sgu_b.T[:, :, None]
    return (u * s.reshape(B, S, SGU_W)) @ w_out


def _routed_experts(t, expert, weight, wg, wu, wd):
    T, D = t.shape
    A = T * TOP_K
    flat_e = expert.reshape(-1)
    order = jnp.argsort(flat_e)
    se = flat_e[order]
    tok = order // TOP_K
    counts = jnp.bincount(flat_e, length=N_EXPERTS)
    nblk_e = (counts + EXPERT_BLOCK - 1) // EXPERT_BLOCK
    blk_end = jnp.cumsum(nblk_e)
    blk_start = blk_end - nblk_e
    row_start = jnp.cumsum(counts) - counts
    rank = jnp.arange(A) - row_start[se]
    dest = blk_start[se] * EXPERT_BLOCK + rank
    n_blocks = -(-A // EXPERT_BLOCK) + N_EXPERTS
    src_tok = jnp.zeros((n_blocks * EXPERT_BLOCK,), jnp.int32).at[dest].set(tok.astype(jnp.int32))
    xs = t[src_tok].reshape(n_blocks, EXPERT_BLOCK, D)
    blk_expert = jnp.clip(jnp.searchsorted(blk_end, jnp.arange(n_blocks), side='right'), 0, N_EXPERTS - 1)

    def block_ffn(args):
        xb, e = args
        return (jax.nn.silu(xb @ wg[e]) * (xb @ wu[e])) @ wd[e]

    ys = lax.map(block_ffn, (xs, blk_expert)).reshape(-1, D)
    y_assign = ys[dest] * weight.reshape(-1)[order][:, None].astype(ys.dtype)
    return jax.ops.segment_sum(y_assign, tok, num_segments=T)


def _hier_moe(h, rg_w, rg_b, re_w, re_b, wg, wu, wd):
    B, S, D = h.shape
    t = h.reshape(-1, D)
    T = t.shape[0]
    g_logits = (t @ rg_w).astype(jnp.float32) + rg_b.astype(jnp.float32)
    g_prob = jax.nn.softmax(g_logits, axis=-1)
    g_idx = jnp.argmax(g_logits, axis=-1)
    g_w = jnp.take_along_axis(g_prob, g_idx[:, None], axis=-1)
    e_logits = ((t @ re_w).astype(jnp.float32) + re_b.astype(jnp.float32)).reshape(T, N_GROUPS, EXP_PER_GROUP)
    e_logits = jnp.take_along_axis(e_logits, g_idx[:, None, None], axis=1)[:, 0]
    e_prob = jax.nn.softmax(e_logits, axis=-1)
    top_p, top_i = lax.top_k(e_prob, TOP_K)
    top_p = top_p / jnp.sum(top_p, axis=-1, keepdims=True)
    expert = g_idx[:, None] * EXP_PER_GROUP + top_i
    weight = g_w * top_p
    y = _routed_experts(t, expert, weight, wg, wu, wd)
    return y.reshape(B, S, D)


def _trunk(x, c, ada_w, ada_b, norm_mix_g, norm_ffn_g, ab_w_in, q_norm_g, k_norm_g, attn_sink,
           pool_w, pool_scale, ab_w_out, c_w_in, sgu_ln_g, sgu_ln_b, sgu_w, sgu_b, c_w_out,
           router_group_w, router_group_b, router_expert_w, router_expert_b,
           exp_w_gate, exp_w_up, exp_w_down):
    cs = jax.nn.silu(c)
    for l in range(DEPTH):
        mod = (cs @ ada_w[l] + ada_b[l])[:, None, :].astype(x.dtype)
        sh_m, sc_m, g_m, sh_f, sc_f, g_f = jnp.split(mod, 6, axis=-1)
        h = _rmsnorm(x, norm_mix_g[l]) * (1 + sc_m) + sh_m
        if l % 2 == 0:
            i = l // 2
            mix = _attn_pool_mixer(h, ab_w_in[i], q_norm_g[i], k_norm_g[i], attn_sink[i],
                                   pool_w[i], pool_scale[i], ab_w_out[i])
        else:
            i = l // 2
            mix = _gmlp_mixer(h, c_w_in[i], sgu_ln_g[i], sgu_ln_b[i], sgu_w[i], sgu_b[i], c_w_out[i])
        x = x + g_m * mix
        h = _rmsnorm(x, norm_ffn_g[l]) * (1 + sc_f) + sh_f
        x = x + g_f * _hier_moe(h, router_group_w[l], router_group_b[l], router_expert_w[l],
                                router_expert_b[l], exp_w_gate[l], exp_w_up[l], exp_w_down[l])
    return x


def setup_inputs(seed: int = 0) -> dict:
    key = jax.random.key(seed)
    ks = jax.random.split(key, 32)

    def nrm(k, shape, scale):
        return scale * jax.random.normal(k, shape, jnp.float32)

    D = D_MODEL
    return {
        "x_prompt": nrm(ks[0], (BATCH, SEQ, D), 1.0),
        "x_sample": nrm(ks[1], (DEC_BATCH, DEC_SEQ, D), 1.0),
        "c_prompt": nrm(ks[2], (BATCH, D), 1.0),
        "c_sample": nrm(ks[3], (DEC_BATCH, D), 1.0),
        "ada_w": nrm(ks[4], (DEPTH, D, 6 * D), 0.5 * D ** -0.5),
        "ada_b": nrm(ks[5], (DEPTH, 6 * D), 0.1),
        "norm_mix_g": 1.0 + nrm(ks[6], (DEPTH, D), 0.02),
        "norm_ffn_g": 1.0 + nrm(ks[7], (DEPTH, D), 0.02),
        "ab_w_in": nrm(ks[8], (N_EVEN, D, AB_IN_W), D ** -0.5),
        "q_norm_g": 1.0 + nrm(ks[9], (N_EVEN, HEAD_DIM), 0.02),
        "k_norm_g": 1.0 + nrm(ks[10], (N_EVEN, HEAD_DIM), 0.02),
        "attn_sink": nrm(ks[11], (N_EVEN, N_HEADS), 1.0),
        "pool_w": nrm(ks[12], (N_EVEN, N_POOL_GROUPS, POOL_GW, POOL_GW), POOL_GW ** -0.5),
        "pool_scale": 1.0 + nrm(ks[13], (N_EVEN, POOL_W), 0.02),
        "ab_w_out": nrm(ks[14], (N_EVEN, AB_OUT_W, D), AB_OUT_W ** -0.5),
        "c_w_in": nrm(ks[15], (N_ODD, D, 2 * SGU_W), D ** -0.5),
        "sgu_ln_g": 1.0 + nrm(ks[16], (N_ODD, SGU_W), 0.02),
        "sgu_ln_b": nrm(ks[17], (N_ODD, SGU_W), 0.02),
        "sgu_w": nrm(ks[18], (N_ODD, SGU_HEADS, CHUNK, CHUNK), CHUNK ** -0.5),
        "sgu_b": 1.0 + nrm(ks[19], (N_ODD, SGU_HEADS, CHUNK), 0.1),
        "c_w_out": nrm(ks[20], (N_ODD, SGU_W, D), SGU_W ** -0.5),
        "router_group_w": nrm(ks[21], (DEPTH, D, N_GROUPS), D ** -0.5),
        "router_group_b": nrm(ks[22], (DEPTH, N_GROUPS), 0.01),
        "router_expert_w": nrm(ks[23], (DEPTH, D, N_EXPERTS), D ** -0.5),
        "router_expert_b": nrm(ks[24], (DEPTH, N_EXPERTS), 0.01),
        "exp_w_gate": nrm(ks[25], (DEPTH, N_EXPERTS, D, EXPERT_FF), D ** -0.5),
        "exp_w_up": nrm(ks[26], (DEPTH, N_EXPERTS, D, EXPERT_FF), D ** -0.5),
        "exp_w_down": nrm(ks[27], (DEPTH, N_EXPERTS, EXPERT_FF, D), EXPERT_FF ** -0.5),
    }


def reference(x_prompt, x_sample, c_prompt, c_sample, ada_w, ada_b, norm_mix_g, norm_ffn_g,
              ab_w_in, q_norm_g, k_norm_g, attn_sink, pool_w, pool_scale, ab_w_out,
              c_w_in, sgu_ln_g, sgu_ln_b, sgu_w, sgu_b, c_w_out,
              router_group_w, router_group_b, router_expert_w, router_expert_b,
              exp_w_gate, exp_w_up, exp_w_down):
    y_prompt = _trunk(x_prompt, c_prompt, ada_w, ada_b, norm_mix_g, norm_ffn_g, ab_w_in, q_norm_g,
                      k_norm_g, attn_sink, pool_w, pool_scale, ab_w_out, c_w_in, sgu_ln_g, sgu_ln_b,
                      sgu_w, sgu_b, c_w_out, router_group_w, router_group_b, router_expert_w,
                      router_expert_b, exp_w_gate, exp_w_up, exp_w_down)
    y_sample = _trunk(x_sample, c_sample, ada_w, ada_b, norm_mix_g, norm_ffn_g, ab_w_in, q_norm_g,
                      k_norm_g, attn_sink, pool_w, pool_scale, ab_w_out, c_w_in, sgu_ln_g, sgu_ln_b,
                      sgu_w, sgu_b, c_w_out, router_group_w, router_group_b, router_expert_w,
                      router_expert_b, exp_w_gate, exp_w_up, exp_w_down)
    return (y_prompt, y_sample)
```

```python
import functools

import numpy as np
import jax
import jax.numpy as jnp
from jax import lax
from jax.experimental import pallas as pl
from jax.experimental.pallas import tpu as pltpu

HEAD_DIM = 128
N_HEADS = 8
N_KV_HEADS = 2
GQA_GROUP = N_HEADS // N_KV_HEADS
ATT_BLOCK = 128
ROPE_THETA = 10000.0
Q_W = N_HEADS * HEAD_DIM
KV_W = N_KV_HEADS * HEAD_DIM
POOL_WINDOWS = (2, 4, 8, 16)
POOL_HALO = 8
CHUNK = 128
SGU_HEADS = 8
N_GROUPS = 4
EXP_PER_GROUP = 8
N_EXPERTS = N_GROUPS * EXP_PER_GROUP
EPS = 1e-6

VMEM_LIMIT_BYTES = 56 * 1024 * 1024
LANES = 128

F32 = jnp.float32
BF16 = jnp.bfloat16
NEG_BIG = -1e30


def _params(sem):
    return pltpu.CompilerParams(dimension_semantics=sem, vmem_limit_bytes=VMEM_LIMIT_BYTES)


def _resident(shape):
    nd = len(shape)
    return pl.BlockSpec(shape, lambda *_: (0,) * nd, pipeline_mode=pl.Buffered(1))


def _rows(tm, width):
    return pl.BlockSpec((tm, width), lambda i, *_: (i, 0))


def _mod_spec(part, d):
    return pl.BlockSpec((None, None, 1, d), lambda i, ts, *_: (part, ts[i], 0, 0))


def _norm_mod(x, g, sc, sh):
    r = lax.rsqrt(jnp.mean(x * x, axis=-1, keepdims=True) + EPS)
    return x * r * g * (1.0 + sc) + sh


def _dot(a, b):
    return jnp.dot(a, b, preferred_element_type=F32)


def _dot_nt(a, b):
    return lax.dot_general(a, b, (((1,), (1,)), ((), ())), preferred_element_type=F32)


def _ada_kernel(c_ref, w_ref, b_ref, o_ref):
    c = c_ref[...]
    cs = c * (1.0 / (1.0 + jnp.exp(-c)))
    o_ref[...] = _dot(cs.astype(BF16), w_ref[...].astype(BF16)) + b_ref[...]


def _ada_mod(c_pad, ada_w, ada_b):
    depth, d, n = ada_w.shape
    tn = 1024
    return pl.pallas_call(
        _ada_kernel,
        grid=(depth, n // tn),
        in_specs=[
            pl.BlockSpec(c_pad.shape, lambda l, j: (0, 0)),
            pl.BlockSpec((None, d, tn), lambda l, j: (l, 0, j)),
            pl.BlockSpec((None, 1, tn), lambda l, j: (l, 0, j)),
        ],
        out_specs=pl.BlockSpec((None, c_pad.shape[0], tn), lambda l, j: (l, 0, j)),
        out_shape=jax.ShapeDtypeStruct((depth, c_pad.shape[0], n), F32),
        compiler_params=_params(("arbitrary", "arbitrary")),
        name="ada_mod",
    )(c_pad, ada_w, ada_b.reshape(depth, 1, n))


def _inproj_kernel(ts, tp, x_ref, g_ref, sc_ref, sh_ref, w_ref, qg_ref, kg_ref, cos_ref, sin_ref,
                   q_ref, k_ref, v_ref, p_ref):
    h = _norm_mod(x_ref[...], g_ref[...], sc_ref[...], sh_ref[...]).astype(BF16)
    cos = cos_ref[...]
    sin = sin_ref[...]

    def head_norm_rope(y, gain):
        r = lax.rsqrt(jnp.mean(y * y, axis=-1, keepdims=True) + EPS)
        y = y * r * gain
        return y * cos + pltpu.roll(y, HEAD_DIM // 2, 1) * sin

    q = _dot(h, w_ref[:, 0:Q_W])
    for hh in range(N_HEADS):
        sl = slice(hh * HEAD_DIM, (hh + 1) * HEAD_DIM)
        q_ref[:, sl] = head_norm_rope(q[:, sl], qg_ref[...]).astype(BF16)
    kv = _dot(h, w_ref[:, Q_W:Q_W + 2 * KV_W])
    for hh in range(N_KV_HEADS):
        sl = slice(hh * HEAD_DIM, (hh + 1) * HEAD_DIM)
        k_ref[:, sl] = head_norm_rope(kv[:, sl], kg_ref[...]).astype(BF16)
    v_ref[...] = kv[:, KV_W:].astype(BF16)
    p_ref[...] = _dot(h, w_ref[:, Q_W + 2 * KV_W:])


def _inproj(x, meta, mod, norm_g, w_in_bf, q_g, k_g, cos_t, sin_t, tm):
    t, d = x.shape
    pool_w = w_in_bf.shape[1] - Q_W - 2 * KV_W
    n_tiles = t // tm
    rope_spec = pl.BlockSpec((tm, HEAD_DIM), lambda i, ts, tp: (tp[i] // tm, 0))
    grid_spec = pltpu.PrefetchScalarGridSpec(
        num_scalar_prefetch=2,
        grid=(n_tiles,),
        in_specs=[
            _rows(tm, d),
            _resident((1, d)),
            _mod_spec(1, d),
            _mod_spec(0, d),
            _resident(w_in_bf.shape),
            _resident((1, HEAD_DIM)),
            _resident((1, HEAD_DIM)),
            rope_spec,
            rope_spec,
        ],
        out_specs=[_rows(tm, Q_W), _rows(tm, KV_W), _rows(tm, KV_W), _rows(tm, pool_w)],
    )
    return pl.pallas_call(
        _inproj_kernel,
        grid_spec=grid_spec,
        out_shape=[
            jax.ShapeDtypeStruct((t, Q_W), BF16),
            jax.ShapeDtypeStruct((t, KV_W), BF16),
            jax.ShapeDtypeStruct((t, KV_W), BF16),
            jax.ShapeDtypeStruct((t, pool_w), F32),
        ],
        compiler_params=_params(("arbitrary",)),
        name="attn_pool_inproj",
    )(meta["seq"], meta["pos"], x, norm_g.reshape(1, d), mod, mod, w_in_bf,
      q_g.reshape(1, HEAD_DIM), k_g.reshape(1, HEAD_DIM), cos_t, sin_t)


def _attn_kernel(tp, tl, q_ref, kc_ref, kp_ref, kn_ref, vc_ref, vp_ref, vn_ref, sink_ref, o_ref, kx, vx):
    i = pl.program_id(0)
    tm = q_ref.shape[0]
    nb = tm // ATT_BLOCK
    first = tp[i] == 0
    last = tp[i] + tm == tl[i]
    kx[0:ATT_BLOCK] = kp_ref[...]
    kx[ATT_BLOCK:ATT_BLOCK + tm] = kc_ref[...]
    kx[ATT_BLOCK + tm:] = kn_ref[...]
    vx[0:ATT_BLOCK] = vp_ref[...]
    vx[ATT_BLOCK:ATT_BLOCK + tm] = vc_ref[...]
    vx[ATT_BLOCK + tm:] = vn_ref[...]
    win = 3 * ATT_BLOCK
    qi = lax.broadcasted_iota(jnp.int32, (ATT_BLOCK, win), 0)
    kj = lax.broadcasted_iota(jnp.int32, (ATT_BLOCK, win), 1)
    band = (kj >= qi) & (kj <= qi + 2 * ATT_BLOCK)
    scale = HEAD_DIM ** -0.5
    for b in range(nb):
        valid = band
        if b == 0:
            valid = valid & (kj >= jnp.where(first, ATT_BLOCK, 0))
        if b == nb - 1:
            valid = valid & (kj < jnp.where(last, 2 * ATT_BLOCK, win))
        bias = jnp.where(valid, 0.0, -jnp.inf).astype(F32)
        rows = slice(b * ATT_BLOCK, (b + 1) * ATT_BLOCK)
        for kk in range(N_KV_HEADS):
            cols = slice(kk * HEAD_DIM, (kk + 1) * HEAD_DIM)
            kw = kx[b * ATT_BLOCK:b * ATT_BLOCK + win, cols]
            vw = vx[b * ATT_BLOCK:b * ATT_BLOCK + win, cols]
            heads = [kk * GQA_GROUP + g for g in range(GQA_GROUP)]
            qs = jnp.concatenate([q_ref[rows, hd * HEAD_DIM:(hd + 1) * HEAD_DIM] for hd in heads], axis=0)
            s = _dot_nt(qs, kw) * scale
            s = (s.reshape(GQA_GROUP, ATT_BLOCK, win) + bias[None]).reshape(GQA_GROUP * ATT_BLOCK, win)
            sk = sink_ref[kk][:, 0:1]
            m = jnp.maximum(jnp.max(s, axis=-1, keepdims=True), sk)
            p = jnp.exp(s - m)
            denom = jnp.sum(p, axis=-1, keepdims=True) + jnp.exp(sk - m)
            o = _dot(p.astype(BF16), vw) / denom
            for g, hd in enumerate(heads):
                o_ref[rows, hd * HEAD_DIM:(hd + 1) * HEAD_DIM] = (
                    o[g * ATT_BLOCK:(g + 1) * ATT_BLOCK].astype(BF16))


def _attention(q, k, v, sink_b, meta, tm):
    t = q.shape[0]
    n_tiles = t // tm
    r = tm // ATT_BLOCK
    n_blk = t // ATT_BLOCK
    cur = pl.BlockSpec((tm, KV_W), lambda i, *_: (i, 0))
    prev = pl.BlockSpec((ATT_BLOCK, KV_W), lambda i, *_: (jnp.maximum(i * r - 1, 0), 0))
    nxt = pl.BlockSpec((ATT_BLOCK, KV_W), lambda i, *_: (jnp.minimum((i + 1) * r, n_blk - 1), 0))
    grid_spec = pltpu.PrefetchScalarGridSpec(
        num_scalar_prefetch=2,
        grid=(n_tiles,),
        in_specs=[_rows(tm, Q_W), cur, prev, nxt, cur, prev, nxt, _resident(sink_b.shape)],
        out_specs=_rows(tm, Q_W),
        scratch_shapes=[pltpu.VMEM((tm + 2 * ATT_BLOCK, KV_W), BF16),
                        pltpu.VMEM((tm + 2 * ATT_BLOCK, KV_W), BF16)],
    )
    return pl.pallas_call(
        _attn_kernel,
        grid_spec=grid_spec,
        out_shape=jax.ShapeDtypeStruct((t, Q_W), BF16),
        compiler_params=_params(("arbitrary",)),
        name="banded_attention",
    )(meta["pos"], meta["len"], q, k, k, k, v, v, v, sink_b)


def _mix0_kernel(tp, tl, ts, x_ref, a_ref, pc_ref, pp_ref, pn_ref, gate_ref, pw_ref, ps_ref, wo_ref,
                 o_ref, pext):
    i = pl.program_id(0)
    tm = x_ref.shape[0]
    first = tp[i] == 0
    last = tp[i] + tm == tl[i]
    pext[0:POOL_HALO] = jnp.where(first, 0.0, pp_ref[...])
    pext[POOL_HALO:POOL_HALO + tm] = pc_ref[...]
    pext[POOL_HALO + tm:] = jnp.where(last, 0.0, pn_ref[...])
    n_g = len(POOL_WINDOWS)
    gw = pc_ref.shape[1] // n_g
    pos = tp[i] + lax.broadcasted_iota(jnp.int32, (tm, gw), 0)
    seq_len = tl[i]
    ms = []
    for g, w in enumerate(POOL_WINDOWS):
        cols = slice(g * gw, (g + 1) * gw)
        acc = pext[POOL_HALO - w // 2:POOL_HALO - w // 2 + tm, cols]
        for off in range(-w // 2 + 1, w // 2):
            acc = acc + pext[POOL_HALO + off:POOL_HALO + off + tm, cols]
        cnt = (jnp.minimum(pos + w // 2, seq_len) - jnp.maximum(pos - w // 2, 0)).astype(F32)
        dlt = acc / cnt - pc_ref[:, cols]
        ms.append((_dot(dlt.astype(BF16), pw_ref[g]) * ps_ref[:, cols]).astype(BF16))
    m = jnp.concatenate(ms, axis=1)
    n_a = a_ref.shape[1]
    mix = _dot(a_ref[...], wo_ref[0:n_a, :]) + _dot(m, wo_ref[n_a:, :])
    o_ref[...] = x_ref[...] + gate_ref[...] * mix


def _mix0(x, a, p, meta, mod, pool_w_bf, pool_scale, w_out_bf, tm):
    t, d = x.shape
    pw = p.shape[1]
    n_tiles = t // tm
    r = tm // POOL_HALO
    n_hb = t // POOL_HALO
    prev = pl.BlockSpec((POOL_HALO, pw), lambda i, *_: (jnp.maximum(i * r - 1, 0), 0))
    nxt = pl.BlockSpec((POOL_HALO, pw), lambda i, *_: (jnp.minimum((i + 1) * r, n_hb - 1), 0))
    gate_spec = pl.BlockSpec((None, None, 1, d), lambda i, tp, tl, ts: (2, ts[i], 0, 0))
    grid_spec = pltpu.PrefetchScalarGridSpec(
        num_scalar_prefetch=3,
        grid=(n_tiles,),
        in_specs=[_rows(tm, d), _rows(tm, a.shape[1]), _rows(tm, pw), prev, nxt, gate_spec,
                  _resident(pool_w_bf.shape), _resident((1, pw)), _resident(w_out_bf.shape)],
        out_specs=_rows(tm, d),
        scratch_shapes=[pltpu.VMEM((tm + 2 * POOL_HALO, pw), F32)],
    )
    return pl.pallas_call(
        _mix0_kernel,
        grid_spec=grid_spec,
        out_shape=jax.ShapeDtypeStruct((t, d), F32),
        compiler_params=_params(("arbitrary",)),
        name="pool_outproj",
    )(meta["pos"], meta["len"], meta["seq"], x, a, p, p, p, mod, pool_w_bf,
      pool_scale.reshape(1, pw), w_out_bf)


def _gelu(z):
    return 0.5 * z * (1.0 + lax.erf(z * np.float32(np.sqrt(0.5))))


def _gmlp_in_kernel(ts, x_ref, g_ref, sc_ref, sh_ref, w_ref, lg_ref, lb_ref, u_ref, v_ref):
    h = _norm_mod(x_ref[...], g_ref[...], sc_ref[...], sh_ref[...]).astype(BF16)
    half = u_ref.shape[1]
    u_ref[...] = _gelu(_dot(h, w_ref[:, 0:half]))
    zv = _gelu(_dot(h, w_ref[:, half:]))
    zc = zv - jnp.mean(zv, axis=-1, keepdims=True)
    r = lax.rsqrt(jnp.mean(zc * zc, axis=-1, keepdims=True) + EPS)
    v_ref[...] = (zc * r * lg_ref[...] + lb_ref[...]).astype(BF16)


def _gmlp_in(x, meta, mod, norm_g, w_in_bf, ln_g, ln_b, tm):
    t, d = x.shape
    half = w_in_bf.shape[1] // 2
    grid_spec = pltpu.PrefetchScalarGridSpec(
        num_scalar_prefetch=1,
        grid=(t // tm,),
        in_specs=[_rows(tm, d), _resident((1, d)), _mod_spec(1, d), _mod_spec(0, d),
                  _resident(w_in_bf.shape), _resident((1, half)), _resident((1, half))],
        out_specs=[_rows(tm, half), _rows(tm, half)],
    )
    return pl.pallas_call(
        _gmlp_in_kernel,
        grid_spec=grid_spec,
        out_shape=[jax.ShapeDtypeStruct((t, half), F32), jax.ShapeDtypeStruct((t, half), BF16)],
        compiler_params=_params(("arbitrary",)),
        name="gmlp_in",
    )(meta["seq"], x, norm_g.reshape(1, d), mod, mod, w_in_bf, ln_g.reshape(1, half), ln_b.reshape(1, half))


def _gmlp_out_kernel(ts, x_ref, u_ref, v_ref, gate_ref, sw_ref, sb_ref, wo_ref, o_ref, gated):
    tm = x_ref.shape[0]
    hd = u_ref.shape[1] // SGU_HEADS
    for c in range(tm // CHUNK):
        rows = slice(c * CHUNK, (c + 1) * CHUNK)
        for hh in range(SGU_HEADS):
            cols = slice(hh * hd, (hh + 1) * hd)
            s = _dot(sw_ref[hh], v_ref[rows, cols]) + jnp.tile(sb_ref[hh], (1, hd // LANES))
            gated[rows, cols] = (u_ref[rows, cols] * s).astype(BF16)
    o_ref[...] = x_ref[...] + gate_ref[...] * _dot(gated[...], wo_ref[...])


def _gmlp_out(x, u, v, meta, mod, sgu_w_bf, sgu_b_b, w_out_bf, tm):
    t, d = x.shape
    w = u.shape[1]
    gate_spec = pl.BlockSpec((None, None, 1, d), lambda i, ts: (2, ts[i], 0, 0))
    grid_spec = pltpu.PrefetchScalarGridSpec(
        num_scalar_prefetch=1,
        grid=(t // tm,),
        in_specs=[_rows(tm, d), _rows(tm, w), _rows(tm, w), gate_spec,
                  _resident(sgu_w_bf.shape), _resident(sgu_b_b.shape), _resident(w_out_bf.shape)],
        out_specs=_rows(tm, d),
        scratch_shapes=[pltpu.VMEM((tm, w), BF16)],
    )
    return pl.pallas_call(
        _gmlp_out_kernel,
        grid_spec=grid_spec,
        out_shape=jax.ShapeDtypeStruct((t, d), F32),
        compiler_params=_params(("arbitrary",)),
        name="gmlp_out",
    )(meta["seq"], x, u, v, mod, sgu_w_bf, sgu_b_b, w_out_bf)


ROUTER_ROWS = 48


def _router_kernel(ts, x_ref, g_ref, sc_ref, sh_ref, wr_ref, br_ref, h_ref, ri_ref, rw_ref, cnt_ref, carry):
    i = pl.program_id(0)

    @pl.when(i == 0)
    def _():
        carry[...] = jnp.zeros_like(carry)

    tm = x_ref.shape[0]
    h = _norm_mod(x_ref[...], g_ref[...], sc_ref[...], sh_ref[...])
    h_ref[...] = h
    lg = _dot_nt(wr_ref[...], h.astype(BF16)) + jnp.tile(br_ref[...], (1, tm // LANES))
    rows8 = lax.broadcasted_iota(jnp.int32, (EXP_PER_GROUP, tm), 0)

    def first_argmax(vals, vmax):
        return jnp.min(jnp.where(vals == vmax, rows8, EXP_PER_GROUP), axis=0, keepdims=True)

    gl = lg[0:8]
    gmax = jnp.max(gl, axis=0, keepdims=True)
    gidx = first_argmax(gl, gmax)
    g_w = 1.0 / jnp.sum(jnp.exp(gl - gmax), axis=0, keepdims=True)
    esel = jnp.zeros((EXP_PER_GROUP, tm), F32)
    for g in range(N_GROUPS):
        esel = jnp.where(gidx == g, lg[8 + g * EXP_PER_GROUP:8 + (g + 1) * EXP_PER_GROUP], esel)
    emax = jnp.max(esel, axis=0, keepdims=True)
    pe = jnp.exp(esel - emax)
    prob = pe / jnp.sum(pe, axis=0, keepdims=True)
    p1 = jnp.max(prob, axis=0, keepdims=True)
    i1 = first_argmax(prob, p1)
    rest = jnp.where(rows8 == i1, -1.0, prob)
    p2 = jnp.max(rest, axis=0, keepdims=True)
    i2 = first_argmax(rest, p2)
    den = p1 + p2
    w0 = g_w * (p1 / den)
    w1 = g_w * (p2 / den)
    e0 = gidx * EXP_PER_GROUP + i1
    e1 = gidx * EXP_PER_GROUP + i2

    rows_e = lax.broadcasted_iota(jnp.int32, (N_EXPERTS, tm), 0)
    oh0 = rows_e == e0
    oh1 = rows_e == e1
    both = jnp.where(oh0, 1.0, 0.0) + jnp.where(oh1, 1.0, 0.0)
    ri = lax.broadcasted_iota(jnp.int32, (tm, tm), 0)
    ci = lax.broadcasted_iota(jnp.int32, (tm, tm), 1)
    upper = jnp.where(ri < ci, 1.0, 0.0).astype(BF16)
    before = _dot(both.astype(BF16), upper) + carry[:, 0:1]
    r0 = jnp.sum(jnp.where(oh0, before, 0.0), axis=0, keepdims=True)
    r1 = jnp.sum(jnp.where(oh1, before, 0.0), axis=0, keepdims=True)
    new_cnt = carry[...] + jnp.sum(both, axis=1, keepdims=True)
    carry[...] = new_cnt
    cnt_ref[...] = new_cnt
    zi = jnp.zeros((4, tm), jnp.int32)
    ri_ref[...] = jnp.concatenate([e0, e1, r0.astype(jnp.int32), r1.astype(jnp.int32), zi], axis=0)
    rw_ref[...] = jnp.concatenate([w0, w1, jnp.zeros((6, tm), F32)], axis=0)


def _router(x, meta, mod, norm_g, wr_bf, br_b, tm):
    t, d = x.shape
    grid_spec = pltpu.PrefetchScalarGridSpec(
        num_scalar_prefetch=1,
        grid=(t // tm,),
        in_specs=[_rows(tm, d), _resident((1, d)), _mod_spec(4, d), _mod_spec(3, d),
                  _resident(wr_bf.shape), _resident(br_b.shape)],
        out_specs=[_rows(tm, d),
                   pl.BlockSpec((8, tm), lambda i, *_: (0, i)),
                   pl.BlockSpec((8, tm), lambda i, *_: (0, i)),
                   pl.BlockSpec((N_EXPERTS, LANES), lambda i, *_: (0, 0))],
        scratch_shapes=[pltpu.VMEM((N_EXPERTS, LANES), F32)],
    )
    return pl.pallas_call(
        _router_kernel,
        grid_spec=grid_spec,
        out_shape=[jax.ShapeDtypeStruct((t, d), F32),
                   jax.ShapeDtypeStruct((8, t), jnp.int32),
                   jax.ShapeDtypeStruct((8, t), F32),
                   jax.ShapeDtypeStruct((N_EXPERTS, LANES), F32)],
        compiler_params=_params(("arbitrary",)),
        name="moe_router",
    )(meta["seq"], x, norm_g.reshape(1, d), mod, mod, wr_bf, br_b)


def _gather_rows(idx_ref, n, src_hbm, dst, sem):
    def row_copy(r, tok):
        return pltpu.make_async_copy(src_hbm.at[pl.ds(tok, 1)], dst.at[pl.ds(r, 1)], sem)

    def issue(r, c):
        row_copy(r, idx_ref[0, 0, r]).start()
        return c

    def wait(r, c):
        row_copy(r, 0).wait()
        return c

    lax.fori_loop(0, n, issue, 0)
    lax.fori_loop(0, n, wait, 0)


def _ffn_kernel(be, nu, idx_ref, h_hbm, wg_ref, wu_ref, wd_ref, o_ref, xs, xb, sem):
    i = pl.program_id(0)
    j = pl.program_id(1)

    @pl.when((i >= nu[0]) & (j == 0))
    def _():
        o_ref[...] = jnp.zeros_like(o_ref)

    @pl.when(i < nu[0])
    def _():
        @pl.when(j == 0)
        def _():
            _gather_rows(idx_ref, xs.shape[0], h_hbm, xs, sem)
            xb[...] = xs[...].astype(BF16)

        x = xb[...]
        hg = _dot(x, wg_ref[...].astype(BF16))
        hu = _dot(x, wu_ref[...].astype(BF16))
        act = (hg * (1.0 / (1.0 + jnp.exp(-hg))) * hu).astype(BF16)
        part = _dot(act, wd_ref[...].astype(BF16))

        @pl.when(j == 0)
        def _():
            o_ref[...] = part

        @pl.when(j > 0)
        def _():
            o_ref[...] += part


def _expert_ffn(h, src_tok, blk_expert, n_used, wg, wu, wd, rb, ft):
    t, d = h.shape
    n_blocks = src_tok.shape[0]
    ff = wg.shape[2]
    n_ff = ff // ft

    def ffj(i, j, nu):
        return jnp.where(i < nu[0], j, n_ff - 1)

    grid_spec = pltpu.PrefetchScalarGridSpec(
        num_scalar_prefetch=2,
        grid=(n_blocks, n_ff),
        in_specs=[
            pl.BlockSpec((1, 1, rb), lambda i, j, be, nu: (i, 0, 0), memory_space=pltpu.SMEM),
            pl.BlockSpec(memory_space=pl.ANY),
            pl.BlockSpec((None, d, ft), lambda i, j, be, nu: (be[i], 0, ffj(i, j, nu))),
            pl.BlockSpec((None, d, ft), lambda i, j, be, nu: (be[i], 0, ffj(i, j, nu))),
            pl.BlockSpec((None, ft, d), lambda i, j, be, nu: (be[i], ffj(i, j, nu), 0)),
        ],
        out_specs=pl.BlockSpec((rb, d), lambda i, j, be, nu: (i, 0)),
        scratch_shapes=[pltpu.VMEM((rb, d), F32), pltpu.VMEM((rb, d), BF16), pltpu.SemaphoreType.DMA],
    )
    return pl.pallas_call(
        _ffn_kernel,
        grid_spec=grid_spec,
        out_shape=jax.ShapeDtypeStruct((n_blocks * rb, d), F32),
        compiler_params=_params(("arbitrary", "arbitrary")),
        name="expert_ffn",
    )(blk_expert, n_used, src_tok, h, wg, wu, wd)


def _combine_kernel(ts, idx_ref, x_ref, wt_ref, gate_ref, ys_hbm, o_ref, buf, sem):
    tm = x_ref.shape[0]
    _gather_rows(idx_ref, 2 * tm, ys_hbm, buf, sem)
    y = wt_ref[:, 0:1] * buf[0:tm] + wt_ref[:, 1:2] * buf[tm:2 * tm]
    o_ref[...] = x_ref[...] + gate_ref[...] * y


def _combine(x, ys, slots, wt, meta, mod, tm):
    t, d = x.shape
    gate_spec = pl.BlockSpec((None, None, 1, d), lambda i, ts: (5, ts[i], 0, 0))
    grid_spec = pltpu.PrefetchScalarGridSpec(
        num_scalar_prefetch=1,
        grid=(t // tm,),
        in_specs=[
            pl.BlockSpec((1, 1, 2 * tm), lambda i, ts: (i, 0, 0), memory_space=pltpu.SMEM),
            _rows(tm, d),
            _rows(tm, wt.shape[1]),
            gate_spec,
            pl.BlockSpec(memory_space=pl.ANY),
        ],
        out_specs=_rows(tm, d),
        scratch_shapes=[pltpu.VMEM((2 * tm, d), F32), pltpu.SemaphoreType.DMA],
    )
    return pl.pallas_call(
        _combine_kernel,
        grid_spec=grid_spec,
        out_shape=jax.ShapeDtypeStruct((t, d), F32),
        compiler_params=_params(("arbitrary",)),
        name="moe_combine",
    )(meta["seq"], slots, x, wt, mod, ys)


def _tile_meta(seq_lens, tm):
    seq, pos, ln = [], [], []
    for s, n in enumerate(seq_lens):
        assert n % tm == 0
        for k in range(n // tm):
            seq.append(s)
            pos.append(k * tm)
            ln.append(n)
    return {k: jnp.asarray(np.asarray(v, np.int32)) for k, v in (("seq", seq), ("pos", pos), ("len", ln))}


def _rope_tables(s_max):
    half = HEAD_DIM // 2
    inv = ROPE_THETA ** (-jnp.arange(half, dtype=F32) / half)
    ang = jnp.arange(s_max, dtype=F32)[:, None] * inv[None, :]
    cos, sin = jnp.cos(ang), jnp.sin(ang)
    return jnp.concatenate([cos, cos], axis=1), jnp.concatenate([-sin, sin], axis=1)


def _moe(x, metas, mod, norm_g, rg_w, rg_b, re_w, re_b, wg, wu, wd, cfg):
    t, d = x.shape
    wr = jnp.zeros((ROUTER_ROWS, d), F32).at[0:N_GROUPS].set(rg_w.T).at[8:8 + N_EXPERTS].set(re_w.T)
    br = jnp.zeros((ROUTER_ROWS,), F32).at[N_GROUPS:8].set(NEG_BIG).at[0:N_GROUPS].set(rg_b)
    br = br.at[8:8 + N_EXPERTS].set(re_b)
    br_b = jnp.broadcast_to(br[:, None], (ROUTER_ROWS, LANES))
    h, ri, rw, cnt = _router(x, metas[cfg["tm_router"]], mod, norm_g, wr.astype(BF16), br_b, cfg["tm_router"])

    rb = cfg["rb"]
    expert = ri[0:2]
    rank = ri[2:4]
    counts = cnt[:, 0].astype(jnp.int32)
    nblk = (counts + rb - 1) // rb
    blk_end = jnp.cumsum(nblk)
    blk_start = blk_end - nblk
    slot = blk_start[expert] * rb + rank
    n_blocks = -(-2 * t // rb) + N_EXPERTS
    tok = jnp.tile(jnp.arange(t, dtype=jnp.int32), 2)
    src_tok = jnp.zeros((n_blocks * rb,), jnp.int32).at[slot.reshape(-1)].set(tok)
    n_used = blk_end[-1]
    blk = jnp.arange(n_blocks, dtype=jnp.int32)
    blk_expert = jnp.clip(jnp.searchsorted(blk_end, blk, side="right"), 0, N_EXPERTS - 1).astype(jnp.int32)
    blk_expert = jnp.where(blk < n_used, blk_expert, blk_expert[jnp.maximum(n_used - 1, 0)])

    ys = _expert_ffn(h, src_tok.reshape(n_blocks, 1, rb), blk_expert, n_used.reshape(1).astype(jnp.int32),
                     wg, wu, wd, rb, cfg["ft"])

    tmc = cfg["tm_combine"]
    slots = slot.reshape(2, t // tmc, tmc).transpose(1, 0, 2).reshape(t // tmc, 1, 2 * tmc)
    wt = rw.T
    return _combine(x, ys, slots, wt, metas[tmc], mod, tmc)


def _config(seq_lens):
    g = int(np.gcd.reduce(np.asarray(seq_lens)))
    tm = min(512, g)
    return {"tm": tm, "tm_router": tm, "tm_combine": min(256, g), "tm_gmlp": min(256, g), "rb": 1024, "ft": 256}


def _forward(x_prompt, x_sample, c_prompt, c_sample, ada_w, ada_b, norm_mix_g, norm_ffn_g,
             ab_w_in, q_norm_g, k_norm_g, attn_sink, pool_w, pool_scale, ab_w_out,
             c_w_in, sgu_ln_g, sgu_ln_b, sgu_w, sgu_b, c_w_out,
             router_group_w, router_group_b, router_expert_w, router_expert_b,
             exp_w_gate, exp_w_up, exp_w_down, cfg=None):
    bp, sp, d = x_prompt.shape
    bs, ss, _ = x_sample.shape
    seq_lens = [sp] * bp + [ss] * bs
    n_seq = len(seq_lens)
    if cfg is None:
        cfg = _config(seq_lens)
    depth = ada_w.shape[0]
    x = jnp.concatenate([x_prompt.reshape(bp * sp, d), x_sample.reshape(bs * ss, d)], axis=0)
    metas = {tm: _tile_meta(seq_lens, tm) for tm in
             {cfg["tm"], cfg["tm_router"], cfg["tm_combine"], cfg["tm_gmlp"]}}

    n_pad = -(-n_seq // 8) * 8
    c_pad = jnp.zeros((n_pad, d), F32).at[0:n_seq].set(jnp.concatenate([c_prompt, c_sample], axis=0))
    mod_all = _ada_mod(c_pad, ada_w, ada_b)
    cos_t, sin_t = _rope_tables(max(seq_lens))

    for l in range(depth):
        mod = mod_all[l, 0:n_seq].reshape(n_seq, 6, d).transpose(1, 0, 2).reshape(6, n_seq, 1, d)
        i = l // 2
        if l % 2 == 0:
            tm = cfg["tm"]
            q, k, v, p = _inproj(x, metas[tm], mod, norm_mix_g[l], ab_w_in[i].astype(BF16),
                                 q_norm_g[i], k_norm_g[i], cos_t, sin_t, tm)
            sink_b = jnp.broadcast_to(
                jnp.repeat(attn_sink[i].reshape(N_KV_HEADS, GQA_GROUP), ATT_BLOCK, axis=1)[:, :, None],
                (N_KV_HEADS, GQA_GROUP * ATT_BLOCK, LANES)).astype(F32)
            a = _attention(q, k, v, sink_b, metas[tm], tm)
            x = _mix0(x, a, p, metas[tm], mod, pool_w[i].astype(BF16), pool_scale[i],
                      ab_w_out[i].astype(BF16), tm)
        else:
            tm = cfg["tm_gmlp"]
            u, v = _gmlp_in(x, metas[tm], mod, norm_mix_g[l], c_w_in[i].astype(BF16),
                            sgu_ln_g[i], sgu_ln_b[i], tm)
            sgu_b_b = jnp.broadcast_to(sgu_b[i][:, :, None], (SGU_HEADS, CHUNK, LANES)).astype(F32)
            x = _gmlp_out(x, u, v, metas[tm], mod, sgu_w[i].astype(BF16), sgu_b_b,
                          c_w_out[i].astype(BF16), tm)
        x = _moe(x, metas, mod, norm_ffn_g[l], router_group_w[l], router_group_b[l],
                 router_expert_w[l], router_expert_b[l], exp_w_gate[l], exp_w_up[l], exp_w_down[l], cfg)

    y_prompt = x[0:bp * sp].reshape(bp, sp, d)
    y_sample = x[bp * sp:].reshape(bs, ss, d)
    return (y_prompt, y_sample)


def kernel(x_prompt, x_sample, c_prompt, c_sample, ada_w, ada_b, norm_mix_g, norm_ffn_g, ab_w_in, q_norm_g,
           k_norm_g, attn_sink, pool_w, pool_scale, ab_w_out, c_w_in, sgu_ln_g, sgu_ln_b, sgu_w, sgu_b,
           c_w_out, router_group_w, router_group_b, router_expert_w, router_expert_b, exp_w_gate, exp_w_up,
           exp_w_down):
    return _forward(x_prompt, x_sample, c_prompt, c_sample, ada_w, ada_b, norm_mix_g, norm_ffn_g, ab_w_in,
                    q_norm_g, k_norm_g, attn_sink, pool_w, pool_scale, ab_w_out, c_w_in, sgu_ln_g, sgu_ln_b,
                    sgu_w, sgu_b, c_w_out, router_group_w, router_group_b, router_expert_w, router_expert_b,
                    exp_w_gate, exp_w_up, exp_w_down)
```

```python
import functools

import numpy as np
import jax
import jax.numpy as jnp
from jax import lax
from jax.experimental import pallas as pl
from jax.experimental.pallas import tpu as pltpu

HEAD_DIM = 128
N_HEADS = 8
N_KV_HEADS = 2
GQA_GROUP = N_HEADS // N_KV_HEADS
ATT_BLOCK = 128
ROPE_THETA = 10000.0
Q_W = N_HEADS * HEAD_DIM
KV_W = N_KV_HEADS * HEAD_DIM
POOL_WINDOWS = (2, 4, 8, 16)
POOL_HALO = 8
CHUNK = 128
SGU_HEADS = 8
N_GROUPS = 4
EXP_PER_GROUP = 8
N_EXPERTS = N_GROUPS * EXP_PER_GROUP
EPS = 1e-6

VMEM_LIMIT_BYTES = 56 * 1024 * 1024
LANES = 128

F32 = jnp.float32
BF16 = jnp.bfloat16
NEG_BIG = -1e30


def _params(sem):
    return pltpu.CompilerParams(dimension_semantics=sem, vmem_limit_bytes=VMEM_LIMIT_BYTES)


def _resident(shape):
    nd = len(shape)
    return pl.BlockSpec(shape, lambda *_: (0,) * nd, pipeline_mode=pl.Buffered(1))


def _rows(tm, width):
    return pl.BlockSpec((tm, width), lambda i, *_: (i, 0))


def _mod_spec(part, d):
    return pl.BlockSpec((None, None, 1, d), lambda i, ts, *_: (part, ts[i], 0, 0))


def _norm_mod(x, g, sc, sh):
    r = lax.rsqrt(jnp.mean(x * x, axis=-1, keepdims=True) + EPS)
    return x * r * g * (1.0 + sc) + sh


def _dot(a, b):
    return jnp.dot(a, b, preferred_element_type=F32)


def _dot_nt(a, b):
    return lax.dot_general(a, b, (((1,), (1,)), ((), ())), preferred_element_type=F32)


def _ada_kernel(c_ref, w_ref, b_ref, o_ref):
    c = c_ref[...]
    cs = c * (1.0 / (1.0 + jnp.exp(-c)))
    o_ref[...] = _dot(cs.astype(BF16), w_ref[...].astype(BF16)) + b_ref[...]


def _ada_mod(c_pad, ada_w, ada_b):
    depth, d, n = ada_w.shape
    tn = 1024
    return pl.pallas_call(
        _ada_kernel,
        grid=(depth, n // tn),
        in_specs=[
            pl.BlockSpec(c_pad.shape, lambda l, j: (0, 0)),
            pl.BlockSpec((None, d, tn), lambda l, j: (l, 0, j)),
            pl.BlockSpec((None, 1, tn), lambda l, j: (l, 0, j)),
        ],
        out_specs=pl.BlockSpec((None, c_pad.shape[0], tn), lambda l, j: (l, 0, j)),
        out_shape=jax.ShapeDtypeStruct((depth, c_pad.shape[0], n), F32),
        compiler_params=_params(("arbitrary", "arbitrary")),
        name="ada_mod",
    )(c_pad, ada_w, ada_b.reshape(depth, 1, n))


def _inproj_kernel(ts, tp, x_ref, g_ref, sc_ref, sh_ref, w_ref, qg_ref, kg_ref, cos_ref, sin_ref,
                   q_ref, k_ref, v_ref, p_ref):
    h = _norm_mod(x_ref[...], g_ref[...], sc_ref[...], sh_ref[...]).astype(BF16)
    cos = cos_ref[...]
    sin = sin_ref[...]

    def head_norm_rope(y, gain):
        r = lax.rsqrt(jnp.mean(y * y, axis=-1, keepdims=True) + EPS)
        y = y * r * gain
        return y * cos + pltpu.roll(y, HEAD_DIM // 2, 1) * sin

    q = _dot(h, w_ref[:, 0:Q_W])
    for hh in range(N_HEADS):
        sl = slice(hh * HEAD_DIM, (hh + 1) * HEAD_DIM)
        q_ref[:, sl] = head_norm_rope(q[:, sl], qg_ref[...]).astype(BF16)
    kv = _dot(h, w_ref[:, Q_W:Q_W + 2 * KV_W])
    for hh in range(N_KV_HEADS):
        sl = slice(hh * HEAD_DIM, (hh + 1) * HEAD_DIM)
        k_ref[:, sl] = head_norm_rope(kv[:, sl], kg_ref[...]).astype(BF16)
    v_ref[...] = kv[:, KV_W:].astype(BF16)
    p_ref[...] = _dot(h, w_ref[:, Q_W + 2 * KV_W:])


def _inproj(x, meta, mod, norm_g, w_in_bf, q_g, k_g, cos_t, sin_t, tm):
    t, d = x.shape
    pool_w = w_in_bf.shape[1] - Q_W - 2 * KV_W
    n_tiles = t // tm
    rope_spec = pl.BlockSpec((tm, HEAD_DIM), lambda i, ts, tp: (tp[i] // tm, 0))
    grid_spec = pltpu.PrefetchScalarGridSpec(
        num_scalar_prefetch=2,
        grid=(n_tiles,),
        in_specs=[
            _rows(tm, d),
            _resident((1, d)),
            _mod_spec(1, d),
            _mod_spec(0, d),
            _resident(w_in_bf.shape),
            _resident((1, HEAD_DIM)),
            _resident((1, HEAD_DIM)),
            rope_spec,
            rope_spec,
        ],
        out_specs=[_rows(tm, Q_W), _rows(tm, KV_W), _rows(tm, KV_W), _rows(tm, pool_w)],
    )
    return pl.pallas_call(
        _inproj_kernel,
        grid_spec=grid_spec,
        out_shape=[
            jax.ShapeDtypeStruct((t, Q_W), BF16),
            jax.ShapeDtypeStruct((t, KV_W), BF16),
            jax.ShapeDtypeStruct((t, KV_W), BF16),
            jax.ShapeDtypeStruct((t, pool_w), F32),
        ],
        compiler_params=_params(("arbitrary",)),
        name="attn_pool_inproj",
    )(meta["seq"], meta["pos"], x, norm_g.reshape(1, d), mod, mod, w_in_bf,
      q_g.reshape(1, HEAD_DIM), k_g.reshape(1, HEAD_DIM), cos_t, sin_t)


def _attn_kernel(tp, tl, q_ref, kc_ref, kp_ref, kn_ref, vc_ref, vp_ref, vn_ref, sink_ref, o_ref, kx, vx):
    i = pl.program_id(0)
    tm = q_ref.shape[0]
    nb = tm // ATT_BLOCK
    first = tp[i] == 0
    last = tp[i] + tm == tl[i]
    kx[0:ATT_BLOCK] = kp_ref[...]
    kx[ATT_BLOCK:ATT_BLOCK + tm] = kc_ref[...]
    kx[ATT_BLOCK + tm:] = kn_ref[...]
    vx[0:ATT_BLOCK] = vp_ref[...]
    vx[ATT_BLOCK:ATT_BLOCK + tm] = vc_ref[...]
    vx[ATT_BLOCK + tm:] = vn_ref[...]
    win = 3 * ATT_BLOCK
    qi = lax.broadcasted_iota(jnp.int32, (ATT_BLOCK, win), 0)
    kj = lax.broadcasted_iota(jnp.int32, (ATT_BLOCK, win), 1)
    band = (kj >= qi) & (kj <= qi + 2 * ATT_BLOCK)
    scale = HEAD_DIM ** -0.5
    for b in range(nb):
        valid = band
        if b == 0:
            valid = valid & (kj >= jnp.where(first, ATT_BLOCK, 0))
        if b == nb - 1:
            valid = valid & (kj < jnp.where(last, 2 * ATT_BLOCK, win))
        bias = jnp.where(valid, 0.0, -jnp.inf).astype(F32)
        rows = slice(b * ATT_BLOCK, (b + 1) * ATT_BLOCK)
        for kk in range(N_KV_HEADS):
            cols = slice(kk * HEAD_DIM, (kk + 1) * HEAD_DIM)
            kw = kx[b * ATT_BLOCK:b * ATT_BLOCK + win, cols]
            vw = vx[b * ATT_BLOCK:b * ATT_BLOCK + win, cols]
            heads = [kk * GQA_GROUP + g for g in range(GQA_GROUP)]
            qs = jnp.concatenate([q_ref[rows, hd * HEAD_DIM:(hd + 1) * HEAD_DIM] for hd in heads], axis=0)
            s = _dot_nt(qs, kw) * scale
            s = (s.reshape(GQA_GROUP, ATT_BLOCK, win) + bias[None]).reshape(GQA_GROUP * ATT_BLOCK, win)
            sk = sink_ref[kk][:, 0:1]
            m = jnp.maximum(jnp.max(s, axis=-1, keepdims=True), sk)
            p = jnp.exp(s - m)
            denom = jnp.sum(p, axis=-1, keepdims=True) + jnp.exp(sk - m)
            o = _dot(p.astype(BF16), vw) / denom
            for g, hd in enumerate(heads):
                o_ref[rows, hd * HEAD_DIM:(hd + 1) * HEAD_DIM] = (
                    o[g * ATT_BLOCK:(g + 1) * ATT_BLOCK].astype(BF16))


def _attention(q, k, v, sink_b, meta, tm):
    t = q.shape[0]
    n_tiles = t // tm
    r = tm // ATT_BLOCK
    n_blk = t // ATT_BLOCK
    cur = pl.BlockSpec((tm, KV_W), lambda i, *_: (i, 0))
    prev = pl.BlockSpec((ATT_BLOCK, KV_W), lambda i, *_: (jnp.maximum(i * r - 1, 0), 0))
    nxt = pl.BlockSpec((ATT_BLOCK, KV_W), lambda i, *_: (jnp.minimum((i + 1) * r, n_blk - 1), 0))
    grid_spec = pltpu.PrefetchScalarGridSpec(
        num_scalar_prefetch=2,
        grid=(n_tiles,),
        in_specs=[_rows(tm, Q_W), cur, prev, nxt, cur, prev, nxt, _resident(sink_b.shape)],
        out_specs=_rows(tm, Q_W),
        scratch_shapes=[pltpu.VMEM((tm + 2 * ATT_BLOCK, KV_W), BF16),
                        pltpu.VMEM((tm + 2 * ATT_BLOCK, KV_W), BF16)],
    )
    return pl.pallas_call(
        _attn_kernel,
        grid_spec=grid_spec,
        out_shape=jax.ShapeDtypeStruct((t, Q_W), BF16),
        compiler_params=_params(("arbitrary",)),
        name="banded_attention",
    )(meta["pos"], meta["len"], q, k, k, k, v, v, v, sink_b)


def _mix0_kernel(tp, tl, ts, x_ref, a_ref, pc_ref, pp_ref, pn_ref, gate_ref, pw_ref, ps_ref, wo_ref,
                 o_ref, pext):
    i = pl.program_id(0)
    tm = x_ref.shape[0]
    first = tp[i] == 0
    last = tp[i] + tm == tl[i]
    pext[0:POOL_HALO] = jnp.where(first, 0.0, pp_ref[...])
    pext[POOL_HALO:POOL_HALO + tm] = pc_ref[...]
    pext[POOL_HALO + tm:] = jnp.where(last, 0.0, pn_ref[...])
    n_g = len(POOL_WINDOWS)
    gw = pc_ref.shape[1] // n_g
    pos = tp[i] + lax.broadcasted_iota(jnp.int32, (tm, gw), 0)
    seq_len = tl[i]
    ms = []
    for g, w in enumerate(POOL_WINDOWS):
        cols = slice(g * gw, (g + 1) * gw)
        acc = pext[POOL_HALO - w // 2:POOL_HALO - w // 2 + tm, cols]
        for off in range(-w // 2 + 1, w // 2):
            acc = acc + pext[POOL_HALO + off:POOL_HALO + off + tm, cols]
        cnt = (jnp.minimum(pos + w // 2, seq_len) - jnp.maximum(pos - w // 2, 0)).astype(F32)
        dlt = acc / cnt - pc_ref[:, cols]
        ms.append((_dot(dlt.astype(BF16), pw_ref[g]) * ps_ref[:, cols]).astype(BF16))
    m = jnp.concatenate(ms, axis=1)
    n_a = a_ref.shape[1]
    mix = _dot(a_ref[...], wo_ref[0:n_a, :]) + _dot(m, wo_ref[n_a:, :])
    o_ref[...] = x_ref[...] + gate_ref[...] * mix


def _mix0(x, a, p, meta, mod, pool_w_bf, pool_scale, w_out_bf, tm):
    t, d = x.shape
    pw = p.shape[1]
    n_tiles = t // tm
    r = tm // POOL_HALO
    n_hb = t // POOL_HALO
    prev = pl.BlockSpec((POOL_HALO, pw), lambda i, *_: (jnp.maximum(i * r - 1, 0), 0))
    nxt = pl.BlockSpec((POOL_HALO, pw), lambda i, *_: (jnp.minimum((i + 1) * r, n_hb - 1), 0))
    gate_spec = pl.BlockSpec((None, None, 1, d), lambda i, tp, tl, ts: (2, ts[i], 0, 0))
    grid_spec = pltpu.PrefetchScalarGridSpec(
        num_scalar_prefetch=3,
        grid=(n_tiles,),
        in_specs=[_rows(tm, d), _rows(tm, a.shape[1]), _rows(tm, pw), prev, nxt, gate_spec,
                  _resident(pool_w_bf.shape), _resident((1, pw)), _resident(w_out_bf.shape)],
        out_specs=_rows(tm, d),
        scratch_shapes=[pltpu.VMEM((tm + 2 * POOL_HALO, pw), F32)],
    )
    return pl.pallas_call(
        _mix0_kernel,
        grid_spec=grid_spec,
        out_shape=jax.ShapeDtypeStruct((t, d), F32),
        compiler_params=_params(("arbitrary",)),
        name="pool_outproj",
    )(meta["pos"], meta["len"], meta["seq"], x, a, p, p, p, mod, pool_w_bf,
      pool_scale.reshape(1, pw), w_out_bf)


def _gelu(z):
    return 0.5 * z * (1.0 + lax.erf(z * np.float32(np.sqrt(0.5))))


def _gmlp_in_kernel(ts, x_ref, g_ref, sc_ref, sh_ref, w_ref, lg_ref, lb_ref, u_ref, v_ref):
    h = _norm_mod(x_ref[...], g_ref[...], sc_ref[...], sh_ref[...]).astype(BF16)
    half = u_ref.shape[1]
    u_ref[...] = _gelu(_dot(h, w_ref[:, 0:half]))
    zv = _gelu(_dot(h, w_ref[:, half:]))
    zc = zv - jnp.mean(zv, axis=-1, keepdims=True)
    r = lax.rsqrt(jnp.mean(zc * zc, axis=-1, keepdims=True) + EPS)
    v_ref[...] = (zc * r * lg_ref[...] + lb_ref[...]).astype(BF16)


def _gmlp_in(x, meta, mod, norm_g, w_in_bf, ln_g, ln_b, tm):
    t, d = x.shape
    half = w_in_bf.shape[1] // 2
    grid_spec = pltpu.PrefetchScalarGridSpec(
        num_scalar_prefetch=1,
        grid=(t // tm,),
        in_specs=[_rows(tm, d), _resident((1, d)), _mod_spec(1, d), _mod_spec(0, d),
                  _resident(w_in_bf.shape), _resident((1, half)), _resident((1, half))],
        out_specs=[_rows(tm, half), _rows(tm, half)],
    )
    return pl.pallas_call(
        _gmlp_in_kernel,
        grid_spec=grid_spec,
        out_shape=[jax.ShapeDtypeStruct((t, half), F32), jax.ShapeDtypeStruct((t, half), BF16)],
        compiler_params=_params(("arbitrary",)),
        name="gmlp_in",
    )(meta["seq"], x, norm_g.reshape(1, d), mod, mod, w_in_bf, ln_g.reshape(1, half), ln_b.reshape(1, half))


def _gmlp_out_kernel(ts, x_ref, u_ref, v_ref, gate_ref, sw_ref, sb_ref, wo_ref, o_ref, gated):
    tm = x_ref.shape[0]
    hd = u_ref.shape[1] // SGU_HEADS
    for c in range(tm // CHUNK):
        rows = slice(c * CHUNK, (c + 1) * CHUNK)
        for hh in range(SGU_HEADS):
            cols = slice(hh * hd, (hh + 1) * hd)
            s = _dot(sw_ref[hh], v_ref[rows, cols]) + jnp.tile(sb_ref[hh], (1, hd // LANES))
            gated[rows, cols] = (u_ref[rows, cols] * s).astype(BF16)
    o_ref[...] = x_ref[...] + gate_ref[...] * _dot(gated[...], wo_ref[...])


def _gmlp_out(x, u, v, meta, mod, sgu_w_bf, sgu_b_b, w_out_bf, tm):
    t, d = x.shape
    w = u.shape[1]
    gate_spec = pl.BlockSpec((None, None, 1, d), lambda i, ts: (2, ts[i], 0, 0))
    grid_spec = pltpu.PrefetchScalarGridSpec(
        num_scalar_prefetch=1,
        grid=(t // tm,),
        in_specs=[_rows(tm, d), _rows(tm, w), _rows(tm, w), gate_spec,
                  _resident(sgu_w_bf.shape), _resident(sgu_b_b.shape), _resident(w_out_bf.shape)],
        out_specs=_rows(tm, d),
        scratch_shapes=[pltpu.VMEM((tm, w), BF16)],
    )
    return pl.pallas_call(
        _gmlp_out_kernel,
        grid_spec=grid_spec,
        out_shape=jax.ShapeDtypeStruct((t, d), F32),
        compiler_params=_params(("arbitrary",)),
        name="gmlp_out",
    )(meta["seq"], x, u, v, mod, sgu_w_bf, sgu_b_b, w_out_bf)


ROUTER_ROWS = 48


def _router_kernel(ts, x_ref, g_ref, sc_ref, sh_ref, wr_ref, br_ref, h_ref, ri_ref, rw_ref, cnt_ref, carry):
    i = pl.program_id(0)

    @pl.when(i == 0)
    def _():
        carry[...] = jnp.zeros_like(carry)

    tm = x_ref.shape[0]
    hb = _norm_mod(x_ref[...], g_ref[...], sc_ref[...], sh_ref[...]).astype(BF16)
    half = hb.shape[1] // 2
    bits = pltpu.bitcast(hb.astype(F32), jnp.uint32)
    h_ref[...] = (bits[:, :half] >> 16) | bits[:, half:]
    lg = _dot_nt(wr_ref[...], hb) + jnp.tile(br_ref[...], (1, tm // LANES))
    rows8 = lax.broadcasted_iota(jnp.int32, (EXP_PER_GROUP, tm), 0)

    def first_argmax(vals, vmax):
        return jnp.min(jnp.where(vals == vmax, rows8, EXP_PER_GROUP), axis=0, keepdims=True)

    gl = lg[0:8]
    gmax = jnp.max(gl, axis=0, keepdims=True)
    gidx = first_argmax(gl, gmax)
    g_w = 1.0 / jnp.sum(jnp.exp(gl - gmax), axis=0, keepdims=True)
    esel = jnp.zeros((EXP_PER_GROUP, tm), F32)
    for g in range(N_GROUPS):
        esel = jnp.where(gidx == g, lg[8 + g * EXP_PER_GROUP:8 + (g + 1) * EXP_PER_GROUP], esel)
    emax = jnp.max(esel, axis=0, keepdims=True)
    pe = jnp.exp(esel - emax)
    prob = pe / jnp.sum(pe, axis=0, keepdims=True)
    p1 = jnp.max(prob, axis=0, keepdims=True)
    i1 = first_argmax(prob, p1)
    rest = jnp.where(rows8 == i1, -1.0, prob)
    p2 = jnp.max(rest, axis=0, keepdims=True)
    i2 = first_argmax(rest, p2)
    den = p1 + p2
    w0 = g_w * (p1 / den)
    w1 = g_w * (p2 / den)
    e0 = gidx * EXP_PER_GROUP + i1
    e1 = gidx * EXP_PER_GROUP + i2

    rows_e = lax.broadcasted_iota(jnp.int32, (N_EXPERTS, tm), 0)
    oh0 = rows_e == e0
    oh1 = rows_e == e1
    both = jnp.where(oh0, 1.0, 0.0) + jnp.where(oh1, 1.0, 0.0)
    ri = lax.broadcasted_iota(jnp.int32, (tm, tm), 0)
    ci = lax.broadcasted_iota(jnp.int32, (tm, tm), 1)
    upper = jnp.where(ri < ci, 1.0, 0.0).astype(BF16)
    before = _dot(both.astype(BF16), upper) + carry[:, 0:1]
    r0 = jnp.sum(jnp.where(oh0, before, 0.0), axis=0, keepdims=True)
    r1 = jnp.sum(jnp.where(oh1, before, 0.0), axis=0, keepdims=True)
    new_cnt = carry[...] + jnp.sum(both, axis=1, keepdims=True)
    carry[...] = new_cnt
    cnt_ref[...] = new_cnt
    zi = jnp.zeros((4, tm), jnp.int32)
    ri_ref[...] = jnp.concatenate([e0, e1, r0.astype(jnp.int32), r1.astype(jnp.int32), zi], axis=0)
    rw_ref[...] = jnp.concatenate([w0, w1, jnp.zeros((6, tm), F32)], axis=0)


def _router(x, meta, mod, norm_g, wr_bf, br_b, tm):
    t, d = x.shape
    grid_spec = pltpu.PrefetchScalarGridSpec(
        num_scalar_prefetch=1,
        grid=(t // tm,),
        in_specs=[_rows(tm, d), _resident((1, d)), _mod_spec(4, d), _mod_spec(3, d),
                  _resident(wr_bf.shape), _resident(br_b.shape)],
        out_specs=[_rows(tm, d // 2),
                   pl.BlockSpec((8, tm), lambda i, *_: (0, i)),
                   pl.BlockSpec((8, tm), lambda i, *_: (0, i)),
                   pl.BlockSpec((N_EXPERTS, LANES), lambda i, *_: (0, 0))],
        scratch_shapes=[pltpu.VMEM((N_EXPERTS, LANES), F32)],
    )
    return pl.pallas_call(
        _router_kernel,
        grid_spec=grid_spec,
        out_shape=[jax.ShapeDtypeStruct((t, d // 2), jnp.uint32),
                   jax.ShapeDtypeStruct((8, t), jnp.int32),
                   jax.ShapeDtypeStruct((8, t), F32),
                   jax.ShapeDtypeStruct((N_EXPERTS, LANES), F32)],
        compiler_params=_params(("arbitrary",)),
        name="moe_router",
    )(meta["seq"], x, norm_g.reshape(1, d), mod, mod, wr_bf, br_b)


ROW_UNROLL = 8


def _row_copy(src, s_row, dst, d_row, sem):
    return pltpu.make_async_copy(src.at[pl.ds(s_row, 1)], dst.at[pl.ds(d_row, 1)], sem)


def _dispatch_kernel(pad_lo, pad_hi, slot_ref, hp_hbm, xs_hbm, zrow, sem, zsem):
    i = pl.program_id(0)
    ch = slot_ref.shape[2] // 2

    @pl.when(i == 0)
    def _():
        zrow[...] = jnp.zeros_like(zrow)

        def per_expert(e, c):
            lax.fori_loop(pad_lo[e], pad_hi[e],
                          lambda r, c2: (_row_copy(zrow, 0, xs_hbm, r, zsem).start(), c2)[1], 0)
            lax.fori_loop(pad_lo[e], pad_hi[e],
                          lambda r, c2: (_row_copy(zrow, 0, xs_hbm, r, zsem).wait(), c2)[1], 0)
            return c

        lax.fori_loop(0, N_EXPERTS, per_expert, 0)

    base = i * ch

    def issue(q, c):
        for u in range(ROW_UNROLL):
            r = q * ROW_UNROLL + u
            for k in range(2):
                _row_copy(hp_hbm, base + r, xs_hbm, slot_ref[0, 0, k * ch + r], sem).start()
        return c

    lax.fori_loop(0, ch // ROW_UNROLL, issue, 0)
    pltpu.make_async_copy(hp_hbm.at[pl.ds(0, 2 * ch)], xs_hbm.at[pl.ds(0, 2 * ch)], sem).wait()


def _dispatch(hp, slots, pad_lo, pad_hi, n_rows, ch):
    t, w = hp.shape
    grid_spec = pltpu.PrefetchScalarGridSpec(
        num_scalar_prefetch=2,
        grid=(t // ch,),
        in_specs=[
            pl.BlockSpec((1, 1, 2 * ch), lambda i, *_: (i, 0, 0), memory_space=pltpu.SMEM),
            pl.BlockSpec(memory_space=pl.ANY),
        ],
        out_specs=pl.BlockSpec(memory_space=pl.ANY),
        scratch_shapes=[pltpu.VMEM((8, w), hp.dtype), pltpu.SemaphoreType.DMA, pltpu.SemaphoreType.DMA],
    )
    return pl.pallas_call(
        _dispatch_kernel,
        grid_spec=grid_spec,
        out_shape=jax.ShapeDtypeStruct((n_rows, w), hp.dtype),
        compiler_params=_params(("arbitrary",)),
        name="moe_dispatch",
    )(pad_lo, pad_hi, slots, hp)


def _ffn_kernel(be, nu, xs_ref, wg_ref, wu_ref, wd_ref, o_ref, xb):
    i = pl.program_id(0)
    j = pl.program_id(1)

    @pl.when((i >= nu[0]) & (j == 0))
    def _():
        o_ref[...] = jnp.zeros_like(o_ref)

    @pl.when(i < nu[0])
    def _():
        @pl.when(j == 0)
        def _():
            xp = xs_ref[...]
            half = xp.shape[1]
            xb[:, 0:half] = pltpu.bitcast(xp << 16, F32).astype(BF16)
            xb[:, half:] = pltpu.bitcast(xp & jnp.uint32(0xFFFF0000), F32).astype(BF16)

        x = xb[...]
        hg = _dot(x, wg_ref[...].astype(BF16))
        hu = _dot(x, wu_ref[...].astype(BF16))
        act = (hg * (1.0 / (1.0 + jnp.exp(-hg))) * hu).astype(BF16)
        part = _dot(act, wd_ref[...].astype(BF16))

        @pl.when(j == 0)
        def _():
            o_ref[...] = part

        @pl.when(j > 0)
        def _():
            o_ref[...] += part


def _expert_ffn(xs, blk_expert, n_used, layer, wg, wu, wd, rb, ft):
    n_rows, w = xs.shape
    d = 2 * w
    n_blocks = n_rows // rb
    ff = wg.shape[3]
    n_ff = ff // ft

    def ffj(i, j, nu):
        return jnp.where(i < nu[0], j, n_ff - 1)

    grid_spec = pltpu.PrefetchScalarGridSpec(
        num_scalar_prefetch=2,
        grid=(n_blocks, n_ff),
        in_specs=[
            pl.BlockSpec((rb, w), lambda i, j, be, nu: (i, 0)),
            pl.BlockSpec((None, None, d, ft), lambda i, j, be, nu: (layer, be[i], 0, ffj(i, j, nu))),
            pl.BlockSpec((None, None, d, ft), lambda i, j, be, nu: (layer, be[i], 0, ffj(i, j, nu))),
            pl.BlockSpec((None, None, ft, d), lambda i, j, be, nu: (layer, be[i], ffj(i, j, nu), 0)),
        ],
        out_specs=pl.BlockSpec((rb, d), lambda i, j, be, nu: (i, 0)),
        scratch_shapes=[pltpu.VMEM((rb, d), BF16)],
    )
    return pl.pallas_call(
        _ffn_kernel,
        grid_spec=grid_spec,
        out_shape=jax.ShapeDtypeStruct((n_rows, d), F32),
        compiler_params=_params(("arbitrary", "arbitrary")),
        name="expert_ffn",
    )(blk_expert, n_used, xs, wg, wu, wd)


def _combine_kernel(ts, idx_ref, nxt_ref, x_ref, wt_ref, gate_ref, ys_hbm, o_ref, buf, sem):
    i = pl.program_id(0)
    tm = x_ref.shape[0]
    cur = i % 2

    def fetch(idx, b):
        def issue(q, c):
            for u in range(ROW_UNROLL):
                r = q * ROW_UNROLL + u
                _row_copy(ys_hbm, idx[0, 0, r], buf.at[b], r, sem.at[b]).start()
            return c

        lax.fori_loop(0, 2 * tm // ROW_UNROLL, issue, 0)

    @pl.when(i == 0)
    def _():
        fetch(idx_ref, 0)

    @pl.when(i + 1 < pl.num_programs(0))
    def _():
        fetch(nxt_ref, 1 - cur)

    pltpu.make_async_copy(buf.at[cur], buf.at[cur], sem.at[cur]).wait()
    y = wt_ref[:, 0:1] * buf[cur, 0:tm] + wt_ref[:, 1:2] * buf[cur, tm:2 * tm]
    o_ref[...] = x_ref[...] + gate_ref[...] * y


def _combine(x, ys, slots, wt, meta, mod, tm):
    t, d = x.shape
    n_tiles = t // tm
    gate_spec = pl.BlockSpec((None, None, 1, d), lambda i, ts: (5, ts[i], 0, 0))
    grid_spec = pltpu.PrefetchScalarGridSpec(
        num_scalar_prefetch=1,
        grid=(n_tiles,),
        in_specs=[
            pl.BlockSpec((1, 1, 2 * tm), lambda i, ts: (i, 0, 0), memory_space=pltpu.SMEM),
            pl.BlockSpec((1, 1, 2 * tm), lambda i, ts: (jnp.minimum(i + 1, n_tiles - 1), 0, 0),
                         memory_space=pltpu.SMEM),
            _rows(tm, d),
            _rows(tm, wt.shape[1]),
            gate_spec,
            pl.BlockSpec(memory_space=pl.ANY),
        ],
        out_specs=_rows(tm, d),
        scratch_shapes=[pltpu.VMEM((2, 2 * tm, d), F32), pltpu.SemaphoreType.DMA((2,))],
    )
    return pl.pallas_call(
        _combine_kernel,
        grid_spec=grid_spec,
        out_shape=jax.ShapeDtypeStruct((t, d), F32),
        compiler_params=_params(("arbitrary",)),
        name="moe_combine",
    )(meta["seq"], slots, slots, x, wt, mod, ys)


def _tile_meta(seq_lens, tm):
    seq, pos, ln = [], [], []
    for s, n in enumerate(seq_lens):
        assert n % tm == 0
        for k in range(n // tm):
            seq.append(s)
            pos.append(k * tm)
            ln.append(n)
    return {k: jnp.asarray(np.asarray(v, np.int32)) for k, v in (("seq", seq), ("pos", pos), ("len", ln))}


def _rope_tables(s_max):
    half = HEAD_DIM // 2
    inv = ROPE_THETA ** (-jnp.arange(half, dtype=F32) / half)
    ang = jnp.arange(s_max, dtype=F32)[:, None] * inv[None, :]
    cos, sin = jnp.cos(ang), jnp.sin(ang)
    return jnp.concatenate([cos, cos], axis=1), jnp.concatenate([-sin, sin], axis=1)


def _tile_slots(slot, tm):
    t = slot.shape[1]
    return slot.reshape(2, t // tm, tm).transpose(1, 0, 2).reshape(t // tm, 1, 2 * tm)


def _moe(x, metas, mod, norm_g, rg_w, rg_b, re_w, re_b, layer, wg, wu, wd, cfg):
    t, d = x.shape
    wr = jnp.zeros((ROUTER_ROWS, d), F32).at[0:N_GROUPS].set(rg_w.T).at[8:8 + N_EXPERTS].set(re_w.T)
    br = jnp.zeros((ROUTER_ROWS,), F32).at[N_GROUPS:8].set(NEG_BIG).at[0:N_GROUPS].set(rg_b)
    br = br.at[8:8 + N_EXPERTS].set(re_b)
    br_b = jnp.broadcast_to(br[:, None], (ROUTER_ROWS, LANES))
    hp, ri, rw, cnt = _router(x, metas[cfg["tm_router"]], mod, norm_g, wr.astype(BF16), br_b, cfg["tm_router"])

    rb = cfg["rb"]
    expert = ri[0:2]
    rank = ri[2:4]
    counts = cnt[:, 0].astype(jnp.int32)
    nblk = (counts + rb - 1) // rb
    blk_end = jnp.cumsum(nblk)
    row_start = (blk_end - nblk) * rb
    ids = jnp.arange(N_EXPERTS, dtype=jnp.int32)
    start_of = jnp.sum(jnp.where(expert[:, :, None] == ids, row_start, 0), axis=-1)
    slot = start_of + rank
    n_blocks = -(-2 * t // rb) + N_EXPERTS
    n_used = blk_end[-1]
    blk = jnp.arange(n_blocks, dtype=jnp.int32)
    blk_expert = jnp.minimum(jnp.sum(blk[:, None] >= blk_end[None, :], axis=1), N_EXPERTS - 1).astype(jnp.int32)
    last_used = jnp.sum(jnp.where(blk == n_used - 1, blk_expert, 0))
    blk_expert = jnp.where(blk < n_used, blk_expert, last_used)
    pad_lo = row_start + counts
    pad_hi = (blk_end * rb).at[N_EXPERTS - 1].set(n_blocks * rb)

    ch = cfg["ch_dispatch"]
    xs = _dispatch(hp, _tile_slots(slot, ch), pad_lo.astype(jnp.int32), pad_hi.astype(jnp.int32),
                   n_blocks * rb, ch)
    ys = _expert_ffn(xs, blk_expert, n_used.reshape(1).astype(jnp.int32), layer, wg, wu, wd, rb, cfg["ft"])

    tmc = cfg["tm_combine"]
    wt = rw.T
    return _combine(x, ys, _tile_slots(slot, tmc), wt, metas[tmc], mod, tmc)


def _config(seq_lens):
    g = int(np.gcd.reduce(np.asarray(seq_lens)))
    tm = min(512, g)
    return {"tm": tm, "tm_router": tm, "tm_combine": min(256, g), "tm_gmlp": min(256, g), "rb": 1024, "ft": 256,
            "ch_dispatch": min(1024, g)}


def _forward(x_prompt, x_sample, c_prompt, c_sample, ada_w, ada_b, norm_mix_g, norm_ffn_g,
             ab_w_in, q_norm_g, k_norm_g, attn_sink, pool_w, pool_scale, ab_w_out,
             c_w_in, sgu_ln_g, sgu_ln_b, sgu_w, sgu_b, c_w_out,
             router_group_w, router_group_b, router_expert_w, router_expert_b,
             exp_w_gate, exp_w_up, exp_w_down, cfg=None):
    bp, sp, d = x_prompt.shape
    bs, ss, _ = x_sample.shape
    seq_lens = [sp] * bp + [ss] * bs
    n_seq = len(seq_lens)
    if cfg is None:
        cfg = _config(seq_lens)
    depth = ada_w.shape[0]
    x = jnp.concatenate([x_prompt.reshape(bp * sp, d), x_sample.reshape(bs * ss, d)], axis=0)
    metas = {tm: _tile_meta(seq_lens, tm) for tm in
             {cfg["tm"], cfg["tm_router"], cfg["tm_combine"], cfg["tm_gmlp"]}}

    n_pad = -(-n_seq // 8) * 8
    c_pad = jnp.zeros((n_pad, d), F32).at[0:n_seq].set(jnp.concatenate([c_prompt, c_sample], axis=0))
    mod_all = _ada_mod(c_pad, ada_w, ada_b)
    cos_t, sin_t = _rope_tables(max(seq_lens))

    for l in range(depth):
        mod = mod_all[l, 0:n_seq].reshape(n_seq, 6, d).transpose(1, 0, 2).reshape(6, n_seq, 1, d)
        i = l // 2
        if l % 2 == 0:
            tm = cfg["tm"]
            q, k, v, p = _inproj(x, metas[tm], mod, norm_mix_g[l], ab_w_in[i].astype(BF16),
                                 q_norm_g[i], k_norm_g[i], cos_t, sin_t, tm)
            sink_b = jnp.broadcast_to(
                jnp.repeat(attn_sink[i].reshape(N_KV_HEADS, GQA_GROUP), ATT_BLOCK, axis=1)[:, :, None],
                (N_KV_HEADS, GQA_GROUP * ATT_BLOCK, LANES)).astype(F32)
            a = _attention(q, k, v, sink_b, metas[tm], tm)
            x = _mix0(x, a, p, metas[tm], mod, pool_w[i].astype(BF16), pool_scale[i],
                      ab_w_out[i].astype(BF16), tm)
        else:
            tm = cfg["tm_gmlp"]
            u, v = _gmlp_in(x, metas[tm], mod, norm_mix_g[l], c_w_in[i].astype(BF16),
                            sgu_ln_g[i], sgu_ln_b[i], tm)
            sgu_b_b = jnp.broadcast_to(sgu_b[i][:, :, None], (SGU_HEADS, CHUNK, LANES)).astype(F32)
            x = _gmlp_out(x, u, v, metas[tm], mod, sgu_w[i].astype(BF16), sgu_b_b,
                          c_w_out[i].astype(BF16), tm)
        x = _moe(x, metas, mod, norm_ffn_g[l], router_group_w[l], router_group_b[l],
                 router_expert_w[l], router_expert_b[l], l, exp_w_gate, exp_w_up, exp_w_down, cfg)

    y_prompt = x[0:bp * sp].reshape(bp, sp, d)
    y_sample = x[bp * sp:].reshape(bs, ss, d)
    return (y_prompt, y_sample)


def kernel(x_prompt, x_sample, c_prompt, c_sample, ada_w, ada_b, norm_mix_g, norm_ffn_g, ab_w_in, q_norm_g,
           k_norm_g, attn_sink, pool_w, pool_scale, ab_w_out, c_w_in, sgu_ln_g, sgu_ln_b, sgu_w, sgu_b,
           c_w_out, router_group_w, router_group_b, router_expert_w, router_expert_b, exp_w_gate, exp_w_up,
           exp_w_down):
    return _forward(x_prompt, x_sample, c_prompt, c_sample, ada_w, ada_b, norm_mix_g, norm_ffn_g, ab_w_in,
                    q_norm_g, k_norm_g, attn_sink, pool_w, pool_scale, ab_w_out, c_w_in, sgu_ln_g, sgu_ln_b,
                    sgu_w, sgu_b, c_w_out, router_group_w, router_group_b, router_expert_w, router_expert_b,
                    exp_w_gate, exp_w_up, exp_w_down)
```

```python
import functools

import numpy as np
import jax
import jax.numpy as jnp
from jax import lax
from jax.experimental import pallas as pl
from jax.experimental.pallas import tpu as pltpu

HEAD_DIM = 128
N_HEADS = 8
N_KV_HEADS = 2
GQA_GROUP = N_HEADS // N_KV_HEADS
ATT_BLOCK = 128
ROPE_THETA = 10000.0
Q_W = N_HEADS * HEAD_DIM
KV_W = N_KV_HEADS * HEAD_DIM
POOL_WINDOWS = (2, 4, 8, 16)
POOL_HALO = 8
CHUNK = 128
SGU_HEADS = 8
N_GROUPS = 4
EXP_PER_GROUP = 8
N_EXPERTS = N_GROUPS * EXP_PER_GROUP
EPS = 1e-6

VMEM_LIMIT_BYTES = 56 * 1024 * 1024
LANES = 128

F32 = jnp.float32
BF16 = jnp.bfloat16
NEG_BIG = -1e30


def _params(sem):
    return pltpu.CompilerParams(dimension_semantics=sem, vmem_limit_bytes=VMEM_LIMIT_BYTES)


def _resident(shape):
    nd = len(shape)
    return pl.BlockSpec(shape, lambda *_: (0,) * nd, pipeline_mode=pl.Buffered(1))


def _rows(tm, width):
    return pl.BlockSpec((tm, width), lambda i, *_: (i, 0))


def _split_rows(tm, width, n_a):
    return [pl.BlockSpec((tm, width), lambda i, *_: (jnp.minimum(i, n_a - 1), 0)),
            pl.BlockSpec((tm, width), lambda i, *_: (jnp.maximum(i - n_a, 0), 0))]


def _as_pair(x, tm):
    if isinstance(x, tuple):
        return x[0], x[1], x[0].shape[0] // tm
    return x, x, x.shape[0] // tm


def _mod_spec(part, d):
    return pl.BlockSpec((None, None, 1, d), lambda i, ts, *_: (part, ts[i], 0, 0))


def _norm_mod(x, g, sc, sh):
    r = lax.rsqrt(jnp.mean(x * x, axis=-1, keepdims=True) + EPS)
    return x * r * g * (1.0 + sc) + sh


def _dot(a, b):
    return jnp.dot(a, b, preferred_element_type=F32)


def _dot_nt(a, b):
    return lax.dot_general(a, b, (((1,), (1,)), ((), ())), preferred_element_type=F32)


def _ada_kernel(c_ref, w_ref, b_ref, o_ref):
    c = c_ref[...]
    cs = c * (1.0 / (1.0 + jnp.exp(-c)))
    o_ref[...] = _dot(cs.astype(BF16), w_ref[...].astype(BF16)) + b_ref[...]


def _ada_mod(c_pad, ada_w, ada_b):
    depth, d, n = ada_w.shape
    tn = 1024
    return pl.pallas_call(
        _ada_kernel,
        grid=(depth, n // tn),
        in_specs=[
            pl.BlockSpec(c_pad.shape, lambda l, j: (0, 0)),
            pl.BlockSpec((None, d, tn), lambda l, j: (l, 0, j)),
            pl.BlockSpec((None, 1, tn), lambda l, j: (l, 0, j)),
        ],
        out_specs=pl.BlockSpec((None, c_pad.shape[0], tn), lambda l, j: (l, 0, j)),
        out_shape=jax.ShapeDtypeStruct((depth, c_pad.shape[0], n), F32),
        compiler_params=_params(("arbitrary", "arbitrary")),
        name="ada_mod",
    )(c_pad, ada_w, ada_b.reshape(depth, 1, n))


def _inproj_kernel(ts, tp, xa_ref, xb_ref, g_ref, sc_ref, sh_ref, w_ref, qg_ref, kg_ref, cos_ref, sin_ref,
                   q_ref, k_ref, v_ref, p_ref, *, n_a):
    x = jnp.where(pl.program_id(0) < n_a, xa_ref[...], xb_ref[...])
    h = _norm_mod(x, g_ref[...], sc_ref[...], sh_ref[...]).astype(BF16)
    cos = cos_ref[...]
    sin = sin_ref[...]

    def head_norm_rope(y, gain):
        r = lax.rsqrt(jnp.mean(y * y, axis=-1, keepdims=True) + EPS)
        y = y * r * gain
        return y * cos + pltpu.roll(y, HEAD_DIM // 2, 1) * sin

    q = _dot(h, w_ref[:, 0:Q_W])
    for hh in range(N_HEADS):
        sl = slice(hh * HEAD_DIM, (hh + 1) * HEAD_DIM)
        q_ref[:, sl] = head_norm_rope(q[:, sl], qg_ref[...]).astype(BF16)
    kv = _dot(h, w_ref[:, Q_W:Q_W + 2 * KV_W])
    for hh in range(N_KV_HEADS):
        sl = slice(hh * HEAD_DIM, (hh + 1) * HEAD_DIM)
        k_ref[:, sl] = head_norm_rope(kv[:, sl], kg_ref[...]).astype(BF16)
    v_ref[...] = kv[:, KV_W:].astype(BF16)
    p_ref[...] = _dot(h, w_ref[:, Q_W + 2 * KV_W:])


def _inproj(x, meta, mod, norm_g, w_in_bf, q_g, k_g, cos_t, sin_t, tm):
    xa, xb, n_a = _as_pair(x, tm)
    d = xa.shape[1]
    n_tiles = meta["seq"].shape[0]
    t = n_tiles * tm
    pool_w = w_in_bf.shape[1] - Q_W - 2 * KV_W
    rope_spec = pl.BlockSpec((tm, HEAD_DIM), lambda i, ts, tp: (tp[i] // tm, 0))
    grid_spec = pltpu.PrefetchScalarGridSpec(
        num_scalar_prefetch=2,
        grid=(n_tiles,),
        in_specs=_split_rows(tm, d, n_a) + [
            _resident((1, d)),
            _mod_spec(1, d),
            _mod_spec(0, d),
            _resident(w_in_bf.shape),
            _resident((1, HEAD_DIM)),
            _resident((1, HEAD_DIM)),
            rope_spec,
            rope_spec,
        ],
        out_specs=[_rows(tm, Q_W), _rows(tm, KV_W), _rows(tm, KV_W), _rows(tm, pool_w)],
    )
    return pl.pallas_call(
        functools.partial(_inproj_kernel, n_a=n_a),
        grid_spec=grid_spec,
        out_shape=[
            jax.ShapeDtypeStruct((t, Q_W), BF16),
            jax.ShapeDtypeStruct((t, KV_W), BF16),
            jax.ShapeDtypeStruct((t, KV_W), BF16),
            jax.ShapeDtypeStruct((t, pool_w), F32),
        ],
        compiler_params=_params(("arbitrary",)),
        name="attn_pool_inproj",
    )(meta["seq"], meta["pos"], xa, xb, norm_g.reshape(1, d), mod, mod, w_in_bf,
      q_g.reshape(1, HEAD_DIM), k_g.reshape(1, HEAD_DIM), cos_t, sin_t)


def _attn_kernel(tp, tl, q_ref, kc_ref, kp_ref, kn_ref, vc_ref, vp_ref, vn_ref, sink_ref, o_ref, kx, vx):
    i = pl.program_id(0)
    tm = q_ref.shape[0]
    nb = tm // ATT_BLOCK
    first = tp[i] == 0
    last = tp[i] + tm == tl[i]
    kx[0:ATT_BLOCK] = kp_ref[...]
    kx[ATT_BLOCK:ATT_BLOCK + tm] = kc_ref[...]
    kx[ATT_BLOCK + tm:] = kn_ref[...]
    vx[0:ATT_BLOCK] = vp_ref[...]
    vx[ATT_BLOCK:ATT_BLOCK + tm] = vc_ref[...]
    vx[ATT_BLOCK + tm:] = vn_ref[...]
    win = 3 * ATT_BLOCK
    qi = lax.broadcasted_iota(jnp.int32, (ATT_BLOCK, win), 0)
    kj = lax.broadcasted_iota(jnp.int32, (ATT_BLOCK, win), 1)
    band = (kj >= qi) & (kj <= qi + 2 * ATT_BLOCK)
    scale = HEAD_DIM ** -0.5
    for b in range(nb):
        valid = band
        if b == 0:
            valid = valid & (kj >= jnp.where(first, ATT_BLOCK, 0))
        if b == nb - 1:
            valid = valid & (kj < jnp.where(last, 2 * ATT_BLOCK, win))
        bias = jnp.where(valid, 0.0, -jnp.inf).astype(F32)
        rows = slice(b * ATT_BLOCK, (b + 1) * ATT_BLOCK)
        for kk in range(N_KV_HEADS):
            cols = slice(kk * HEAD_DIM, (kk + 1) * HEAD_DIM)
            kw = kx[b * ATT_BLOCK:b * ATT_BLOCK + win, cols]
            vw = vx[b * ATT_BLOCK:b * ATT_BLOCK + win, cols]
            heads = [kk * GQA_GROUP + g for g in range(GQA_GROUP)]
            qs = jnp.concatenate([q_ref[rows, hd * HEAD_DIM:(hd + 1) * HEAD_DIM] for hd in heads], axis=0)
            s = _dot_nt(qs, kw) * scale
            s = (s.reshape(GQA_GROUP, ATT_BLOCK, win) + bias[None]).reshape(GQA_GROUP * ATT_BLOCK, win)
            sk = sink_ref[kk][:, 0:1]
            m = jnp.maximum(jnp.max(s, axis=-1, keepdims=True), sk)
            p = jnp.exp(s - m)
            denom = jnp.sum(p, axis=-1, keepdims=True) + jnp.exp(sk - m)
            o = _dot(p.astype(BF16), vw) / denom
            for g, hd in enumerate(heads):
                o_ref[rows, hd * HEAD_DIM:(hd + 1) * HEAD_DIM] = (
                    o[g * ATT_BLOCK:(g + 1) * ATT_BLOCK].astype(BF16))


def _attention(q, k, v, sink_b, meta, tm):
    t = q.shape[0]
    n_tiles = t // tm
    r = tm // ATT_BLOCK
    n_blk = t // ATT_BLOCK
    cur = pl.BlockSpec((tm, KV_W), lambda i, *_: (i, 0))
    prev = pl.BlockSpec((ATT_BLOCK, KV_W), lambda i, *_: (jnp.maximum(i * r - 1, 0), 0))
    nxt = pl.BlockSpec((ATT_BLOCK, KV_W), lambda i, *_: (jnp.minimum((i + 1) * r, n_blk - 1), 0))
    grid_spec = pltpu.PrefetchScalarGridSpec(
        num_scalar_prefetch=2,
        grid=(n_tiles,),
        in_specs=[_rows(tm, Q_W), cur, prev, nxt, cur, prev, nxt, _resident(sink_b.shape)],
        out_specs=_rows(tm, Q_W),
        scratch_shapes=[pltpu.VMEM((tm + 2 * ATT_BLOCK, KV_W), BF16),
                        pltpu.VMEM((tm + 2 * ATT_BLOCK, KV_W), BF16)],
    )
    return pl.pallas_call(
        _attn_kernel,
        grid_spec=grid_spec,
        out_shape=jax.ShapeDtypeStruct((t, Q_W), BF16),
        compiler_params=_params(("arbitrary",)),
        name="banded_attention",
    )(meta["pos"], meta["len"], q, k, k, k, v, v, v, sink_b)


def _mix0_kernel(tp, tl, ts, xa_ref, xb_ref, a_ref, pc_ref, pp_ref, pn_ref, gate_ref, pw_ref, ps_ref, wo_ref,
                 o_ref, pext, *, n_a):
    i = pl.program_id(0)
    tm = xa_ref.shape[0]
    first = tp[i] == 0
    last = tp[i] + tm == tl[i]
    pext[0:POOL_HALO] = jnp.where(first, 0.0, pp_ref[...])
    pext[POOL_HALO:POOL_HALO + tm] = pc_ref[...]
    pext[POOL_HALO + tm:] = jnp.where(last, 0.0, pn_ref[...])
    n_g = len(POOL_WINDOWS)
    gw = pc_ref.shape[1] // n_g
    pos = tp[i] + lax.broadcasted_iota(jnp.int32, (tm, gw), 0)
    seq_len = tl[i]
    ms = []
    for g, w in enumerate(POOL_WINDOWS):
        cols = slice(g * gw, (g + 1) * gw)
        acc = pext[POOL_HALO - w // 2:POOL_HALO - w // 2 + tm, cols]
        for off in range(-w // 2 + 1, w // 2):
            acc = acc + pext[POOL_HALO + off:POOL_HALO + off + tm, cols]
        cnt = (jnp.minimum(pos + w // 2, seq_len) - jnp.maximum(pos - w // 2, 0)).astype(F32)
        dlt = acc / cnt - pc_ref[:, cols]
        ms.append((_dot(dlt.astype(BF16), pw_ref[g]) * ps_ref[:, cols]).astype(BF16))
    m = jnp.concatenate(ms, axis=1)
    w_a = a_ref.shape[1]
    mix = _dot(a_ref[...], wo_ref[0:w_a, :]) + _dot(m, wo_ref[w_a:, :])
    x = jnp.where(i < n_a, xa_ref[...], xb_ref[...])
    o_ref[...] = x + gate_ref[...] * mix


def _mix0(x, a, p, meta, mod, pool_w_bf, pool_scale, w_out_bf, tm):
    xa, xb, n_a = _as_pair(x, tm)
    d = xa.shape[1]
    t, pw = p.shape
    n_tiles = t // tm
    r = tm // POOL_HALO
    n_hb = t // POOL_HALO
    prev = pl.BlockSpec((POOL_HALO, pw), lambda i, *_: (jnp.maximum(i * r - 1, 0), 0))
    nxt = pl.BlockSpec((POOL_HALO, pw), lambda i, *_: (jnp.minimum((i + 1) * r, n_hb - 1), 0))
    gate_spec = pl.BlockSpec((None, None, 1, d), lambda i, tp, tl, ts: (2, ts[i], 0, 0))
    grid_spec = pltpu.PrefetchScalarGridSpec(
        num_scalar_prefetch=3,
        grid=(n_tiles,),
        in_specs=_split_rows(tm, d, n_a) + [
            _rows(tm, a.shape[1]), _rows(tm, pw), prev, nxt, gate_spec,
            _resident(pool_w_bf.shape), _resident((1, pw)), _resident(w_out_bf.shape)],
        out_specs=_rows(tm, d),
        scratch_shapes=[pltpu.VMEM((tm + 2 * POOL_HALO, pw), F32)],
    )
    return pl.pallas_call(
        functools.partial(_mix0_kernel, n_a=n_a),
        grid_spec=grid_spec,
        out_shape=jax.ShapeDtypeStruct((t, d), F32),
        compiler_params=_params(("arbitrary",)),
        name="pool_outproj",
    )(meta["pos"], meta["len"], meta["seq"], xa, xb, a, p, p, p, mod, pool_w_bf,
      pool_scale.reshape(1, pw), w_out_bf)


def _gelu(z):
    return 0.5 * z * (1.0 + lax.erf(z * np.float32(np.sqrt(0.5))))


def _gmlp_in_kernel(ts, x_ref, g_ref, sc_ref, sh_ref, w_ref, lg_ref, lb_ref, u_ref, v_ref):
    h = _norm_mod(x_ref[...], g_ref[...], sc_ref[...], sh_ref[...]).astype(BF16)
    half = u_ref.shape[1]
    u_ref[...] = _gelu(_dot(h, w_ref[:, 0:half]))
    zv = _gelu(_dot(h, w_ref[:, half:]))
    zc = zv - jnp.mean(zv, axis=-1, keepdims=True)
    r = lax.rsqrt(jnp.mean(zc * zc, axis=-1, keepdims=True) + EPS)
    v_ref[...] = (zc * r * lg_ref[...] + lb_ref[...]).astype(BF16)


def _gmlp_in(x, meta, mod, norm_g, w_in_bf, ln_g, ln_b, tm):
    t, d = x.shape
    half = w_in_bf.shape[1] // 2
    grid_spec = pltpu.PrefetchScalarGridSpec(
        num_scalar_prefetch=1,
        grid=(t // tm,),
        in_specs=[_rows(tm, d), _resident((1, d)), _mod_spec(1, d), _mod_spec(0, d),
                  _resident(w_in_bf.shape), _resident((1, half)), _resident((1, half))],
        out_specs=[_rows(tm, half), _rows(tm, half)],
    )
    return pl.pallas_call(
        _gmlp_in_kernel,
        grid_spec=grid_spec,
        out_shape=[jax.ShapeDtypeStruct((t, half), F32), jax.ShapeDtypeStruct((t, half), BF16)],
        compiler_params=_params(("arbitrary",)),
        name="gmlp_in",
    )(meta["seq"], x, norm_g.reshape(1, d), mod, mod, w_in_bf, ln_g.reshape(1, half), ln_b.reshape(1, half))


def _gmlp_out_kernel(ts, x_ref, u_ref, v_ref, gate_ref, sw_ref, sb_ref, wo_ref, o_ref, gated):
    tm = x_ref.shape[0]
    hd = u_ref.shape[1] // SGU_HEADS
    for c in range(tm // CHUNK):
        rows = slice(c * CHUNK, (c + 1) * CHUNK)
        for hh in range(SGU_HEADS):
            cols = slice(hh * hd, (hh + 1) * hd)
            s = _dot(sw_ref[hh], v_ref[rows, cols]) + jnp.tile(sb_ref[hh], (1, hd // LANES))
            gated[rows, cols] = (u_ref[rows, cols] * s).astype(BF16)
    o_ref[...] = x_ref[...] + gate_ref[...] * _dot(gated[...], wo_ref[...])


def _gmlp_out(x, u, v, meta, mod, sgu_w_bf, sgu_b_b, w_out_bf, tm):
    t, d = x.shape
    w = u.shape[1]
    gate_spec = pl.BlockSpec((None, None, 1, d), lambda i, ts: (2, ts[i], 0, 0))
    grid_spec = pltpu.PrefetchScalarGridSpec(
        num_scalar_prefetch=1,
        grid=(t // tm,),
        in_specs=[_rows(tm, d), _rows(tm, w), _rows(tm, w), gate_spec,
                  _resident(sgu_w_bf.shape), _resident(sgu_b_b.shape), _resident(w_out_bf.shape)],
        out_specs=_rows(tm, d),
        scratch_shapes=[pltpu.VMEM((tm, w), BF16)],
    )
    return pl.pallas_call(
        _gmlp_out_kernel,
        grid_spec=grid_spec,
        out_shape=jax.ShapeDtypeStruct((t, d), F32),
        compiler_params=_params(("arbitrary",)),
        name="gmlp_out",
    )(meta["seq"], x, u, v, mod, sgu_w_bf, sgu_b_b, w_out_bf)


ROUTER_ROWS = 48


def _router_kernel(ts, x_ref, g_ref, sc_ref, sh_ref, wr_ref, br_ref, h_ref, ri_ref, rw_ref, cnt_ref, carry):
    i = pl.program_id(0)

    @pl.when(i == 0)
    def _():
        carry[...] = jnp.zeros_like(carry)

    tm = x_ref.shape[0]
    hb = _norm_mod(x_ref[...], g_ref[...], sc_ref[...], sh_ref[...]).astype(BF16)
    half = hb.shape[1] // 2
    bits = pltpu.bitcast(hb.astype(F32), jnp.uint32)
    h_ref[...] = (bits[:, :half] >> 16) | bits[:, half:]
    lg = _dot_nt(wr_ref[...], hb) + jnp.tile(br_ref[...], (1, tm // LANES))
    rows8 = lax.broadcasted_iota(jnp.int32, (EXP_PER_GROUP, tm), 0)

    def first_argmax(vals, vmax):
        return jnp.min(jnp.where(vals == vmax, rows8, EXP_PER_GROUP), axis=0, keepdims=True)

    gl = lg[0:8]
    gmax = jnp.max(gl, axis=0, keepdims=True)
    gidx = first_argmax(gl, gmax)
    g_w = 1.0 / jnp.sum(jnp.exp(gl - gmax), axis=0, keepdims=True)
    esel = jnp.zeros((EXP_PER_GROUP, tm), F32)
    for g in range(N_GROUPS):
        esel = jnp.where(gidx == g, lg[8 + g * EXP_PER_GROUP:8 + (g + 1) * EXP_PER_GROUP], esel)
    emax = jnp.max(esel, axis=0, keepdims=True)
    pe = jnp.exp(esel - emax)
    prob = pe / jnp.sum(pe, axis=0, keepdims=True)
    p1 = jnp.max(prob, axis=0, keepdims=True)
    i1 = first_argmax(prob, p1)
    rest = jnp.where(rows8 == i1, -1.0, prob)
    p2 = jnp.max(rest, axis=0, keepdims=True)
    i2 = first_argmax(rest, p2)
    den = p1 + p2
    w0 = g_w * (p1 / den)
    w1 = g_w * (p2 / den)
    e0 = gidx * EXP_PER_GROUP + i1
    e1 = gidx * EXP_PER_GROUP + i2

    rows_e = lax.broadcasted_iota(jnp.int32, (N_EXPERTS, tm), 0)
    oh0 = rows_e == e0
    oh1 = rows_e == e1
    both = jnp.where(oh0, 1.0, 0.0) + jnp.where(oh1, 1.0, 0.0)
    ri = lax.broadcasted_iota(jnp.int32, (tm, tm), 0)
    ci = lax.broadcasted_iota(jnp.int32, (tm, tm), 1)
    upper = jnp.where(ri < ci, 1.0, 0.0).astype(BF16)
    before = _dot(both.astype(BF16), upper) + carry[:, 0:1]
    r0 = jnp.sum(jnp.where(oh0, before, 0.0), axis=0, keepdims=True)
    r1 = jnp.sum(jnp.where(oh1, before, 0.0), axis=0, keepdims=True)
    new_cnt = carry[...] + jnp.sum(both, axis=1, keepdims=True)
    carry[...] = new_cnt
    cnt_ref[...] = new_cnt
    zi = jnp.zeros((4, tm), jnp.int32)
    ri_ref[...] = jnp.concatenate([e0, e1, r0.astype(jnp.int32), r1.astype(jnp.int32), zi], axis=0)
    rw_ref[...] = jnp.concatenate([w0, w1, jnp.zeros((6, tm), F32)], axis=0)


def _router(x, meta, mod, norm_g, wr_bf, br_b, tm):
    t, d = x.shape
    grid_spec = pltpu.PrefetchScalarGridSpec(
        num_scalar_prefetch=1,
        grid=(t // tm,),
        in_specs=[_rows(tm, d), _resident((1, d)), _mod_spec(4, d), _mod_spec(3, d),
                  _resident(wr_bf.shape), _resident(br_b.shape)],
        out_specs=[_rows(tm, d // 2),
                   pl.BlockSpec((8, tm), lambda i, *_: (0, i)),
                   pl.BlockSpec((8, tm), lambda i, *_: (0, i)),
                   pl.BlockSpec((N_EXPERTS, LANES), lambda i, *_: (0, 0))],
        scratch_shapes=[pltpu.VMEM((N_EXPERTS, LANES), F32)],
    )
    return pl.pallas_call(
        _router_kernel,
        grid_spec=grid_spec,
        out_shape=[jax.ShapeDtypeStruct((t, d // 2), jnp.uint32),
                   jax.ShapeDtypeStruct((8, t), jnp.int32),
                   jax.ShapeDtypeStruct((8, t), F32),
                   jax.ShapeDtypeStruct((N_EXPERTS, LANES), F32)],
        compiler_params=_params(("arbitrary",)),
        name="moe_router",
    )(meta["seq"], x, norm_g.reshape(1, d), mod, mod, wr_bf, br_b)


ROW_UNROLL = 8


def _row_copy(src, s_row, dst, d_row, sem):
    return pltpu.make_async_copy(src.at[pl.ds(s_row, 1)], dst.at[pl.ds(d_row, 1)], sem)


def _dispatch_kernel(pad_lo, pad_hi, slot_ref, hp_ref, xs_hbm, zrow, sem, zsem):
    i = pl.program_id(0)
    ch = slot_ref.shape[2] // 2

    @pl.when(i == 0)
    def _():
        zrow[...] = jnp.zeros_like(zrow)

        def per_expert(e, c):
            lax.fori_loop(pad_lo[e], pad_hi[e],
                          lambda r, c2: (_row_copy(zrow, 0, xs_hbm, r, zsem).start(), c2)[1], 0)
            lax.fori_loop(pad_lo[e], pad_hi[e],
                          lambda r, c2: (_row_copy(zrow, 0, xs_hbm, r, zsem).wait(), c2)[1], 0)
            return c

        lax.fori_loop(0, N_EXPERTS, per_expert, 0)

    def issue(q, c):
        for u in range(ROW_UNROLL):
            r = q * ROW_UNROLL + u
            for k in range(2):
                _row_copy(hp_ref, r, xs_hbm, slot_ref[0, 0, k * ch + r], sem).start()
        return c

    lax.fori_loop(0, ch // ROW_UNROLL, issue, 0)
    for k in range(2):
        pltpu.make_async_copy(hp_ref, xs_hbm.at[pl.ds(0, ch)], sem).wait()


def _dispatch(hp, slots, pad_lo, pad_hi, n_rows, ch):
    t, w = hp.shape
    grid_spec = pltpu.PrefetchScalarGridSpec(
        num_scalar_prefetch=2,
        grid=(t // ch,),
        in_specs=[
            pl.BlockSpec((1, 1, 2 * ch), lambda i, *_: (i, 0, 0), memory_space=pltpu.SMEM),
            _rows(ch, w),
        ],
        out_specs=pl.BlockSpec(memory_space=pl.ANY),
        scratch_shapes=[pltpu.VMEM((8, w), hp.dtype), pltpu.SemaphoreType.DMA, pltpu.SemaphoreType.DMA],
    )
    return pl.pallas_call(
        _dispatch_kernel,
        grid_spec=grid_spec,
        out_shape=jax.ShapeDtypeStruct((n_rows, w), hp.dtype),
        compiler_params=_params(("arbitrary",)),
        name="moe_dispatch",
    )(pad_lo, pad_hi, slots, hp)


def _ffn_kernel(be, nu, xs_ref, wg_ref, wu_ref, wd_ref, o_ref, xb):
    i = pl.program_id(0)
    j = pl.program_id(1)

    @pl.when((i >= nu[0]) & (j == 0))
    def _():
        o_ref[...] = jnp.zeros_like(o_ref)

    @pl.when(i < nu[0])
    def _():
        @pl.when(j == 0)
        def _():
            xp = xs_ref[...]
            half = xp.shape[1]
            xb[:, 0:half] = pltpu.bitcast(xp << 16, F32).astype(BF16)
            xb[:, half:] = pltpu.bitcast(xp & jnp.uint32(0xFFFF0000), F32).astype(BF16)

        x = xb[...]
        hg = _dot(x, wg_ref[...].astype(BF16))
        hu = _dot(x, wu_ref[...].astype(BF16))
        act = (hg * (1.0 / (1.0 + jnp.exp(-hg))) * hu).astype(BF16)
        part = _dot(act, wd_ref[...].astype(BF16))

        @pl.when(j == 0)
        def _():
            o_ref[...] = part

        @pl.when(j > 0)
        def _():
            o_ref[...] += part


def _expert_ffn(xs, blk_expert, n_used, layer, wg, wu, wd, rb, ft):
    n_rows, w = xs.shape
    d = 2 * w
    n_blocks = n_rows // rb
    ff = wg.shape[3]
    n_ff = ff // ft

    def ffj(i, j, nu):
        return jnp.where(i < nu[0], j, n_ff - 1)

    grid_spec = pltpu.PrefetchScalarGridSpec(
        num_scalar_prefetch=2,
        grid=(n_blocks, n_ff),
        in_specs=[
            pl.BlockSpec((rb, w), lambda i, j, be, nu: (i, 0)),
            pl.BlockSpec((None, None, d, ft), lambda i, j, be, nu: (layer, be[i], 0, ffj(i, j, nu))),
            pl.BlockSpec((None, None, d, ft), lambda i, j, be, nu: (layer, be[i], 0, ffj(i, j, nu))),
            pl.BlockSpec((None, None, ft, d), lambda i, j, be, nu: (layer, be[i], ffj(i, j, nu), 0)),
        ],
        out_specs=pl.BlockSpec((rb, d), lambda i, j, be, nu: (i, 0)),
        scratch_shapes=[pltpu.VMEM((rb, d), BF16)],
    )
    return pl.pallas_call(
        _ffn_kernel,
        grid_spec=grid_spec,
        out_shape=jax.ShapeDtypeStruct((n_rows, d), F32),
        compiler_params=_params(("arbitrary", "arbitrary")),
        name="expert_ffn",
    )(blk_expert, n_used, xs, wg, wu, wd)


def _combine_kernel(ts, idx_ref, nxt_ref, x_ref, wt_ref, gate_ref, ys_hbm, *rest, n_a):
    outs, (buf, sem) = rest[:-2], rest[-2:]
    i = pl.program_id(0)
    tm = x_ref.shape[0]
    cur = i % 2

    def fetch(idx, b):
        def issue(q, c):
            for u in range(ROW_UNROLL):
                r = q * ROW_UNROLL + u
                _row_copy(ys_hbm, idx[0, 0, r], buf.at[b], r, sem.at[b]).start()
            return c

        lax.fori_loop(0, 2 * tm // ROW_UNROLL, issue, 0)

    @pl.when(i == 0)
    def _():
        fetch(idx_ref, 0)

    @pl.when(i + 1 < pl.num_programs(0))
    def _():
        fetch(nxt_ref, 1 - cur)

    pltpu.make_async_copy(buf.at[cur], buf.at[cur], sem.at[cur]).wait()
    y = wt_ref[:, 0:1] * buf[cur, 0:tm] + wt_ref[:, 1:2] * buf[cur, tm:2 * tm]
    res = x_ref[...] + gate_ref[...] * y
    if n_a is None:
        outs[0][...] = res
    else:
        @pl.when(i < n_a)
        def _():
            outs[0][...] = res

        @pl.when(i >= n_a)
        def _():
            outs[1][...] = res


def _combine(x, ys, slots, wt, meta, mod, tm, split_rows=None):
    t, d = x.shape
    n_tiles = t // tm
    if split_rows is None:
        n_a, out_specs = None, _rows(tm, d)
        out_shape = jax.ShapeDtypeStruct((t, d), F32)
    else:
        n_a, out_specs = split_rows // tm, _split_rows(tm, d, split_rows // tm)
        out_shape = [jax.ShapeDtypeStruct((split_rows, d), F32), jax.ShapeDtypeStruct((t - split_rows, d), F32)]
    gate_spec = pl.BlockSpec((None, None, 1, d), lambda i, ts: (5, ts[i], 0, 0))
    grid_spec = pltpu.PrefetchScalarGridSpec(
        num_scalar_prefetch=1,
        grid=(n_tiles,),
        in_specs=[
            pl.BlockSpec((1, 1, 2 * tm), lambda i, ts: (i, 0, 0), memory_space=pltpu.SMEM),
            pl.BlockSpec((1, 1, 2 * tm), lambda i, ts: (jnp.minimum(i + 1, n_tiles - 1), 0, 0),
                         memory_space=pltpu.SMEM),
            _rows(tm, d),
            _rows(tm, wt.shape[1]),
            gate_spec,
            pl.BlockSpec(memory_space=pl.ANY),
        ],
        out_specs=out_specs,
        scratch_shapes=[pltpu.VMEM((2, 2 * tm, d), F32), pltpu.SemaphoreType.DMA((2,))],
    )
    return pl.pallas_call(
        functools.partial(_combine_kernel, n_a=n_a),
        grid_spec=grid_spec,
        out_shape=out_shape,
        compiler_params=_params(("arbitrary",)),
        name="moe_combine",
    )(meta["seq"], slots, slots, x, wt, mod, ys)


def _tile_meta(seq_lens, tm):
    seq, pos, ln = [], [], []
    for s, n in enumerate(seq_lens):
        assert n % tm == 0
        for k in range(n // tm):
            seq.append(s)
            pos.append(k * tm)
            ln.append(n)
    return {k: jnp.asarray(np.asarray(v, np.int32)) for k, v in (("seq", seq), ("pos", pos), ("len", ln))}


def _rope_tables(s_max):
    half = HEAD_DIM // 2
    inv = ROPE_THETA ** (-jnp.arange(half, dtype=F32) / half)
    ang = jnp.arange(s_max, dtype=F32)[:, None] * inv[None, :]
    cos, sin = jnp.cos(ang), jnp.sin(ang)
    return jnp.concatenate([cos, cos], axis=1), jnp.concatenate([-sin, sin], axis=1)


def _tile_slots(slot, tm):
    t = slot.shape[1]
    return slot.reshape(2, t // tm, tm).transpose(1, 0, 2).reshape(t // tm, 1, 2 * tm)


def _moe(x, metas, mod, norm_g, rg_w, rg_b, re_w, re_b, layer, wg, wu, wd, cfg, split_rows=None):
    t, d = x.shape
    wr = jnp.zeros((ROUTER_ROWS, d), F32).at[0:N_GROUPS].set(rg_w.T).at[8:8 + N_EXPERTS].set(re_w.T)
    br = jnp.zeros((ROUTER_ROWS,), F32).at[N_GROUPS:8].set(NEG_BIG).at[0:N_GROUPS].set(rg_b)
    br = br.at[8:8 + N_EXPERTS].set(re_b)
    br_b = jnp.broadcast_to(br[:, None], (ROUTER_ROWS, LANES))
    hp, ri, rw, cnt = _router(x, metas[cfg["tm_router"]], mod, norm_g, wr.astype(BF16), br_b, cfg["tm_router"])

    rb = cfg["rb"]
    expert = ri[0:2]
    rank = ri[2:4]
    counts = cnt[:, 0].astype(jnp.int32)
    nblk = (counts + rb - 1) // rb
    blk_end = jnp.cumsum(nblk)
    row_start = (blk_end - nblk) * rb
    ids = jnp.arange(N_EXPERTS, dtype=jnp.int32)
    start_of = jnp.sum(jnp.where(expert[:, :, None] == ids, row_start, 0), axis=-1)
    slot = start_of + rank
    n_blocks = -(-2 * t // rb) + N_EXPERTS
    n_used = blk_end[-1]
    blk = jnp.arange(n_blocks, dtype=jnp.int32)
    blk_expert = jnp.minimum(jnp.sum(blk[:, None] >= blk_end[None, :], axis=1), N_EXPERTS - 1).astype(jnp.int32)
    last_used = jnp.sum(jnp.where(blk == n_used - 1, blk_expert, 0))
    blk_expert = jnp.where(blk < n_used, blk_expert, last_used)
    pad_lo = row_start + counts
    pad_hi = (blk_end * rb).at[N_EXPERTS - 1].set(n_blocks * rb)

    ch = cfg["ch_dispatch"]
    xs = _dispatch(hp, _tile_slots(slot, ch), pad_lo.astype(jnp.int32), pad_hi.astype(jnp.int32),
                   n_blocks * rb, ch)
    ys = _expert_ffn(xs, blk_expert, n_used.reshape(1).astype(jnp.int32), layer, wg, wu, wd, rb, cfg["ft"])

    tmc = cfg["tm_combine"]
    wt = rw.T
    return _combine(x, ys, _tile_slots(slot, tmc), wt, metas[tmc], mod, tmc, split_rows)


def _config(seq_lens):
    g = int(np.gcd.reduce(np.asarray(seq_lens)))
    tm = min(512, g)
    return {"tm": tm, "tm_router": tm, "tm_combine": min(256, g), "tm_gmlp": min(512, g), "rb": 1024, "ft": 256,
            "ch_dispatch": min(1024, g)}


def _forward(x_prompt, x_sample, c_prompt, c_sample, ada_w, ada_b, norm_mix_g, norm_ffn_g,
             ab_w_in, q_norm_g, k_norm_g, attn_sink, pool_w, pool_scale, ab_w_out,
             c_w_in, sgu_ln_g, sgu_ln_b, sgu_w, sgu_b, c_w_out,
             router_group_w, router_group_b, router_expert_w, router_expert_b,
             exp_w_gate, exp_w_up, exp_w_down, cfg=None):
    bp, sp, d = x_prompt.shape
    bs, ss, _ = x_sample.shape
    seq_lens = [sp] * bp + [ss] * bs
    n_seq = len(seq_lens)
    if cfg is None:
        cfg = _config(seq_lens)
    depth = ada_w.shape[0]
    x = (x_prompt.reshape(bp * sp, d), x_sample.reshape(bs * ss, d))
    metas ={tm: _tile_meta(seq_lens, tm) for tm in
             {cfg["tm"], cfg["tm_router"], cfg["tm_combine"], cfg["tm_gmlp"]}}

    n_pad = -(-n_seq // 8) * 8
    c_pad = jnp.zeros((n_pad, d), F32).at[0:n_seq].set(jnp.concatenate([c_prompt, c_sample], axis=0))
    mod_all = _ada_mod(c_pad, ada_w, ada_b)
    cos_t, sin_t = _rope_tables(max(seq_lens))

    for l in range(depth):
        mod = mod_all[l, 0:n_seq].reshape(n_seq, 6, d).transpose(1, 0, 2).reshape(6, n_seq, 1, d)
        i = l // 2
        if l % 2 == 0:
            tm = cfg["tm"]
            q, k, v, p = _inproj(x, metas[tm], mod, norm_mix_g[l], ab_w_in[i].astype(BF16),
                                 q_norm_g[i], k_norm_g[i], cos_t, sin_t, tm)
            sink_b = jnp.broadcast_to(
                jnp.repeat(attn_sink[i].reshape(N_KV_HEADS, GQA_GROUP), ATT_BLOCK, axis=1)[:, :, None],
                (N_KV_HEADS, GQA_GROUP * ATT_BLOCK, LANES)).astype(F32)
            a = _attention(q, k, v, sink_b, metas[tm], tm)
            x = _mix0(x, a, p, metas[tm], mod, pool_w[i].astype(BF16), pool_scale[i],
                      ab_w_out[i].astype(BF16), tm)
        else:
            tm = cfg["tm_gmlp"]
            if isinstance(x, tuple):
                x = jnp.concatenate(x, axis=0)
            u, v = _gmlp_in(x,metas[tm], mod, norm_mix_g[l], c_w_in[i].astype(BF16),
                            sgu_ln_g[i], sgu_ln_b[i], tm)
            sgu_b_b = jnp.broadcast_to(sgu_b[i][:, :, None], (SGU_HEADS, CHUNK, LANES)).astype(F32)
            x = _gmlp_out(x, u, v, metas[tm], mod, sgu_w[i].astype(BF16), sgu_b_b,
                          c_w_out[i].astype(BF16), tm)
        x = _moe(x, metas, mod, norm_ffn_g[l], router_group_w[l], router_group_b[l],
                 router_expert_w[l], router_expert_b[l], l, exp_w_gate, exp_w_up, exp_w_down, cfg,
                 split_rows=bp * sp if l == depth - 1 else None)

    y_prompt, y_sample = x
    return (y_prompt.reshape(bp, sp, d), y_sample.reshape(bs, ss, d))


def kernel(x_prompt, x_sample, c_prompt, c_sample, ada_w, ada_b, norm_mix_g, norm_ffn_g, ab_w_in, q_norm_g,
           k_norm_g, attn_sink, pool_w, pool_scale, ab_w_out, c_w_in, sgu_ln_g, sgu_ln_b, sgu_w, sgu_b,
           c_w_out, router_group_w, router_group_b, router_expert_w, router_expert_b, exp_w_gate, exp_w_up,
           exp_w_down):
    return _forward(x_prompt, x_sample, c_prompt, c_sample, ada_w, ada_b, norm_mix_g, norm_ffn_g, ab_w_in,
                    q_norm_g, k_norm_g, attn_sink, pool_w, pool_scale, ab_w_out, c_w_in, sgu_ln_g, sgu_ln_b,
                    sgu_w, sgu_b, c_w_out, router_group_w, router_group_b, router_expert_w, router_expert_b,
                    exp_w_gate, exp_w_up, exp_w_down)
```

```python
import functools

import numpy as np
import jax
import jax.numpy as jnp
from jax import lax
from jax.experimental import pallas as pl
from jax.experimental.pallas import tpu as pltpu

HEAD_DIM = 128
N_HEADS = 8
N_KV_HEADS = 2
GQA_GROUP = N_HEADS // N_KV_HEADS
ATT_BLOCK = 128
ROPE_THETA = 10000.0
Q_W = N_HEADS * HEAD_DIM
KV_W = N_KV_HEADS * HEAD_DIM
POOL_WINDOWS = (2, 4, 8, 16)
POOL_HALO = 8
CHUNK = 128
SGU_HEADS = 8
N_GROUPS = 4
EXP_PER_GROUP = 8
N_EXPERTS = N_GROUPS * EXP_PER_GROUP
EPS = 1e-6

VMEM_LIMIT_BYTES = 56 * 1024 * 1024
LANES = 128

F32 = jnp.float32
BF16 = jnp.bfloat16
NEG_BIG = -1e30


def _params(sem):
    return pltpu.CompilerParams(dimension_semantics=sem, vmem_limit_bytes=VMEM_LIMIT_BYTES)


def _resident(shape):
    nd = len(shape)
    return pl.BlockSpec(shape, lambda *_: (0,) * nd, pipeline_mode=pl.Buffered(1))


def _rows(tm, width):
    return pl.BlockSpec((tm, width), lambda i, *_: (i, 0))


def _split_rows(tm, width, n_a):
    return [pl.BlockSpec((tm, width), lambda i, *_: (jnp.minimum(i, n_a - 1), 0)),
            pl.BlockSpec((tm, width), lambda i, *_: (jnp.maximum(i - n_a, 0), 0))]


def _as_pair(x, tm):
    if isinstance(x, tuple):
        return x[0], x[1], x[0].shape[0] // tm
    return x, x, x.shape[0] // tm


def _mod_spec(part, d):
    return pl.BlockSpec((None, None, 1, d), lambda i, ts, *_: (part, ts[i], 0, 0))


def _norm_mod(x, g, sc, sh):
    r = lax.rsqrt(jnp.mean(x * x, axis=-1, keepdims=True) + EPS)
    return x * r * g * (1.0 + sc) + sh


def _dot(a, b):
    return jnp.dot(a, b, preferred_element_type=F32)


def _dot_nt(a, b):
    return lax.dot_general(a, b, (((1,), (1,)), ((), ())), preferred_element_type=F32)


def _ada_kernel(c_ref, w_ref, b_ref, o_ref):
    c = c_ref[...]
    cs = c * (1.0 / (1.0 + jnp.exp(-c)))
    o_ref[...] = _dot(cs.astype(BF16), w_ref[...].astype(BF16)) + b_ref[...]


def _ada_mod(c_pad, ada_w, ada_b):
    depth, d, n = ada_w.shape
    tn = 1024
    return pl.pallas_call(
        _ada_kernel,
        grid=(depth, n // tn),
        in_specs=[
            pl.BlockSpec(c_pad.shape, lambda l, j: (0, 0)),
            pl.BlockSpec((None, d, tn), lambda l, j: (l, 0, j)),
            pl.BlockSpec((None, 1, tn), lambda l, j: (l, 0, j)),
        ],
        out_specs=pl.BlockSpec((None, c_pad.shape[0], tn), lambda l, j: (l, 0, j)),
        out_shape=jax.ShapeDtypeStruct((depth, c_pad.shape[0], n), F32),
        compiler_params=_params(("arbitrary", "arbitrary")),
        name="ada_mod",
    )(c_pad, ada_w, ada_b.reshape(depth, 1, n))


def _inproj_kernel(ts, tp, xa_ref, xb_ref, g_ref, sc_ref, sh_ref, w_ref, qg_ref, kg_ref, cos_ref, sin_ref,
                   q_ref, k_ref, v_ref, p_ref, *, n_a):
    x = jnp.where(pl.program_id(0) < n_a, xa_ref[...], xb_ref[...])
    h = _norm_mod(x, g_ref[...], sc_ref[...], sh_ref[...]).astype(BF16)
    cos = cos_ref[...]
    sin = sin_ref[...]

    def head_norm_rope(y, gain):
        r = lax.rsqrt(jnp.mean(y * y, axis=-1, keepdims=True) + EPS)
        y = y * r * gain
        return y * cos + pltpu.roll(y, HEAD_DIM // 2, 1) * sin

    q = _dot(h, w_ref[:, 0:Q_W])
    for hh in range(N_HEADS):
        sl = slice(hh * HEAD_DIM, (hh + 1) * HEAD_DIM)
        q_ref[:, sl] = head_norm_rope(q[:, sl], qg_ref[...]).astype(BF16)
    kv = _dot(h, w_ref[:, Q_W:Q_W + 2 * KV_W])
    for hh in range(N_KV_HEADS):
        sl = slice(hh * HEAD_DIM, (hh + 1) * HEAD_DIM)
        k_ref[:, sl] = head_norm_rope(kv[:, sl], kg_ref[...]).astype(BF16)
    v_ref[...] = kv[:, KV_W:].astype(BF16)
    p_ref[...] = _dot(h, w_ref[:, Q_W + 2 * KV_W:])


def _inproj(x, meta, mod, norm_g, w_in_bf, q_g, k_g, cos_t, sin_t, tm):
    xa, xb, n_a = _as_pair(x, tm)
    d = xa.shape[1]
    n_tiles = meta["seq"].shape[0]
    t = n_tiles * tm
    pool_w = w_in_bf.shape[1] - Q_W - 2 * KV_W
    rope_spec = pl.BlockSpec((tm, HEAD_DIM), lambda i, ts, tp: (tp[i] // tm, 0))
    grid_spec = pltpu.PrefetchScalarGridSpec(
        num_scalar_prefetch=2,
        grid=(n_tiles,),
        in_specs=_split_rows(tm, d, n_a) + [
            _resident((1, d)),
            _mod_spec(1, d),
            _mod_spec(0, d),
            _resident(w_in_bf.shape),
            _resident((1, HEAD_DIM)),
            _resident((1, HEAD_DIM)),
            rope_spec,
            rope_spec,
        ],
        out_specs=[_rows(tm, Q_W), _rows(tm, KV_W), _rows(tm, KV_W), _rows(tm, pool_w)],
    )
    return pl.pallas_call(
        functools.partial(_inproj_kernel, n_a=n_a),
        grid_spec=grid_spec,
        out_shape=[
            jax.ShapeDtypeStruct((t, Q_W), BF16),
            jax.ShapeDtypeStruct((t, KV_W), BF16),
            jax.ShapeDtypeStruct((t, KV_W), BF16),
            jax.ShapeDtypeStruct((t, pool_w), F32),
        ],
        compiler_params=_params(("arbitrary",)),
        name="attn_pool_inproj",
    )(meta["seq"], meta["pos"], xa, xb, norm_g.reshape(1, d), mod, mod, w_in_bf,
      q_g.reshape(1, HEAD_DIM), k_g.reshape(1, HEAD_DIM), cos_t, sin_t)


def _attn_kernel(tp, tl, q_ref, kc_ref, kp_ref, kn_ref, vc_ref, vp_ref, vn_ref, sink_ref, o_ref,
                 kx, vx, s_scr, p_scr, r_scr):
    i = pl.program_id(0)
    tm = q_ref.shape[0]
    nb = tm // ATT_BLOCK
    first = tp[i] == 0
    last = tp[i] + tm == tl[i]
    kx[0:ATT_BLOCK] = kp_ref[...]
    kx[ATT_BLOCK:ATT_BLOCK + tm] = kc_ref[...]
    kx[ATT_BLOCK + tm:] = kn_ref[...]
    vx[0:ATT_BLOCK] = vp_ref[...]
    vx[ATT_BLOCK:ATT_BLOCK + tm] = vc_ref[...]
    vx[ATT_BLOCK + tm:] = vn_ref[...]
    win = 3 * ATT_BLOCK
    qi = lax.broadcasted_iota(jnp.int32, (ATT_BLOCK, win), 0)
    kj = lax.broadcasted_iota(jnp.int32, (ATT_BLOCK, win), 1)
    band = (kj >= qi) & (kj <= qi + 2 * ATT_BLOCK)
    scale = HEAD_DIM ** -0.5
    units = [(b, kk) for b in range(nb) for kk in range(N_KV_HEADS)]

    def heads_of(kk):
        return [kk * GQA_GROUP + g for g in range(GQA_GROUP)]

    for n, (b, kk) in enumerate(units):
        rows = slice(b * ATT_BLOCK, (b + 1) * ATT_BLOCK)
        kw = kx[b * ATT_BLOCK:b * ATT_BLOCK + win, kk * HEAD_DIM:(kk + 1) * HEAD_DIM]
        qs = jnp.concatenate([q_ref[rows, hd * HEAD_DIM:(hd + 1) * HEAD_DIM] for hd in heads_of(kk)], axis=0)
        s_scr[n] = _dot_nt(qs, kw)
    for n, (b, kk) in enumerate(units):
        valid = band
        if b == 0:
            valid = valid & (kj >= jnp.where(first, ATT_BLOCK, 0))
        if b == nb - 1:
            valid = valid & (kj < jnp.where(last, 2 * ATT_BLOCK, win))
        bias = jnp.where(valid, 0.0, -jnp.inf).astype(F32)
        s = s_scr[n] * scale
        s = (s.reshape(GQA_GROUP, ATT_BLOCK, win) + bias[None]).reshape(GQA_GROUP * ATT_BLOCK, win)
        sk = sink_ref[kk][:, 0:1]
        m = jnp.maximum(jnp.max(s, axis=-1, keepdims=True), sk)
        p = jnp.exp(s - m)
        denom = jnp.sum(p, axis=-1, keepdims=True) + jnp.exp(sk - m)
        p_scr[n] = p.astype(BF16)
        r_scr[n] = jnp.broadcast_to(1.0 / denom, r_scr.shape[1:])
    for n, (b, kk) in enumerate(units):
        rows = slice(b * ATT_BLOCK, (b + 1) * ATT_BLOCK)
        vw = vx[b * ATT_BLOCK:b * ATT_BLOCK + win, kk * HEAD_DIM:(kk + 1) * HEAD_DIM]
        o = _dot(p_scr[n], vw) * r_scr[n]
        for g, hd in enumerate(heads_of(kk)):
            o_ref[rows, hd * HEAD_DIM:(hd + 1) * HEAD_DIM] = o[g * ATT_BLOCK:(g + 1) * ATT_BLOCK].astype(BF16)


def _attention(q, k, v, sink_b, meta, tm):
    t = q.shape[0]
    n_tiles = t // tm
    r = tm // ATT_BLOCK
    n_blk = t // ATT_BLOCK
    cur = pl.BlockSpec((tm, KV_W), lambda i, *_: (i, 0))
    prev = pl.BlockSpec((ATT_BLOCK, KV_W), lambda i, *_: (jnp.maximum(i * r - 1, 0), 0))
    nxt = pl.BlockSpec((ATT_BLOCK, KV_W), lambda i, *_: (jnp.minimum((i + 1) * r, n_blk - 1), 0))
    grid_spec = pltpu.PrefetchScalarGridSpec(
        num_scalar_prefetch=2,
        grid=(n_tiles,),
        in_specs=[_rows(tm, Q_W), cur, prev, nxt, cur, prev, nxt, _resident(sink_b.shape)],
        out_specs=_rows(tm, Q_W),
        scratch_shapes=[pltpu.VMEM((tm + 2 * ATT_BLOCK, KV_W), BF16),
                        pltpu.VMEM((tm + 2 * ATT_BLOCK, KV_W), BF16),
                        pltpu.VMEM((r * N_KV_HEADS, GQA_GROUP * ATT_BLOCK, 3 * ATT_BLOCK), F32),
                        pltpu.VMEM((r * N_KV_HEADS, GQA_GROUP * ATT_BLOCK, 3 * ATT_BLOCK), BF16),
                        pltpu.VMEM((r * N_KV_HEADS, GQA_GROUP * ATT_BLOCK, HEAD_DIM), F32)],
    )
    return pl.pallas_call(
        _attn_kernel,
        grid_spec=grid_spec,
        out_shape=jax.ShapeDtypeStruct((t, Q_W), BF16),
        compiler_params=_params(("arbitrary",)),
        name="banded_attention",
    )(meta["pos"], meta["len"], q, k, k, k, v, v, v, sink_b)


def _mix0_kernel(tp, tl, ts, xa_ref, xb_ref, a_ref, pc_ref, pp_ref, pn_ref, gate_ref, pw_ref, ps_ref, wo_ref,
                 o_ref, pext, *, n_a):
    i = pl.program_id(0)
    tm = xa_ref.shape[0]
    first = tp[i] == 0
    last = tp[i] + tm == tl[i]
    pext[0:POOL_HALO] = jnp.where(first, 0.0, pp_ref[...])
    pext[POOL_HALO:POOL_HALO + tm] = pc_ref[...]
    pext[POOL_HALO + tm:] = jnp.where(last, 0.0, pn_ref[...])
    n_g = len(POOL_WINDOWS)
    gw = pc_ref.shape[1] // n_g
    pos = tp[i] + lax.broadcasted_iota(jnp.int32, (tm, gw), 0)
    seq_len = tl[i]
    ms = []
    for g, w in enumerate(POOL_WINDOWS):
        cols = slice(g * gw, (g + 1) * gw)
        acc = pext[POOL_HALO - w // 2:POOL_HALO - w // 2 + tm, cols]
        for off in range(-w // 2 + 1, w // 2):
            acc = acc + pext[POOL_HALO + off:POOL_HALO + off + tm, cols]
        cnt = (jnp.minimum(pos + w // 2, seq_len) - jnp.maximum(pos - w // 2, 0)).astype(F32)
        dlt = acc / cnt - pc_ref[:, cols]
        ms.append((_dot(dlt.astype(BF16), pw_ref[g]) * ps_ref[:, cols]).astype(BF16))
    m = jnp.concatenate(ms, axis=1)
    w_a = a_ref.shape[1]
    mix = _dot(a_ref[...], wo_ref[0:w_a, :]) + _dot(m, wo_ref[w_a:, :])
    x = jnp.where(i < n_a, xa_ref[...], xb_ref[...])
    o_ref[...] = x + gate_ref[...] * mix


def _mix0(x, a, p, meta, mod, pool_w_bf, pool_scale, w_out_bf, tm):
    xa, xb, n_a = _as_pair(x, tm)
    d = xa.shape[1]
    t, pw = p.shape
    n_tiles = t // tm
    r = tm // POOL_HALO
    n_hb = t // POOL_HALO
    prev = pl.BlockSpec((POOL_HALO, pw), lambda i, *_: (jnp.maximum(i * r - 1, 0), 0))
    nxt = pl.BlockSpec((POOL_HALO, pw), lambda i, *_: (jnp.minimum((i + 1) * r, n_hb - 1), 0))
    gate_spec = pl.BlockSpec((None, None, 1, d), lambda i, tp, tl, ts: (2, ts[i], 0, 0))
    grid_spec = pltpu.PrefetchScalarGridSpec(
        num_scalar_prefetch=3,
        grid=(n_tiles,),
        in_specs=_split_rows(tm, d, n_a) + [
            _rows(tm, a.shape[1]), _rows(tm, pw), prev, nxt, gate_spec,
            _resident(pool_w_bf.shape), _resident((1, pw)), _resident(w_out_bf.shape)],
        out_specs=_rows(tm, d),
        scratch_shapes=[pltpu.VMEM((tm + 2 * POOL_HALO, pw), F32)],
    )
    return pl.pallas_call(
        functools.partial(_mix0_kernel, n_a=n_a),
        grid_spec=grid_spec,
        out_shape=jax.ShapeDtypeStruct((t, d), F32),
        compiler_params=_params(("arbitrary",)),
        name="pool_outproj",
    )(meta["pos"], meta["len"], meta["seq"], xa, xb, a, p, p, p, mod, pool_w_bf,
      pool_scale.reshape(1, pw), w_out_bf)


def _gelu(z):
    return 0.5 * z * (1.0 + lax.erf(z * np.float32(np.sqrt(0.5))))


def _gmlp_in_kernel(ts, x_ref, g_ref, sc_ref, sh_ref, w_ref, lg_ref, lb_ref, u_ref, v_ref):
    h = _norm_mod(x_ref[...], g_ref[...], sc_ref[...], sh_ref[...]).astype(BF16)
    half = u_ref.shape[1]
    u_ref[...] = _gelu(_dot(h, w_ref[:, 0:half]))
    zv = _gelu(_dot(h, w_ref[:, half:]))
    zc = zv - jnp.mean(zv, axis=-1, keepdims=True)
    r = lax.rsqrt(jnp.mean(zc * zc, axis=-1, keepdims=True) + EPS)
    v_ref[...] = (zc * r * lg_ref[...] + lb_ref[...]).astype(BF16)


def _gmlp_in(x, meta, mod, norm_g, w_in_bf, ln_g, ln_b, tm):
    t, d = x.shape
    half = w_in_bf.shape[1] // 2
    grid_spec = pltpu.PrefetchScalarGridSpec(
        num_scalar_prefetch=1,
        grid=(t // tm,),
        in_specs=[_rows(tm, d), _resident((1, d)), _mod_spec(1, d), _mod_spec(0, d),
                  _resident(w_in_bf.shape), _resident((1, half)), _resident((1, half))],
        out_specs=[_rows(tm, half), _rows(tm, half)],
    )
    return pl.pallas_call(
        _gmlp_in_kernel,
        grid_spec=grid_spec,
        out_shape=[jax.ShapeDtypeStruct((t, half), F32), jax.ShapeDtypeStruct((t, half), BF16)],
        compiler_params=_params(("arbitrary",)),
        name="gmlp_in",
    )(meta["seq"], x, norm_g.reshape(1, d), mod, mod, w_in_bf, ln_g.reshape(1, half), ln_b.reshape(1, half))


def _gmlp_out_kernel(ts, x_ref, u_ref, v_ref, gate_ref, sw_ref, sb_ref, wo_ref, o_ref, gated):
    tm = x_ref.shape[0]
    hd = u_ref.shape[1] // SGU_HEADS
    for c in range(tm // CHUNK):
        rows = slice(c * CHUNK, (c + 1) * CHUNK)
        for hh in range(SGU_HEADS):
            cols = slice(hh * hd, (hh + 1) * hd)
            s = _dot(sw_ref[hh], v_ref[rows, cols]) + jnp.tile(sb_ref[hh], (1, hd // LANES))
            gated[rows, cols] = (u_ref[rows, cols] * s).astype(BF16)
    o_ref[...] = x_ref[...] + gate_ref[...] * _dot(gated[...], wo_ref[...])


def _gmlp_out(x, u, v, meta, mod, sgu_w_bf, sgu_b_b, w_out_bf, tm):
    t, d = x.shape
    w = u.shape[1]
    gate_spec = pl.BlockSpec((None, None, 1, d), lambda i, ts: (2, ts[i], 0, 0))
    grid_spec = pltpu.PrefetchScalarGridSpec(
        num_scalar_prefetch=1,
        grid=(t // tm,),
        in_specs=[_rows(tm, d), _rows(tm, w), _rows(tm, w), gate_spec,
                  _resident(sgu_w_bf.shape), _resident(sgu_b_b.shape), _resident(w_out_bf.shape)],
        out_specs=_rows(tm, d),
        scratch_shapes=[pltpu.VMEM((tm, w), BF16)],
    )
    return pl.pallas_call(
        _gmlp_out_kernel,
        grid_spec=grid_spec,
        out_shape=jax.ShapeDtypeStruct((t, d), F32),
        compiler_params=_params(("arbitrary",)),
        name="gmlp_out",
    )(meta["seq"], x, u, v, mod, sgu_w_bf, sgu_b_b, w_out_bf)


ROUTER_ROWS = 48


def _router_kernel(ts, x_ref, g_ref, sc_ref, sh_ref, wr_ref, br_ref, h_ref, ri_ref, rw_ref, cnt_ref, carry):
    i = pl.program_id(0)

    @pl.when(i == 0)
    def _():
        carry[...] = jnp.zeros_like(carry)

    tm = x_ref.shape[0]
    hb = _norm_mod(x_ref[...], g_ref[...], sc_ref[...], sh_ref[...]).astype(BF16)
    half = hb.shape[1] // 2
    bits = pltpu.bitcast(hb.astype(F32), jnp.uint32)
    h_ref[...] = (bits[:, :half] >> 16) | bits[:, half:]
    lg = _dot_nt(wr_ref[...], hb) + jnp.tile(br_ref[...], (1, tm // LANES))
    rows8 = lax.broadcasted_iota(jnp.int32, (EXP_PER_GROUP, tm), 0)

    def first_argmax(vals, vmax):
        return jnp.min(jnp.where(vals == vmax, rows8, EXP_PER_GROUP), axis=0, keepdims=True)

    gl = lg[0:8]
    gmax = jnp.max(gl, axis=0, keepdims=True)
    gidx = first_argmax(gl, gmax)
    g_w = 1.0 / jnp.sum(jnp.exp(gl - gmax), axis=0, keepdims=True)
    esel = jnp.zeros((EXP_PER_GROUP, tm), F32)
    for g in range(N_GROUPS):
        esel = jnp.where(gidx == g, lg[8 + g * EXP_PER_GROUP:8 + (g + 1) * EXP_PER_GROUP], esel)
    emax = jnp.max(esel, axis=0, keepdims=True)
    pe = jnp.exp(esel - emax)
    prob = pe / jnp.sum(pe, axis=0, keepdims=True)
    p1 = jnp.max(prob, axis=0, keepdims=True)
    i1 = first_argmax(prob, p1)
    rest = jnp.where(rows8 == i1, -1.0, prob)
    p2 = jnp.max(rest, axis=0, keepdims=True)
    i2 = first_argmax(rest, p2)
    den = p1 + p2
    w0 = g_w * (p1 / den)
    w1 = g_w * (p2 / den)
    e0 = gidx * EXP_PER_GROUP + i1
    e1 = gidx * EXP_PER_GROUP + i2

    rows_e = lax.broadcasted_iota(jnp.int32, (N_EXPERTS, tm), 0)
    oh0 = rows_e == e0
    oh1 = rows_e == e1
    both = jnp.where(oh0, 1.0, 0.0) + jnp.where(oh1, 1.0, 0.0)
    ri = lax.broadcasted_iota(jnp.int32, (tm, tm), 0)
    ci = lax.broadcasted_iota(jnp.int32, (tm, tm), 1)
    upper = jnp.where(ri < ci, 1.0, 0.0).astype(BF16)
    before = _dot(both.astype(BF16), upper) + carry[:, 0:1]
    r0 = jnp.sum(jnp.where(oh0, before, 0.0), axis=0, keepdims=True)
    r1 = jnp.sum(jnp.where(oh1, before, 0.0), axis=0, keepdims=True)
    new_cnt = carry[...] + jnp.sum(both, axis=1, keepdims=True)
    carry[...] = new_cnt
    cnt_ref[...] = new_cnt
    zi = jnp.zeros((4, tm), jnp.int32)
    ri_ref[...] = jnp.concatenate([e0, e1, r0.astype(jnp.int32), r1.astype(jnp.int32), zi], axis=0)
    rw_ref[...] = jnp.concatenate([w0, w1, jnp.zeros((6, tm), F32)], axis=0)


def _router(x, meta, mod, norm_g, wr_bf, br_b, tm):
    t, d = x.shape
    grid_spec = pltpu.PrefetchScalarGridSpec(
        num_scalar_prefetch=1,
        grid=(t // tm,),
        in_specs=[_rows(tm, d), _resident((1, d)), _mod_spec(4, d), _mod_spec(3, d),
                  _resident(wr_bf.shape), _resident(br_b.shape)],
        out_specs=[_rows(tm, d // 2),
                   pl.BlockSpec((8, tm), lambda i, *_: (0, i)),
                   pl.BlockSpec((8, tm), lambda i, *_: (0, i)),
                   pl.BlockSpec((N_EXPERTS, LANES), lambda i, *_: (0, 0))],
        scratch_shapes=[pltpu.VMEM((N_EXPERTS, LANES), F32)],
    )
    return pl.pallas_call(
        _router_kernel,
        grid_spec=grid_spec,
        out_shape=[jax.ShapeDtypeStruct((t, d // 2), jnp.uint32),
                   jax.ShapeDtypeStruct((8, t), jnp.int32),
                   jax.ShapeDtypeStruct((8, t), F32),
                   jax.ShapeDtypeStruct((N_EXPERTS, LANES), F32)],
        compiler_params=_params(("arbitrary",)),
        name="moe_router",
    )(meta["seq"], x, norm_g.reshape(1, d), mod, mod, wr_bf, br_b)


SUBLANES = 8


def _row_copy(src, s_grp, s_sub, dst, d_grp, d_sub, sem):
    return pltpu.make_async_copy(src.at[s_grp, pl.ds(s_sub, 1)], dst.at[d_grp, pl.ds(d_sub, 1)], sem)


def _split_row_index(idx):
    return jnp.concatenate([idx >> 3, idx & 7], axis=-1)


def _dispatch_kernel(pad_lo, pad_hi, slot_ref, hp_ref, xs_hbm, zrow, sem, zsem):
    i = pl.program_id(0)
    ch = slot_ref.shape[2] // 4
    n_grp = ch // SUBLANES

    @pl.when(i == 0)
    def _():
        zrow[...] = jnp.zeros_like(zrow)

        def per_range(e, c):
            lo, hi = pad_lo[e], pad_hi[e]
            lo8 = jnp.minimum(((lo + SUBLANES - 1) >> 3) << 3, hi)

            def row(r):
                return _row_copy(zrow, 0, 0, xs_hbm, r >> 3, r & 7, zsem)

            def grp(g):
                return pltpu.make_async_copy(zrow, xs_hbm.at[pl.ds(g, 1)], zsem)

            lax.fori_loop(lo, lo8, lambda r, c2: (row(r).start(), c2)[1], 0)
            lax.fori_loop(lo8 >> 3, hi >> 3, lambda g, c2: (grp(g).start(), c2)[1], 0)
            lax.fori_loop(lo, lo8, lambda r, c2: (row(r).wait(), c2)[1], 0)
            lax.fori_loop(lo8 >> 3, hi >> 3, lambda g, c2: (grp(g).wait(), c2)[1], 0)
            return c

        lax.fori_loop(0, pad_lo.shape[0], per_range, 0)

    def issue(q, c):
        for u in range(SUBLANES):
            r = q * SUBLANES + u
            for k in range(2):
                _row_copy(hp_ref, q, u, xs_hbm, slot_ref[0, 0, k * ch + r], slot_ref[0, 0, (2 + k) * ch + r],
                          sem).start()
        return c

    lax.fori_loop(0, n_grp, issue, 0)
    for k in range(2):
        pltpu.make_async_copy(hp_ref, xs_hbm.at[pl.ds(0, n_grp)], sem).wait()


def _dispatch(hp, slots, pad_lo, pad_hi, n_rows, ch):
    t, w = hp.shape
    grid_spec = pltpu.PrefetchScalarGridSpec(
        num_scalar_prefetch=2,
        grid=(t // ch,),
        in_specs=[
            pl.BlockSpec((1, 1, 4 * ch), lambda i, *_: (i, 0, 0), memory_space=pltpu.SMEM),
            pl.BlockSpec((ch // SUBLANES, SUBLANES, w), lambda i, *_: (i, 0, 0)),
        ],
        out_specs=pl.BlockSpec(memory_space=pl.ANY),
        scratch_shapes=[pltpu.VMEM((1, SUBLANES, w), hp.dtype), pltpu.SemaphoreType.DMA,
                        pltpu.SemaphoreType.DMA],
    )
    xs = pl.pallas_call(
        _dispatch_kernel,
        grid_spec=grid_spec,
        out_shape=jax.ShapeDtypeStruct((n_rows // SUBLANES, SUBLANES, w), hp.dtype),
        compiler_params=_params(("arbitrary",)),
        name="moe_dispatch",
    )(pad_lo, pad_hi, _split_row_index(slots), hp.reshape(t // SUBLANES, SUBLANES, w))
    return xs.reshape(n_rows, w)


FFN_ROW_STEPS = 4


def _ffn_kernel(be, nv, xs_ref, wg_ref, wu_ref, wd_ref, o_ref, xb):
    i = pl.program_id(0)
    j = pl.program_id(1)
    rb = o_ref.shape[0]
    step = rb // FFN_ROW_STEPS
    n = nv[i]

    @pl.when((j == 0) & (n > 0))
    def _():
        xp = xs_ref[...]
        half = xp.shape[1]
        xb[:, 0:half] = pltpu.bitcast(xp << 16, F32).astype(BF16)
        xb[:, half:] = pltpu.bitcast(xp & jnp.uint32(0xFFFF0000), F32).astype(BF16)

    def ffn_rows(rows):
        x = xb[0:rows]
        hg = _dot(x, wg_ref[...].astype(BF16))
        hu = _dot(x, wu_ref[...].astype(BF16))
        act = (hg * (1.0 / (1.0 + jnp.exp(-hg))) * hu).astype(BF16)
        part = _dot(act, wd_ref[...].astype(BF16))

        @pl.when(j == 0)
        def _():
            o_ref[0:rows] = part
            if rows < rb:
                o_ref[rows:] = jnp.zeros((rb - rows, o_ref.shape[1]), F32)

        @pl.when(j > 0)
        def _():
            o_ref[0:rows] += part

    @pl.when((n == 0) & (j == 0))
    def _():
        o_ref[...] = jnp.zeros_like(o_ref)

    for k in range(1, FFN_ROW_STEPS + 1):
        pl.when((n > (k - 1) * step) & (n <= k * step))(functools.partial(ffn_rows, k * step))


def _expert_ffn(xs, blk_expert, blk_valid, layer, wg, wu, wd, rb, ft):
    n_rows, w = xs.shape
    d = 2 * w
    n_blocks = n_rows // rb
    ff = wg.shape[3]
    n_ff = ff // ft

    def ffj(i, j, nv):
        return jnp.where(nv[i] > 0, j, n_ff - 1)

    grid_spec = pltpu.PrefetchScalarGridSpec(
        num_scalar_prefetch=2,
        grid=(n_blocks, n_ff),
        in_specs=[
            pl.BlockSpec((rb, w), lambda i, j, be, nv: (i, 0)),
            pl.BlockSpec((None, None, d, ft), lambda i, j, be, nv: (layer, be[i], 0, ffj(i, j, nv))),
            pl.BlockSpec((None, None, d, ft), lambda i, j, be, nv: (layer, be[i], 0, ffj(i, j, nv))),
            pl.BlockSpec((None, None, ft, d), lambda i, j, be, nv: (layer, be[i], ffj(i, j, nv), 0)),
        ],
        out_specs=pl.BlockSpec((rb, d), lambda i, j, be, nv: (i, 0)),
        scratch_shapes=[pltpu.VMEM((rb, d), BF16)],
    )
    return pl.pallas_call(
        _ffn_kernel,
        grid_spec=grid_spec,
        out_shape=jax.ShapeDtypeStruct((n_rows, d), F32),
        compiler_params=_params(("arbitrary", "arbitrary")),
        name="expert_ffn",
    )(blk_expert, blk_valid, xs, wg, wu, wd)


def _combine_kernel(ts, idx_ref, nxt_ref, x_ref, wt_ref, gate_ref, ys_hbm, *rest, n_a):
    outs, (buf, sem) = rest[:-2], rest[-2:]
    i = pl.program_id(0)
    tm = x_ref.shape[0]
    cur = i % 2

    n_grp = 2 * tm // SUBLANES

    def fetch(idx, b):
        def issue(q, c):
            for u in range(SUBLANES):
                r = q * SUBLANES + u
                _row_copy(ys_hbm, idx[0, 0, r], idx[0, 0, 2 * tm + r], buf.at[b], q, u, sem.at[b]).start()
            return c

        lax.fori_loop(0, n_grp, issue, 0)

    @pl.when(i == 0)
    def _():
        fetch(idx_ref, 0)

    @pl.when(i + 1 < pl.num_programs(0))
    def _():
        fetch(nxt_ref, 1 - cur)

    pltpu.make_async_copy(buf.at[cur], buf.at[cur], sem.at[cur]).wait()
    d = x_ref.shape[1]
    y0 = buf[cur, 0:n_grp // 2].reshape(tm, d)
    y1 = buf[cur, n_grp // 2:n_grp].reshape(tm, d)
    res = x_ref[...] + gate_ref[...] * (wt_ref[:, 0:1] * y0 + wt_ref[:, 1:2] * y1)
    if n_a is None:
        outs[0][...] = res
    else:
        @pl.when(i < n_a)
        def _():
            outs[0][...] = res

        @pl.when(i >= n_a)
        def _():
            outs[1][...] = res


def _combine(x, ys, slots, wt, meta, mod, tm, split_rows=None):
    t, d = x.shape
    n_tiles = t // tm
    if split_rows is None:
        n_a, out_specs = None, _rows(tm, d)
        out_shape = jax.ShapeDtypeStruct((t, d), F32)
    else:
        n_a, out_specs = split_rows // tm, _split_rows(tm, d, split_rows // tm)
        out_shape = [jax.ShapeDtypeStruct((split_rows, d), F32), jax.ShapeDtypeStruct((t - split_rows, d), F32)]
    gate_spec = pl.BlockSpec((None, None, 1, d), lambda i, ts: (5, ts[i], 0, 0))
    grid_spec = pltpu.PrefetchScalarGridSpec(
        num_scalar_prefetch=1,
        grid=(n_tiles,),
        in_specs=[
            pl.BlockSpec((1, 1, 4 * tm), lambda i, ts: (i, 0, 0), memory_space=pltpu.SMEM),
            pl.BlockSpec((1, 1, 4 * tm), lambda i, ts: (jnp.minimum(i + 1, n_tiles - 1), 0, 0),
                         memory_space=pltpu.SMEM),
            _rows(tm, d),
            _rows(tm, wt.shape[1]),
            gate_spec,
            pl.BlockSpec(memory_space=pl.ANY),
        ],
        out_specs=out_specs,
        scratch_shapes=[pltpu.VMEM((2, 2 * tm // SUBLANES, SUBLANES, d), F32), pltpu.SemaphoreType.DMA((2,))],
    )
    slots = _split_row_index(slots)
    ys = ys.reshape(ys.shape[0] // SUBLANES, SUBLANES, d)
    return pl.pallas_call(
        functools.partial(_combine_kernel, n_a=n_a),
        grid_spec=grid_spec,
        out_shape=out_shape,
        compiler_params=_params(("arbitrary",)),
        name="moe_combine",
    )(meta["seq"], slots, slots, x, wt, mod, ys)


def _tile_meta(seq_lens, tm):
    seq, pos, ln = [], [], []
    for s, n in enumerate(seq_lens):
        assert n % tm == 0
        for k in range(n // tm):
            seq.append(s)
            pos.append(k * tm)
            ln.append(n)
    return {k: jnp.asarray(np.asarray(v, np.int32)) for k, v in (("seq", seq), ("pos", pos), ("len", ln))}


def _rope_tables(s_max):
    half = HEAD_DIM // 2
    inv = ROPE_THETA ** (-jnp.arange(half, dtype=F32) / half)
    ang = jnp.arange(s_max, dtype=F32)[:, None] * inv[None, :]
    cos, sin = jnp.cos(ang), jnp.sin(ang)
    return jnp.concatenate([cos, cos], axis=1), jnp.concatenate([-sin, sin], axis=1)


def _tile_slots(slot, tm):
    t = slot.shape[1]
    return slot.reshape(2, t // tm, tm).transpose(1, 0, 2).reshape(t // tm, 1, 2 * tm)


def _moe(x, metas, mod, norm_g, rg_w, rg_b, re_w, re_b, layer, wg, wu, wd, cfg, split_rows=None):
    t, d = x.shape
    wr = jnp.zeros((ROUTER_ROWS, d), F32).at[0:N_GROUPS].set(rg_w.T).at[8:8 + N_EXPERTS].set(re_w.T)
    br = jnp.zeros((ROUTER_ROWS,), F32).at[N_GROUPS:8].set(NEG_BIG).at[0:N_GROUPS].set(rg_b)
    br = br.at[8:8 + N_EXPERTS].set(re_b)
    br_b = jnp.broadcast_to(br[:, None], (ROUTER_ROWS, LANES))
    hp, ri, rw, cnt = _router(x, metas[cfg["tm_router"]], mod, norm_g, wr.astype(BF16), br_b, cfg["tm_router"])

    rb = cfg["rb"]
    expert = ri[0:2]
    rank = ri[2:4]
    counts = cnt[:, 0].astype(jnp.int32)
    nblk = (counts + rb - 1) // rb
    blk_end = jnp.cumsum(nblk)
    row_start = (blk_end - nblk) * rb
    ids = jnp.arange(N_EXPERTS, dtype=jnp.int32)
    start_of = jnp.sum(jnp.where(expert[:, :, None] == ids, row_start, 0), axis=-1)
    slot = start_of + rank
    n_blocks = -(-2 * t // rb) + N_EXPERTS
    n_used = blk_end[-1]
    blk = jnp.arange(n_blocks, dtype=jnp.int32)
    blk_expert = jnp.minimum(jnp.sum(blk[:, None] >= blk_end[None, :], axis=1), N_EXPERTS - 1).astype(jnp.int32)
    last_used = jnp.sum(jnp.where(blk == n_used - 1, blk_expert, 0))
    blk_expert = jnp.where(blk < n_used, blk_expert, last_used)
    onehot = blk_expert[:, None] == ids
    row_end_of = jnp.sum(jnp.where(onehot, row_start + counts, 0), axis=1)
    blk_valid = jnp.where(blk < n_used, jnp.clip(row_end_of - blk * rb, 0, rb), 0).astype(jnp.int32)
    pad_lo = jnp.concatenate([row_start + counts, (n_used * rb).reshape(1)])
    pad_hi = jnp.concatenate([blk_end * rb, jnp.full((1,), n_blocks * rb, jnp.int32)])

    ch = cfg["ch_dispatch"]
    xs = _dispatch(hp, _tile_slots(slot, ch), pad_lo.astype(jnp.int32), pad_hi.astype(jnp.int32),
                   n_blocks * rb, ch)
    ys = _expert_ffn(xs, blk_expert, blk_valid, layer, wg, wu, wd, rb, cfg["ft"])

    tmc = cfg["tm_combine"]
    wt = rw.T
    return _combine(x, ys, _tile_slots(slot, tmc), wt, metas[tmc], mod, tmc, split_rows)


def _config(seq_lens):
    g = int(np.gcd.reduce(np.asarray(seq_lens)))
    tm = min(512, g)
    return {"tm": tm, "tm_router": tm, "tm_combine": min(256, g), "tm_gmlp": min(512, g), "rb": 1024, "ft": 256,
            "ch_dispatch": min(1024, g)}


def _forward(x_prompt, x_sample, c_prompt, c_sample, ada_w, ada_b, norm_mix_g, norm_ffn_g,
             ab_w_in, q_norm_g, k_norm_g, attn_sink, pool_w, pool_scale, ab_w_out,
             c_w_in, sgu_ln_g, sgu_ln_b, sgu_w, sgu_b, c_w_out,
             router_group_w, router_group_b, router_expert_w, router_expert_b,
             exp_w_gate, exp_w_up, exp_w_down, cfg=None):
    bp, sp, d = x_prompt.shape
    bs, ss, _ = x_sample.shape
    seq_lens = [sp] * bp + [ss] * bs
    n_seq = len(seq_lens)
    if cfg is None:
        cfg = _config(seq_lens)
    depth = ada_w.shape[0]
    x = (x_prompt.reshape(bp * sp, d), x_sample.reshape(bs * ss, d))
    metas ={tm: _tile_meta(seq_lens, tm) for tm in
             {cfg["tm"], cfg["tm_router"], cfg["tm_combine"], cfg["tm_gmlp"]}}

    n_pad = -(-n_seq // 8) * 8
    c_pad = jnp.zeros((n_pad, d), F32).at[0:n_seq].set(jnp.concatenate([c_prompt, c_sample], axis=0))
    mod_all = _ada_mod(c_pad, ada_w, ada_b)
    cos_t, sin_t = _rope_tables(max(seq_lens))

    for l in range(depth):
        mod = mod_all[l, 0:n_seq].reshape(n_seq, 6, d).transpose(1, 0, 2).reshape(6, n_seq, 1, d)
        i = l // 2
        if l % 2 == 0:
            tm = cfg["tm"]
            q, k, v, p = _inproj(x, metas[tm], mod, norm_mix_g[l], ab_w_in[i].astype(BF16),
                                 q_norm_g[i], k_norm_g[i], cos_t, sin_t, tm)
            sink_b = jnp.broadcast_to(
                jnp.repeat(attn_sink[i].reshape(N_KV_HEADS, GQA_GROUP), ATT_BLOCK, axis=1)[:, :, None],
                (N_KV_HEADS, GQA_GROUP * ATT_BLOCK, LANES)).astype(F32)
            a = _attention(q, k, v, sink_b, metas[tm], tm)
            x = _mix0(x, a, p, metas[tm], mod, pool_w[i].astype(BF16), pool_scale[i],
                      ab_w_out[i].astype(BF16), tm)
        else:
            tm = cfg["tm_gmlp"]
            if isinstance(x, tuple):
                x = jnp.concatenate(x, axis=0)
            u, v = _gmlp_in(x,metas[tm], mod, norm_mix_g[l], c_w_in[i].astype(BF16),
                            sgu_ln_g[i], sgu_ln_b[i], tm)
            sgu_b_b = jnp.broadcast_to(sgu_b[i][:, :, None], (SGU_HEADS, CHUNK, LANES)).astype(F32)
            x = _gmlp_out(x, u, v, metas[tm], mod, sgu_w[i].astype(BF16), sgu_b_b,
                          c_w_out[i].astype(BF16), tm)
        x = _moe(x, metas, mod, norm_ffn_g[l], router_group_w[l], router_group_b[l],
                 router_expert_w[l], router_expert_b[l], l, exp_w_gate, exp_w_up, exp_w_down, cfg,
                 split_rows=bp * sp if l == depth - 1 else None)

    y_prompt, y_sample = x
    return (y_prompt.reshape(bp, sp, d), y_sample.reshape(bs, ss, d))


def kernel(x_prompt, x_sample, c_prompt, c_sample, ada_w, ada_b, norm_mix_g, norm_ffn_g, ab_w_in, q_norm_g,
           k_norm_g, attn_sink, pool_w, pool_scale, ab_w_out, c_w_in, sgu_ln_g, sgu_ln_b, sgu_w, sgu_b,
           c_w_out, router_group_w, router_group_b, router_expert_w, router_expert_b, exp_w_gate, exp_w_up,
           exp_w_down):
    return _forward(x_prompt, x_sample, c_prompt, c_sample, ada_w, ada_b, norm_mix_g, norm_ffn_g, ab_w_in,
                    q_norm_g, k_norm_g, attn_sink, pool_w, pool_scale, ab_w_out, c_w_in, sgu_ln_g, sgu_ln_b,
                    sgu_w, sgu_b, c_w_out, router_group_w, router_group_b, router_expert_w, router_expert_b,
                    exp_w_gate, exp_w_up, exp_w_down)
```

```python
import functools

import numpy as np
import jax
import jax.numpy as jnp
from jax import lax
from jax.experimental import pallas as pl
from jax.experimental.pallas import tpu as pltpu

HEAD_DIM = 128
N_HEADS = 8
N_KV_HEADS = 2
GQA_GROUP = N_HEADS // N_KV_HEADS
ATT_BLOCK = 128
ROPE_THETA = 10000.0
Q_W = N_HEADS * HEAD_DIM
KV_W = N_KV_HEADS * HEAD_DIM
POOL_WINDOWS = (2, 4, 8, 16)
POOL_HALO = 8
CHUNK = 128
SGU_HEADS = 8
N_GROUPS = 4
EXP_PER_GROUP = 8
N_EXPERTS = N_GROUPS * EXP_PER_GROUP
EPS = 1e-6

VMEM_LIMIT_BYTES = 56 * 1024 * 1024
LANES = 128

F32 = jnp.float32
BF16 = jnp.bfloat16
NEG_BIG = -1e30


def _params(sem):
    return pltpu.CompilerParams(dimension_semantics=sem, vmem_limit_bytes=VMEM_LIMIT_BYTES)


def _resident(shape):
    nd = len(shape)
    return pl.BlockSpec(shape, lambda *_: (0,) * nd, pipeline_mode=pl.Buffered(1))


def _rows(tm, width):
    return pl.BlockSpec((tm, width), lambda i, *_: (i, 0))


def _split_rows(tm, width, n_a):
    return [pl.BlockSpec((tm, width), lambda i, *_: (jnp.minimum(i, n_a - 1), 0)),
            pl.BlockSpec((tm, width), lambda i, *_: (jnp.maximum(i - n_a, 0), 0))]


def _as_pair(x, tm):
    if isinstance(x, tuple):
        return x[0], x[1], x[0].shape[0] // tm
    return x, x, x.shape[0] // tm


def _mod_spec(part, d, seq_arg=0):
    return pl.BlockSpec((None, None, 1, d), lambda i, *pf: (part, pf[seq_arg][i], 0, 0))


def _norm_mod(x, g, sc, sh):
    r = lax.rsqrt(jnp.mean(x * x, axis=-1, keepdims=True) + EPS)
    return x * r * g * (1.0 + sc) + sh


def _dot(a, b):
    return jnp.dot(a, b, preferred_element_type=F32)


def _dot_nt(a, b):
    return lax.dot_general(a, b, (((1,), (1,)), ((), ())), preferred_element_type=F32)


def _ada_kernel(c_ref, w_ref, b_ref, o_ref):
    c = c_ref[...]
    cs = c * (1.0 / (1.0 + jnp.exp(-c)))
    o_ref[...] = _dot(cs.astype(BF16), w_ref[...].astype(BF16)) + b_ref[...]


def _ada_mod(c_pad, ada_w, ada_b):
    depth, d, n = ada_w.shape
    tn = 1024
    return pl.pallas_call(
        _ada_kernel,
        grid=(depth, n // tn),
        in_specs=[
            pl.BlockSpec(c_pad.shape, lambda l, j: (0, 0)),
            pl.BlockSpec((None, d, tn), lambda l, j: (l, 0, j)),
            pl.BlockSpec((None, 1, tn), lambda l, j: (l, 0, j)),
        ],
        out_specs=pl.BlockSpec((None, c_pad.shape[0], tn), lambda l, j: (l, 0, j)),
        out_shape=jax.ShapeDtypeStruct((depth, c_pad.shape[0], n), F32),
        compiler_params=_params(("arbitrary", "arbitrary")),
        name="ada_mod",
    )(c_pad, ada_w, ada_b.reshape(depth, 1, n))


def _inproj_kernel(ts, tp, xa_ref, xb_ref, g_ref, sc_ref, sh_ref, w_ref, qg_ref, kg_ref, cos_ref, sin_ref,
                   q_ref, k_ref, v_ref, p_ref, *, n_a):
    x = jnp.where(pl.program_id(0) < n_a, xa_ref[...], xb_ref[...])
    h = _norm_mod(x, g_ref[...], sc_ref[...], sh_ref[...]).astype(BF16)
    cos = cos_ref[...]
    sin = sin_ref[...]

    def head_norm_rope(y, gain):
        r = lax.rsqrt(jnp.mean(y * y, axis=-1, keepdims=True) + EPS)
        y = y * r * gain
        return y * cos + pltpu.roll(y, HEAD_DIM // 2, 1) * sin

    q = _dot(h, w_ref[:, 0:Q_W])
    for hh in range(N_HEADS):
        sl = slice(hh * HEAD_DIM, (hh + 1) * HEAD_DIM)
        q_ref[:, sl] = head_norm_rope(q[:, sl], qg_ref[...]).astype(BF16)
    kv = _dot(h, w_ref[:, Q_W:Q_W + 2 * KV_W])
    for hh in range(N_KV_HEADS):
        sl = slice(hh * HEAD_DIM, (hh + 1) * HEAD_DIM)
        k_ref[:, sl] = head_norm_rope(kv[:, sl], kg_ref[...]).astype(BF16)
    v_ref[...] = kv[:, KV_W:].astype(BF16)
    p_ref[...] = _dot(h, w_ref[:, Q_W + 2 * KV_W:])


def _inproj(x, meta, mod, norm_g, w_in_bf, q_g, k_g, cos_t, sin_t, tm):
    xa, xb, n_a = _as_pair(x, tm)
    d = xa.shape[1]
    n_tiles = meta["seq"].shape[0]
    t = n_tiles * tm
    pool_w = w_in_bf.shape[1] - Q_W - 2 * KV_W
    rope_spec = pl.BlockSpec((tm, HEAD_DIM), lambda i, ts, tp: (tp[i] // tm, 0))
    grid_spec = pltpu.PrefetchScalarGridSpec(
        num_scalar_prefetch=2,
        grid=(n_tiles,),
        in_specs=_split_rows(tm, d, n_a) + [
            _resident((1, d)),
            _mod_spec(1, d),
            _mod_spec(0, d),
            _resident(w_in_bf.shape),
            _resident((1, HEAD_DIM)),
            _resident((1, HEAD_DIM)),
            rope_spec,
            rope_spec,
        ],
        out_specs=[_rows(tm, Q_W), _rows(tm, KV_W), _rows(tm, KV_W), _rows(tm, pool_w)],
    )
    return pl.pallas_call(
        functools.partial(_inproj_kernel, n_a=n_a),
        grid_spec=grid_spec,
        out_shape=[
            jax.ShapeDtypeStruct((t, Q_W), BF16),
            jax.ShapeDtypeStruct((t, KV_W), BF16),
            jax.ShapeDtypeStruct((t, KV_W), BF16),
            jax.ShapeDtypeStruct((t, pool_w), F32),
        ],
        compiler_params=_params(("arbitrary",)),
        name="attn_pool_inproj",
    )(meta["seq"], meta["pos"], xa, xb, norm_g.reshape(1, d), mod, mod, w_in_bf,
      q_g.reshape(1, HEAD_DIM), k_g.reshape(1, HEAD_DIM), cos_t, sin_t)


def _attn_kernel(tp, tl, q_ref, kc_ref, kp_ref, kn_ref, vc_ref, vp_ref, vn_ref, sink_ref, o_ref,
                 kx, vx, s_scr, p_scr, r_scr):
    i = pl.program_id(0)
    tm = q_ref.shape[0]
    nb = tm // ATT_BLOCK
    first = tp[i] == 0
    last = tp[i] + tm == tl[i]
    kx[0:ATT_BLOCK] = kp_ref[...]
    kx[ATT_BLOCK:ATT_BLOCK + tm] = kc_ref[...]
    kx[ATT_BLOCK + tm:] = kn_ref[...]
    vx[0:ATT_BLOCK] = vp_ref[...]
    vx[ATT_BLOCK:ATT_BLOCK + tm] = vc_ref[...]
    vx[ATT_BLOCK + tm:] = vn_ref[...]
    win = 3 * ATT_BLOCK
    qi = lax.broadcasted_iota(jnp.int32, (ATT_BLOCK, win), 0)
    kj = lax.broadcasted_iota(jnp.int32, (ATT_BLOCK, win), 1)
    band = (kj >= qi) & (kj <= qi + 2 * ATT_BLOCK)
    scale = HEAD_DIM ** -0.5
    units = [(b, kk) for b in range(nb) for kk in range(N_KV_HEADS)]

    def heads_of(kk):
        return [kk * GQA_GROUP + g for g in range(GQA_GROUP)]

    for n, (b, kk) in enumerate(units):
        rows = slice(b * ATT_BLOCK, (b + 1) * ATT_BLOCK)
        kw = kx[b * ATT_BLOCK:b * ATT_BLOCK + win, kk * HEAD_DIM:(kk + 1) * HEAD_DIM]
        qs = jnp.concatenate([q_ref[rows, hd * HEAD_DIM:(hd + 1) * HEAD_DIM] for hd in heads_of(kk)], axis=0)
        s_scr[n] = _dot_nt(qs, kw)
    for n, (b, kk) in enumerate(units):
        valid = band
        if b == 0:
            valid = valid & (kj >= jnp.where(first, ATT_BLOCK, 0))
        if b == nb - 1:
            valid = valid & (kj < jnp.where(last, 2 * ATT_BLOCK, win))
        bias = jnp.where(valid, 0.0, -jnp.inf).astype(F32)
        s = s_scr[n] * scale
        s = (s.reshape(GQA_GROUP, ATT_BLOCK, win) + bias[None]).reshape(GQA_GROUP * ATT_BLOCK, win)
        sk = sink_ref[kk][:, 0:1]
        m = jnp.maximum(jnp.max(s, axis=-1, keepdims=True), sk)
        p = jnp.exp(s - m)
        denom = jnp.sum(p, axis=-1, keepdims=True) + jnp.exp(sk - m)
        p_scr[n] = p.astype(BF16)
        r_scr[n] = jnp.broadcast_to(1.0 / denom, r_scr.shape[1:])
    for n, (b, kk) in enumerate(units):
        rows = slice(b * ATT_BLOCK, (b + 1) * ATT_BLOCK)
        vw = vx[b * ATT_BLOCK:b * ATT_BLOCK + win, kk * HEAD_DIM:(kk + 1) * HEAD_DIM]
        o = _dot(p_scr[n], vw) * r_scr[n]
        for g, hd in enumerate(heads_of(kk)):
            o_ref[rows, hd * HEAD_DIM:(hd + 1) * HEAD_DIM] = o[g * ATT_BLOCK:(g + 1) * ATT_BLOCK].astype(BF16)


def _attention(q, k, v, sink_b, meta, tm):
    t = q.shape[0]
    n_tiles = t // tm
    r = tm // ATT_BLOCK
    n_blk = t // ATT_BLOCK
    cur = pl.BlockSpec((tm, KV_W), lambda i, *_: (i, 0))
    prev = pl.BlockSpec((ATT_BLOCK, KV_W), lambda i, *_: (jnp.maximum(i * r - 1, 0), 0))
    nxt = pl.BlockSpec((ATT_BLOCK, KV_W), lambda i, *_: (jnp.minimum((i + 1) * r, n_blk - 1), 0))
    grid_spec = pltpu.PrefetchScalarGridSpec(
        num_scalar_prefetch=2,
        grid=(n_tiles,),
        in_specs=[_rows(tm, Q_W), cur, prev, nxt, cur, prev, nxt, _resident(sink_b.shape)],
        out_specs=_rows(tm, Q_W),
        scratch_shapes=[pltpu.VMEM((tm + 2 * ATT_BLOCK, KV_W), BF16),
                        pltpu.VMEM((tm + 2 * ATT_BLOCK, KV_W), BF16),
                        pltpu.VMEM((r * N_KV_HEADS, GQA_GROUP * ATT_BLOCK, 3 * ATT_BLOCK), F32),
                        pltpu.VMEM((r * N_KV_HEADS, GQA_GROUP * ATT_BLOCK, 3 * ATT_BLOCK), BF16),
                        pltpu.VMEM((r * N_KV_HEADS, GQA_GROUP * ATT_BLOCK, HEAD_DIM), F32)],
    )
    return pl.pallas_call(
        _attn_kernel,
        grid_spec=grid_spec,
        out_shape=jax.ShapeDtypeStruct((t, Q_W), BF16),
        compiler_params=_params(("arbitrary",)),
        name="banded_attention",
    )(meta["pos"], meta["len"], q, k, k, k, v, v, v, sink_b)


def _mix0_kernel(tp, tl, ts, xa_ref, xb_ref, a_ref, pc_ref, pp_ref, pn_ref, gate_ref, pw_ref, ps_ref, wo_ref,
                 *rest, n_a):
    route_in, o_ref = rest[:N_ROUTE_IN], rest[N_ROUTE_IN]
    route_out, (pext, carry) = rest[N_ROUTE_IN + 1:N_ROUTE_IN + 1 + N_ROUTE_OUT], rest[-2:]
    i = pl.program_id(0)
    tm = xa_ref.shape[0]
    first = tp[i] == 0
    last = tp[i] + tm == tl[i]
    pext[0:POOL_HALO] = jnp.where(first, 0.0, pp_ref[...])
    pext[POOL_HALO:POOL_HALO + tm] = pc_ref[...]
    pext[POOL_HALO + tm:] = jnp.where(last, 0.0, pn_ref[...])
    n_g = len(POOL_WINDOWS)
    gw = pc_ref.shape[1] // n_g
    pos = tp[i] + lax.broadcasted_iota(jnp.int32, (tm, gw), 0)
    seq_len = tl[i]
    ms = []
    for g, w in enumerate(POOL_WINDOWS):
        cols = slice(g * gw, (g + 1) * gw)
        acc = pext[POOL_HALO - w // 2:POOL_HALO - w // 2 + tm, cols]
        for off in range(-w // 2 + 1, w // 2):
            acc = acc + pext[POOL_HALO + off:POOL_HALO + off + tm, cols]
        cnt = (jnp.minimum(pos + w // 2, seq_len) - jnp.maximum(pos - w // 2, 0)).astype(F32)
        dlt = acc / cnt - pc_ref[:, cols]
        ms.append((_dot(dlt.astype(BF16), pw_ref[g]) * ps_ref[:, cols]).astype(BF16))
    m = jnp.concatenate(ms, axis=1)
    w_a = a_ref.shape[1]
    mix = _dot(a_ref[...], wo_ref[0:w_a, :]) + _dot(m, wo_ref[w_a:, :])
    x = jnp.where(i < n_a, xa_ref[...], xb_ref[...])
    x_new = x + gate_ref[...] * mix
    o_ref[...] = x_new
    _route_tile(x_new, *route_in, *route_out, carry)


def _mix0(x, a, p, meta, mod, pool_w_bf, pool_scale, w_out_bf, route, tm):
    xa, xb, n_a = _as_pair(x, tm)
    d = xa.shape[1]
    t, pw = p.shape
    n_tiles = t // tm
    r = tm // POOL_HALO
    n_hb = t // POOL_HALO
    prev = pl.BlockSpec((POOL_HALO, pw), lambda i, *_: (jnp.maximum(i * r - 1, 0), 0))
    nxt = pl.BlockSpec((POOL_HALO, pw), lambda i, *_: (jnp.minimum((i + 1) * r, n_hb - 1), 0))
    r_in, r_out, r_shapes, r_scratch = _route_specs(tm, t, d, seq_arg=2)
    grid_spec = pltpu.PrefetchScalarGridSpec(
        num_scalar_prefetch=3,
        grid=(n_tiles,),
        in_specs=_split_rows(tm, d, n_a) + [
            _rows(tm, a.shape[1]), _rows(tm, pw), prev, nxt, _mod_spec(2, d, seq_arg=2),
            _resident(pool_w_bf.shape), _resident((1, pw)), _resident(w_out_bf.shape)] + r_in,
        out_specs=[_rows(tm, d)] + r_out,
        scratch_shapes=[pltpu.VMEM((tm + 2 * POOL_HALO, pw), F32), r_scratch],
    )
    return pl.pallas_call(
        functools.partial(_mix0_kernel, n_a=n_a),
        grid_spec=grid_spec,
        out_shape=[jax.ShapeDtypeStruct((t, d), F32)] + r_shapes,
        compiler_params=_params(("arbitrary",)),
        name="pool_outproj_route",
    )(meta["pos"], meta["len"], meta["seq"], xa, xb, a, p, p, p, mod, pool_w_bf,
      pool_scale.reshape(1, pw), w_out_bf, *_route_operands(route, mod, d))


def _gelu(z):
    return 0.5 * z * (1.0 + lax.erf(z * np.float32(np.sqrt(0.5))))


def _gmlp_in_kernel(ts, x_ref, g_ref, sc_ref, sh_ref, w_ref, lg_ref, lb_ref, u_ref, v_ref):
    h = _norm_mod(x_ref[...], g_ref[...], sc_ref[...], sh_ref[...]).astype(BF16)
    half = u_ref.shape[1]
    u_ref[...] = _gelu(_dot(h, w_ref[:, 0:half]))
    zv = _gelu(_dot(h, w_ref[:, half:]))
    zc = zv - jnp.mean(zv, axis=-1, keepdims=True)
    r = lax.rsqrt(jnp.mean(zc * zc, axis=-1, keepdims=True) + EPS)
    v_ref[...] = (zc * r * lg_ref[...] + lb_ref[...]).astype(BF16)


def _gmlp_in(x, meta, mod, norm_g, w_in_bf, ln_g, ln_b, tm):
    t, d = x.shape
    half = w_in_bf.shape[1] // 2
    grid_spec = pltpu.PrefetchScalarGridSpec(
        num_scalar_prefetch=1,
        grid=(t // tm,),
        in_specs=[_rows(tm, d), _resident((1, d)), _mod_spec(1, d), _mod_spec(0, d),
                  _resident(w_in_bf.shape), _resident((1, half)), _resident((1, half))],
        out_specs=[_rows(tm, half), _rows(tm, half)],
    )
    return pl.pallas_call(
        _gmlp_in_kernel,
        grid_spec=grid_spec,
        out_shape=[jax.ShapeDtypeStruct((t, half), F32), jax.ShapeDtypeStruct((t, half), BF16)],
        compiler_params=_params(("arbitrary",)),
        name="gmlp_in",
    )(meta["seq"], x, norm_g.reshape(1, d), mod, mod, w_in_bf, ln_g.reshape(1, half), ln_b.reshape(1, half))


def _gmlp_out_kernel(ts, x_ref, u_ref, v_ref, gate_ref, sw_ref, sb_ref, wo_ref, *rest):
    route_in, o_ref = rest[:N_ROUTE_IN], rest[N_ROUTE_IN]
    route_out, (gated, carry) = rest[N_ROUTE_IN + 1:N_ROUTE_IN + 1 + N_ROUTE_OUT], rest[-2:]
    tm = x_ref.shape[0]
    hd = u_ref.shape[1] // SGU_HEADS
    for c in range(tm // CHUNK):
        rows = slice(c * CHUNK, (c + 1) * CHUNK)
        for hh in range(SGU_HEADS):
            cols = slice(hh * hd, (hh + 1) * hd)
            s = _dot(sw_ref[hh], v_ref[rows, cols]) + jnp.tile(sb_ref[hh], (1, hd // LANES))
            gated[rows, cols] = (u_ref[rows, cols] * s).astype(BF16)
    x_new = x_ref[...] + gate_ref[...] * _dot(gated[...], wo_ref[...])
    o_ref[...] = x_new
    _route_tile(x_new, *route_in, *route_out, carry)


def _gmlp_out(x, u, v, meta, mod, sgu_w_bf, sgu_b_b, w_out_bf, route, tm):
    t, d = x.shape
    w = u.shape[1]
    r_in, r_out, r_shapes, r_scratch = _route_specs(tm, t, d, seq_arg=0)
    grid_spec = pltpu.PrefetchScalarGridSpec(
        num_scalar_prefetch=1,
        grid=(t // tm,),
        in_specs=[_rows(tm, d), _rows(tm, w), _rows(tm, w), _mod_spec(2, d),
                  _resident(sgu_w_bf.shape), _resident(sgu_b_b.shape), _resident(w_out_bf.shape)] + r_in,
        out_specs=[_rows(tm, d)] + r_out,
        scratch_shapes=[pltpu.VMEM((tm, w), BF16), r_scratch],
    )
    return pl.pallas_call(
        _gmlp_out_kernel,
        grid_spec=grid_spec,
        out_shape=[jax.ShapeDtypeStruct((t, d), F32)] + r_shapes,
        compiler_params=_params(("arbitrary",)),
        name="gmlp_out_route",
    )(meta["seq"], x, u, v, mod, sgu_w_bf, sgu_b_b, w_out_bf, *_route_operands(route, mod, d))


ROUTER_ROWS = 48


def _route_tile(x, g_ref, sc_ref, sh_ref, wr_ref, br_ref, h_ref, ri_ref, rw_ref, cnt_ref, carry):
    i = pl.program_id(0)

    @pl.when(i == 0)
    def _():
        carry[...] = jnp.zeros_like(carry)

    tm = x.shape[0]
    hb = _norm_mod(x, g_ref[...], sc_ref[...], sh_ref[...]).astype(BF16)
    h_ref[...] = _pack_bf16_pair(hb)
    lg = _dot_nt(wr_ref[...], hb) + jnp.tile(br_ref[...], (1, tm // LANES))
    rows8 = lax.broadcasted_iota(jnp.int32, (EXP_PER_GROUP, tm), 0)

    def first_argmax(vals, vmax):
        return jnp.min(jnp.where(vals == vmax, rows8, EXP_PER_GROUP), axis=0, keepdims=True)

    gl = lg[0:8]
    gmax = jnp.max(gl, axis=0, keepdims=True)
    gidx = first_argmax(gl, gmax)
    g_w = 1.0 / jnp.sum(jnp.exp(gl - gmax), axis=0, keepdims=True)
    esel = jnp.zeros((EXP_PER_GROUP, tm), F32)
    for g in range(N_GROUPS):
        esel = jnp.where(gidx == g, lg[8 + g * EXP_PER_GROUP:8 + (g + 1) * EXP_PER_GROUP], esel)
    emax = jnp.max(esel, axis=0, keepdims=True)
    pe = jnp.exp(esel - emax)
    prob = pe / jnp.sum(pe, axis=0, keepdims=True)
    p1 = jnp.max(prob, axis=0, keepdims=True)
    i1 = first_argmax(prob, p1)
    rest = jnp.where(rows8 == i1, -1.0, prob)
    p2 = jnp.max(rest, axis=0, keepdims=True)
    i2 = first_argmax(rest, p2)
    den = p1 + p2
    w0 = g_w * (p1 / den)
    w1 = g_w * (p2 / den)
    e0 = gidx * EXP_PER_GROUP + i1
    e1 = gidx * EXP_PER_GROUP + i2

    rows_e = lax.broadcasted_iota(jnp.int32, (N_EXPERTS, tm), 0)
    oh0 = rows_e == e0
    oh1 = rows_e == e1
    both = jnp.where(oh0, 1.0, 0.0) + jnp.where(oh1, 1.0, 0.0)
    ri = lax.broadcasted_iota(jnp.int32, (tm, tm), 0)
    ci = lax.broadcasted_iota(jnp.int32, (tm, tm), 1)
    upper = jnp.where(ri < ci, 1.0, 0.0).astype(BF16)
    before = _dot(both.astype(BF16), upper) + carry[:, 0:1]
    r0 = jnp.sum(jnp.where(oh0, before, 0.0), axis=0, keepdims=True)
    r1 = jnp.sum(jnp.where(oh1, before, 0.0), axis=0, keepdims=True)
    new_cnt = carry[...] + jnp.sum(both, axis=1, keepdims=True)
    carry[...] = new_cnt
    cnt_ref[...] = new_cnt
    zi = jnp.zeros((4, tm), jnp.int32)
    ri_ref[...] = jnp.concatenate([e0, e1, r0.astype(jnp.int32), r1.astype(jnp.int32), zi], axis=0)
    rw_ref[...] = jnp.concatenate([w0, w1, jnp.zeros((6, tm), F32)], axis=0)


N_ROUTE_IN, N_ROUTE_OUT = 5, 4


def _route_specs(tm, t, d, seq_arg):
    in_specs = [_resident((1, d)), _mod_spec(4, d, seq_arg), _mod_spec(3, d, seq_arg),
                _resident((ROUTER_ROWS, d)), _resident((ROUTER_ROWS, LANES))]
    out_specs = [_rows(tm, d // 2),
                 pl.BlockSpec((8, tm), lambda i, *_: (0, i)),
                 pl.BlockSpec((8, tm), lambda i, *_: (0, i)),
                 pl.BlockSpec((N_EXPERTS, LANES), lambda i, *_: (0, 0))]
    out_shapes = [jax.ShapeDtypeStruct((t, d // 2), jnp.uint32),
                  jax.ShapeDtypeStruct((8, t), jnp.int32),
                  jax.ShapeDtypeStruct((8, t), F32),
                  jax.ShapeDtypeStruct((N_EXPERTS, LANES), F32)]
    return in_specs, out_specs, out_shapes, pltpu.VMEM((N_EXPERTS, LANES), F32)


def _route_operands(route, mod, d):
    norm_g, wr_bf, br_b = route
    return [norm_g.reshape(1, d), mod, mod, wr_bf, br_b]


SUBLANES = 8


def _row_copy(src, s_grp, s_sub, dst, d_grp, d_sub, sem):
    return pltpu.make_async_copy(src.at[s_grp, pl.ds(s_sub, 1)], dst.at[d_grp, pl.ds(d_sub, 1)], sem)


def _split_row_index(idx):
    return jnp.concatenate([idx >> 3, idx & 7], axis=-1)


def _dispatch_kernel(pad_lo, pad_hi, slot_ref, hp_ref, xs_hbm, zrow, sem, zsem):
    i = pl.program_id(0)
    ch = slot_ref.shape[2] // 4
    n_grp = ch // SUBLANES

    @pl.when(i == 0)
    def _():
        zrow[...] = jnp.zeros_like(zrow)

        def per_range(e, c):
            lo, hi = pad_lo[e], pad_hi[e]
            lo8 = jnp.minimum(((lo + SUBLANES - 1) >> 3) << 3, hi)

            def row(r):
                return _row_copy(zrow, 0, 0, xs_hbm, r >> 3, r & 7, zsem)

            def grp(g):
                return pltpu.make_async_copy(zrow, xs_hbm.at[pl.ds(g, 1)], zsem)

            lax.fori_loop(lo, lo8, lambda r, c2: (row(r).start(), c2)[1], 0)
            lax.fori_loop(lo8 >> 3, hi >> 3, lambda g, c2: (grp(g).start(), c2)[1], 0)
            lax.fori_loop(lo, lo8, lambda r, c2: (row(r).wait(), c2)[1], 0)
            lax.fori_loop(lo8 >> 3, hi >> 3, lambda g, c2: (grp(g).wait(), c2)[1], 0)
            return c

        lax.fori_loop(0, pad_lo.shape[0], per_range, 0)

    def issue(q, c):
        for u in range(SUBLANES):
            r = q * SUBLANES + u
            for k in range(2):
                _row_copy(hp_ref, q, u, xs_hbm, slot_ref[0, 0, k * ch + r], slot_ref[0, 0, (2 + k) * ch + r],
                          sem).start()
        return c

    lax.fori_loop(0, n_grp, issue, 0)
    for k in range(2):
        pltpu.make_async_copy(hp_ref, xs_hbm.at[pl.ds(0, n_grp)], sem).wait()


def _dispatch(hp, slots, pad_lo, pad_hi, n_rows, ch):
    t, w = hp.shape
    grid_spec = pltpu.PrefetchScalarGridSpec(
        num_scalar_prefetch=2,
        grid=(t // ch,),
        in_specs=[
            pl.BlockSpec((1, 1, 4 * ch), lambda i, *_: (i, 0, 0), memory_space=pltpu.SMEM),
            pl.BlockSpec((ch // SUBLANES, SUBLANES, w), lambda i, *_: (i, 0, 0)),
        ],
        out_specs=pl.BlockSpec(memory_space=pl.ANY),
        scratch_shapes=[pltpu.VMEM((1, SUBLANES, w), hp.dtype), pltpu.SemaphoreType.DMA,
                        pltpu.SemaphoreType.DMA],
    )
    xs = pl.pallas_call(
        _dispatch_kernel,
        grid_spec=grid_spec,
        out_shape=jax.ShapeDtypeStruct((n_rows // SUBLANES, SUBLANES, w), hp.dtype),
        compiler_params=_params(("arbitrary",)),
        name="moe_dispatch",
    )(pad_lo, pad_hi, _split_row_index(slots), hp.reshape(t // SUBLANES, SUBLANES, w))
    return xs.reshape(n_rows, w)


FFN_ROW_STEPS = 4


def _pack_bf16_pair(x):
    half = x.shape[1] // 2
    bits = pltpu.bitcast(x.astype(BF16).astype(F32), jnp.uint32)
    return (bits[:, :half] >> 16) | bits[:, half:]


def _unpack_bf16_pair(xp):
    return pltpu.bitcast(xp << 16, F32), pltpu.bitcast(xp & jnp.uint32(0xFFFF0000), F32)


def _ffn_kernel(be, nv, jm, xs_ref, wg_ref, wu_ref, wd_ref, o_ref, xb, acc, *, n_ff):
    i = pl.program_id(0)
    j = pl.program_id(1)
    rb = o_ref.shape[0]
    step = rb // FFN_ROW_STEPS
    n = nv[i]

    @pl.when((j == 0) & (n > 0))
    def _():
        lo, hi = _unpack_bf16_pair(xs_ref[...])
        half = lo.shape[1]
        xb[:, 0:half] = lo.astype(BF16)
        xb[:, half:] = hi.astype(BF16)

    def ffn_rows(rows):
        x = xb[0:rows]
        hg = _dot(x, wg_ref[...].astype(BF16))
        hu = _dot(x, wu_ref[...].astype(BF16))
        act = (hg * (1.0 / (1.0 + jnp.exp(-hg))) * hu).astype(BF16)
        part = _dot(act, wd_ref[...].astype(BF16))

        def finish(total):
            o_ref[0:rows] = _pack_bf16_pair(total)
            if rows < rb:
                o_ref[rows:] = jnp.zeros((rb - rows, o_ref.shape[1]), o_ref.dtype)

        if n_ff == 1:
            finish(part)
            return

        @pl.when(j == 0)
        def _():
            acc[0:rows] = part

        @pl.when((j > 0) & (j < n_ff - 1))
        def _():
            acc[0:rows] += part

        @pl.when(j == n_ff - 1)
        def _():
            finish(acc[0:rows] + part)

    @pl.when((n == 0) & (j == 0))
    def _():
        o_ref[...] = jnp.zeros_like(o_ref)

    for k in range(1, FFN_ROW_STEPS + 1):
        pl.when((n > (k - 1) * step) & (n <= k * step))(functools.partial(ffn_rows, k * step))


FF_FORWARD, FF_BACKWARD, FF_HOLD_FIRST, FF_HOLD_LAST = 0, 1, 2, 3


def _expert_ffn(xs, blk_expert, blk_valid, blk_ff_order, layer, wg, wu, wd, rb, ft):
    n_rows, w = xs.shape
    d = 2 * w
    n_blocks = n_rows // rb
    ff = wg.shape[3]
    n_ff = ff // ft

    def ffj(i, j, jm):
        m = jm[i]
        return jnp.where(m == FF_FORWARD, j,
                         jnp.where(m == FF_BACKWARD, n_ff - 1 - j, jnp.where(m == FF_HOLD_FIRST, 0, n_ff - 1)))

    grid_spec = pltpu.PrefetchScalarGridSpec(
        num_scalar_prefetch=3,
        grid=(n_blocks, n_ff),
        in_specs=[
            pl.BlockSpec((rb, w), lambda i, j, be, nv, jm: (i, 0)),
            pl.BlockSpec((None, None, d, ft), lambda i, j, be, nv, jm: (layer, be[i], 0, ffj(i, j, jm))),
            pl.BlockSpec((None, None, d, ft), lambda i, j, be, nv, jm: (layer, be[i], 0, ffj(i, j, jm))),
            pl.BlockSpec((None, None, ft, d), lambda i, j, be, nv, jm: (layer, be[i], ffj(i, j, jm), 0)),
        ],
        out_specs=pl.BlockSpec((rb, w), lambda i, j, be, nv, jm: (i, 0)),
        scratch_shapes=[pltpu.VMEM((rb, d), BF16), pltpu.VMEM((rb, d), F32)],
    )
    return pl.pallas_call(
        functools.partial(_ffn_kernel, n_ff=n_ff),
        grid_spec=grid_spec,
        out_shape=jax.ShapeDtypeStruct((n_rows, w), jnp.uint32),
        compiler_params=_params(("arbitrary", "arbitrary")),
        name="expert_ffn",
    )(blk_expert, blk_valid, blk_ff_order, xs, wg, wu, wd)


def _combine_kernel(ts, idx_ref, nxt_ref, x_ref, wt_ref, gate_ref, ys_hbm, *rest, n_a):
    outs, (buf, sem) = rest[:-2], rest[-2:]
    i = pl.program_id(0)
    tm = x_ref.shape[0]
    cur = i % 2

    n_grp = 2 * tm // SUBLANES

    def fetch(idx, b):
        def issue(q, c):
            for u in range(SUBLANES):
                r = q * SUBLANES + u
                _row_copy(ys_hbm, idx[0, 0, r], idx[0, 0, 2 * tm + r], buf.at[b], q, u, sem.at[b]).start()
            return c

        lax.fori_loop(0, n_grp, issue, 0)

    @pl.when(i == 0)
    def _():
        fetch(idx_ref, 0)

    @pl.when(i + 1 < pl.num_programs(0))
    def _():
        fetch(nxt_ref, 1 - cur)

    pltpu.make_async_copy(buf.at[cur], buf.at[cur], sem.at[cur]).wait()
    w = buf.shape[-1]
    y0 = _unpack_bf16_pair(buf[cur, 0:n_grp // 2].reshape(tm, w))
    y1 = _unpack_bf16_pair(buf[cur, n_grp // 2:n_grp].reshape(tm, w))
    w0, w1 = wt_ref[:, 0:1], wt_ref[:, 1:2]
    y = jnp.concatenate([w0 * y0[0] + w1 * y1[0], w0 * y0[1] + w1 * y1[1]], axis=1)
    res = x_ref[...] + gate_ref[...] * y
    if n_a is None:
        outs[0][...] = res
    else:
        @pl.when(i < n_a)
        def _():
            outs[0][...] = res

        @pl.when(i >= n_a)
        def _():
            outs[1][...] = res


def _combine(x, ys, slots, wt, meta, mod, tm, split_rows=None):
    t, d = x.shape
    n_tiles = t // tm
    if split_rows is None:
        n_a, out_specs = None, _rows(tm, d)
        out_shape = jax.ShapeDtypeStruct((t, d), F32)
    else:
        n_a, out_specs = split_rows // tm, _split_rows(tm, d, split_rows // tm)
        out_shape = [jax.ShapeDtypeStruct((split_rows, d), F32), jax.ShapeDtypeStruct((t - split_rows, d), F32)]
    gate_spec = pl.BlockSpec((None, None, 1, d), lambda i, ts: (5, ts[i], 0, 0))
    grid_spec = pltpu.PrefetchScalarGridSpec(
        num_scalar_prefetch=1,
        grid=(n_tiles,),
        in_specs=[
            pl.BlockSpec((1, 1, 4 * tm), lambda i, ts: (i, 0, 0), memory_space=pltpu.SMEM),
            pl.BlockSpec((1, 1, 4 * tm), lambda i, ts: (jnp.minimum(i + 1, n_tiles - 1), 0, 0),
                         memory_space=pltpu.SMEM),
            _rows(tm, d),
            _rows(tm, wt.shape[1]),
            gate_spec,
            pl.BlockSpec(memory_space=pl.ANY),
        ],
        out_specs=out_specs,
        scratch_shapes=[pltpu.VMEM((2, 2 * tm // SUBLANES, SUBLANES, ys.shape[1]), ys.dtype),
                        pltpu.SemaphoreType.DMA((2,))],
    )
    slots = _split_row_index(slots)
    ys = ys.reshape(ys.shape[0] // SUBLANES, SUBLANES, ys.shape[1])
    return pl.pallas_call(
        functools.partial(_combine_kernel, n_a=n_a),
        grid_spec=grid_spec,
        out_shape=out_shape,
        compiler_params=_params(("arbitrary",)),
        name="moe_combine",
    )(meta["seq"], slots, slots, x, wt, mod, ys)


def _tile_meta(seq_lens, tm):
    seq, pos, ln = [], [], []
    for s, n in enumerate(seq_lens):
        assert n % tm == 0
        for k in range(n // tm):
            seq.append(s)
            pos.append(k * tm)
            ln.append(n)
    return {k: jnp.asarray(np.asarray(v, np.int32)) for k, v in (("seq", seq), ("pos", pos), ("len", ln))}


def _rope_tables(s_max):
    half = HEAD_DIM // 2
    inv = ROPE_THETA ** (-jnp.arange(half, dtype=F32) / half)
    ang = jnp.arange(s_max, dtype=F32)[:, None] * inv[None, :]
    cos, sin = jnp.cos(ang), jnp.sin(ang)
    return jnp.concatenate([cos, cos], axis=1), jnp.concatenate([-sin, sin], axis=1)


def _tile_slots(slot, tm):
    t = slot.shape[1]
    return slot.reshape(2, t // tm, tm).transpose(1, 0, 2).reshape(t // tm, 1, 2 * tm)


def _route_params(norm_g, rg_w, rg_b, re_w, re_b):
    d = rg_w.shape[0]
    wr = jnp.zeros((ROUTER_ROWS, d), F32).at[0:N_GROUPS].set(rg_w.T).at[8:8 + N_EXPERTS].set(re_w.T)
    br = jnp.zeros((ROUTER_ROWS,), F32).at[N_GROUPS:8].set(NEG_BIG).at[0:N_GROUPS].set(rg_b)
    br = br.at[8:8 + N_EXPERTS].set(re_b)
    return norm_g, wr.astype(BF16), jnp.broadcast_to(br[:, None], (ROUTER_ROWS, LANES))


def _moe(x, routed, metas, mod, layer, wg, wu, wd, cfg, split_rows=None):
    t, d = x.shape
    hp, ri, rw, cnt = routed

    rb = cfg["rb"]
    expert = ri[0:2]
    rank = ri[2:4]
    counts = cnt[:, 0].astype(jnp.int32)
    nblk = (counts + rb - 1) // rb
    blk_end = jnp.cumsum(nblk)
    row_start = (blk_end - nblk) * rb
    ids = jnp.arange(N_EXPERTS, dtype=jnp.int32)
    start_of = jnp.sum(jnp.where(expert[:, :, None] == ids, row_start, 0), axis=-1)
    slot = start_of + rank
    n_blocks = -(-2 * t // rb) + N_EXPERTS
    n_used = blk_end[-1]
    blk = jnp.arange(n_blocks, dtype=jnp.int32)
    blk_expert = jnp.minimum(jnp.sum(blk[:, None] >= blk_end[None, :], axis=1), N_EXPERTS - 1).astype(jnp.int32)
    last_used = jnp.sum(jnp.where(blk == n_used - 1, blk_expert, 0))
    blk_expert = jnp.where(blk < n_used, blk_expert, last_used)
    onehot = blk_expert[:, None] == ids
    row_end_of = jnp.sum(jnp.where(onehot, row_start + counts, 0), axis=1)
    blk_valid = jnp.where(blk < n_used, jnp.clip(row_end_of - blk * rb, 0, rb), 0).astype(jnp.int32)
    first_blk_of = jnp.sum(jnp.where(onehot, blk_end - nblk, 0), axis=1)
    order = (blk - first_blk_of) & 1
    last_order = jnp.sum(jnp.where(blk == n_used - 1, order, 0))
    hold = jnp.where(last_order == FF_FORWARD, FF_HOLD_LAST, FF_HOLD_FIRST)
    blk_ff_order = jnp.where(blk < n_used, order, hold).astype(jnp.int32)
    pad_lo = jnp.concatenate([row_start + counts, (n_used * rb).reshape(1)])
    pad_hi = jnp.concatenate([blk_end * rb, jnp.full((1,), n_blocks * rb, jnp.int32)])

    ch = cfg["ch_dispatch"]
    xs = _dispatch(hp, _tile_slots(slot, ch), pad_lo.astype(jnp.int32), pad_hi.astype(jnp.int32),
                   n_blocks * rb, ch)
    ys = _expert_ffn(xs, blk_expert, blk_valid, blk_ff_order, layer, wg, wu, wd, rb, cfg["ft"])

    tmc = cfg["tm_combine"]
    wt = rw.T
    return _combine(x, ys, _tile_slots(slot, tmc), wt, metas[tmc], mod, tmc, split_rows)


def _config(seq_lens):
    g = int(np.gcd.reduce(np.asarray(seq_lens)))
    tm = min(512, g)
    return {"tm": tm, "tm_combine": min(512, g), "tm_gmlp": min(512, g), "rb": 1024, "ft": 256,
            "ch_dispatch": min(1024, g)}


def _forward(x_prompt, x_sample, c_prompt, c_sample, ada_w, ada_b, norm_mix_g, norm_ffn_g,
             ab_w_in, q_norm_g, k_norm_g, attn_sink, pool_w, pool_scale, ab_w_out,
             c_w_in, sgu_ln_g, sgu_ln_b, sgu_w, sgu_b, c_w_out,
             router_group_w, router_group_b, router_expert_w, router_expert_b,
             exp_w_gate, exp_w_up, exp_w_down, cfg=None):
    bp, sp, d = x_prompt.shape
    bs, ss, _ = x_sample.shape
    seq_lens = [sp] * bp + [ss] * bs
    n_seq = len(seq_lens)
    if cfg is None:
        cfg = _config(seq_lens)
    depth = ada_w.shape[0]
    x = (x_prompt.reshape(bp * sp, d), x_sample.reshape(bs * ss, d))
    metas ={tm: _tile_meta(seq_lens, tm) for tm in
             {cfg["tm"], cfg["tm_combine"], cfg["tm_gmlp"]}}

    n_pad = -(-n_seq // 8) * 8
    c_pad = jnp.zeros((n_pad, d), F32).at[0:n_seq].set(jnp.concatenate([c_prompt, c_sample], axis=0))
    mod_all = _ada_mod(c_pad, ada_w, ada_b)
    cos_t, sin_t = _rope_tables(max(seq_lens))

    for l in range(depth):
        mod = mod_all[l, 0:n_seq].reshape(n_seq, 6, d).transpose(1, 0, 2).reshape(6, n_seq, 1, d)
        i = l // 2
        route = _route_params(norm_ffn_g[l], router_group_w[l], router_group_b[l],
                              router_expert_w[l], router_expert_b[l])
        if l % 2 == 0:
            tm = cfg["tm"]
            q, k, v, p = _inproj(x, metas[tm], mod, norm_mix_g[l], ab_w_in[i].astype(BF16),
                                 q_norm_g[i], k_norm_g[i], cos_t, sin_t, tm)
            sink_b = jnp.broadcast_to(
                jnp.repeat(attn_sink[i].reshape(N_KV_HEADS, GQA_GROUP), ATT_BLOCK, axis=1)[:, :, None],
                (N_KV_HEADS, GQA_GROUP * ATT_BLOCK, LANES)).astype(F32)
            a = _attention(q, k, v, sink_b, metas[tm], tm)
            x, *routed = _mix0(x, a, p, metas[tm], mod, pool_w[i].astype(BF16), pool_scale[i],
                               ab_w_out[i].astype(BF16), route, tm)
        else:
            tm = cfg["tm_gmlp"]
            if isinstance(x, tuple):
                x = jnp.concatenate(x, axis=0)
            u, v = _gmlp_in(x,metas[tm], mod, norm_mix_g[l], c_w_in[i].astype(BF16),
                            sgu_ln_g[i], sgu_ln_b[i], tm)
            sgu_b_b = jnp.broadcast_to(sgu_b[i][:, :, None], (SGU_HEADS, CHUNK, LANES)).astype(F32)
            x, *routed = _gmlp_out(x, u, v, metas[tm], mod, sgu_w[i].astype(BF16), sgu_b_b,
                                   c_w_out[i].astype(BF16), route, tm)
        x = _moe(x, routed, metas, mod, l, exp_w_gate, exp_w_up, exp_w_down, cfg,
                 split_rows=bp * sp if l == depth - 1 else None)

    y_prompt, y_sample = x
    return (y_prompt.reshape(bp, sp, d), y_sample.reshape(bs, ss, d))


def kernel(x_prompt, x_sample, c_prompt, c_sample, ada_w, ada_b, norm_mix_g, norm_ffn_g, ab_w_in, q_norm_g,
           k_norm_g, attn_sink, pool_w, pool_scale, ab_w_out, c_w_in, sgu_ln_g, sgu_ln_b, sgu_w, sgu_b,
           c_w_out, router_group_w, router_group_b, router_expert_w, router_expert_b, exp_w_gate, exp_w_up,
           exp_w_down):
    return _forward(x_prompt, x_sample, c_prompt, c_sample, ada_w, ada_b, norm_mix_g, norm_ffn_g, ab_w_in,
                    q_norm_g, k_norm_g, attn_sink, pool_w, pool_scale, ab_w_out, c_w_in, sgu_ln_g, sgu_ln_b,
                    sgu_w, sgu_b, c_w_out, router_group_w, router_group_b, router_expert_w, router_expert_b,
                    exp_w_gate, exp_w_up, exp_w_down)
```

```python
import functools

import numpy as np
import jax
import jax.numpy as jnp
from jax import lax
from jax.experimental import pallas as pl
from jax.experimental.pallas import tpu as pltpu

HEAD_DIM = 128
N_HEADS = 8
N_KV_HEADS = 2
GQA_GROUP = N_HEADS // N_KV_HEADS
ATT_BLOCK = 128
ROPE_THETA = 10000.0
Q_W = N_HEADS * HEAD_DIM
KV_W = N_KV_HEADS * HEAD_DIM
POOL_WINDOWS = (2, 4, 8, 16)
POOL_HALO = 8
CHUNK = 128
SGU_HEADS = 8
N_GROUPS = 4
EXP_PER_GROUP = 8
N_EXPERTS = N_GROUPS * EXP_PER_GROUP
EPS = 1e-6

VMEM_LIMIT_BYTES = 56 * 1024 * 1024
LANES = 128

F32 = jnp.float32
BF16 = jnp.bfloat16
NEG_BIG = -1e30


def _params(sem):
    return pltpu.CompilerParams(dimension_semantics=sem, vmem_limit_bytes=VMEM_LIMIT_BYTES)


def _resident(shape):
    nd = len(shape)
    return pl.BlockSpec(shape, lambda *_: (0,) * nd, pipeline_mode=pl.Buffered(1))


def _rows(tm, width):
    return pl.BlockSpec((tm, width), lambda i, *_: (i, 0))


def _split_rows(tm, width, n_a):
    return [pl.BlockSpec((tm, width), lambda i, *_: (jnp.minimum(i, n_a - 1), 0)),
            pl.BlockSpec((tm, width), lambda i, *_: (jnp.maximum(i - n_a, 0), 0))]


def _as_pair(x, tm):
    if isinstance(x, tuple):
        return x[0], x[1], x[0].shape[0] // tm
    return x, x, x.shape[0] // tm


def _mod_spec(part, d, seq_arg=0):
    return pl.BlockSpec((None, None, 1, d), lambda i, *pf: (part, pf[seq_arg][i], 0, 0))


def _norm_mod(x, g, sc, sh):
    r = lax.rsqrt(jnp.mean(x * x, axis=-1, keepdims=True) + EPS)
    return x * r * g * (1.0 + sc) + sh


def _dot(a, b):
    return jnp.dot(a, b, preferred_element_type=F32)


def _dot_nt(a, b):
    return lax.dot_general(a, b, (((1,), (1,)), ((), ())), preferred_element_type=F32)


def _ada_kernel(c_ref, w_ref, b_ref, o_ref):
    c = c_ref[...]
    cs = c * (1.0 / (1.0 + jnp.exp(-c)))
    o_ref[...] = _dot(cs.astype(BF16), w_ref[...].astype(BF16)) + b_ref[...]


def _ada_mod(c_pad, ada_w, ada_b):
    depth, d, n = ada_w.shape
    tn = 1024
    return pl.pallas_call(
        _ada_kernel,
        grid=(depth, n // tn),
        in_specs=[
            pl.BlockSpec(c_pad.shape, lambda l, j: (0, 0)),
            pl.BlockSpec((None, d, tn), lambda l, j: (l, 0, j)),
            pl.BlockSpec((None, 1, tn), lambda l, j: (l, 0, j)),
        ],
        out_specs=pl.BlockSpec((None, c_pad.shape[0], tn), lambda l, j: (l, 0, j)),
        out_shape=jax.ShapeDtypeStruct((depth, c_pad.shape[0], n), F32),
        compiler_params=_params(("arbitrary", "arbitrary")),
        name="ada_mod",
    )(c_pad, ada_w, ada_b.reshape(depth, 1, n))


def _inproj_kernel(ts, tp, xa_ref, xb_ref, g_ref, sc_ref, sh_ref, w_ref, qg_ref, kg_ref, cos_ref, sin_ref,
                   q_ref, k_ref, v_ref, p_ref, *, n_a):
    x = jnp.where(pl.program_id(0) < n_a, xa_ref[...], xb_ref[...])
    h = _norm_mod(x, g_ref[...], sc_ref[...], sh_ref[...]).astype(BF16)
    cos = cos_ref[...]
    sin = sin_ref[...]

    def head_norm_rope(y, gain):
        r = lax.rsqrt(jnp.mean(y * y, axis=-1, keepdims=True) + EPS)
        y = y * r * gain
        return y * cos + pltpu.roll(y, HEAD_DIM // 2, 1) * sin

    q = _dot(h, w_ref[:, 0:Q_W])
    for hh in range(N_HEADS):
        sl = slice(hh * HEAD_DIM, (hh + 1) * HEAD_DIM)
        q_ref[:, sl] = head_norm_rope(q[:, sl], qg_ref[...]).astype(BF16)
    kv = _dot(h, w_ref[:, Q_W:Q_W + 2 * KV_W])
    for hh in range(N_KV_HEADS):
        sl = slice(hh * HEAD_DIM, (hh + 1) * HEAD_DIM)
        k_ref[:, sl] = head_norm_rope(kv[:, sl], kg_ref[...]).astype(BF16)
    v_ref[...] = kv[:, KV_W:].astype(BF16)
    p_ref[...] = _dot(h, w_ref[:, Q_W + 2 * KV_W:])


def _inproj(x, meta, mod, norm_g, w_in_bf, q_g, k_g, cos_t, sin_t, tm):
    xa, xb, n_a = _as_pair(x, tm)
    d = xa.shape[1]
    n_tiles = meta["seq"].shape[0]
    t = n_tiles * tm
    pool_w = w_in_bf.shape[1] - Q_W - 2 * KV_W
    rope_spec = pl.BlockSpec((tm, HEAD_DIM), lambda i, ts, tp: (tp[i] // tm, 0))
    grid_spec = pltpu.PrefetchScalarGridSpec(
        num_scalar_prefetch=2,
        grid=(n_tiles,),
        in_specs=_split_rows(tm, d, n_a) + [
            _resident((1, d)),
            _mod_spec(1, d),
            _mod_spec(0, d),
            _resident(w_in_bf.shape),
            _resident((1, HEAD_DIM)),
            _resident((1, HEAD_DIM)),
            rope_spec,
            rope_spec,
        ],
        out_specs=[_rows(tm, Q_W), _rows(tm, KV_W), _rows(tm, KV_W), _rows(tm, pool_w)],
    )
    return pl.pallas_call(
        functools.partial(_inproj_kernel, n_a=n_a),
        grid_spec=grid_spec,
        out_shape=[
            jax.ShapeDtypeStruct((t, Q_W), BF16),
            jax.ShapeDtypeStruct((t, KV_W), BF16),
            jax.ShapeDtypeStruct((t, KV_W), BF16),
            jax.ShapeDtypeStruct((t, pool_w), F32),
        ],
        compiler_params=_params(("arbitrary",)),
        name="attn_pool_inproj",
    )(meta["seq"], meta["pos"], xa, xb, norm_g.reshape(1, d), mod, mod, w_in_bf,
      q_g.reshape(1, HEAD_DIM), k_g.reshape(1, HEAD_DIM), cos_t, sin_t)


def _attn_kernel(tp, tl, q_ref, kc_ref, kp_ref, kn_ref, vc_ref, vp_ref, vn_ref, sink_ref, o_ref,
                 kx, vx, s_scr, p_scr, r_scr):
    i = pl.program_id(0)
    tm = q_ref.shape[0]
    nb = tm // ATT_BLOCK
    first = tp[i] == 0
    last = tp[i] + tm == tl[i]
    kx[0:ATT_BLOCK] = kp_ref[...]
    kx[ATT_BLOCK:ATT_BLOCK + tm] = kc_ref[...]
    kx[ATT_BLOCK + tm:] = kn_ref[...]
    vx[0:ATT_BLOCK] = vp_ref[...]
    vx[ATT_BLOCK:ATT_BLOCK + tm] = vc_ref[...]
    vx[ATT_BLOCK + tm:] = vn_ref[...]
    win = 3 * ATT_BLOCK
    qi = lax.broadcasted_iota(jnp.int32, (ATT_BLOCK, win), 0)
    kj = lax.broadcasted_iota(jnp.int32, (ATT_BLOCK, win), 1)
    band = (kj >= qi) & (kj <= qi + 2 * ATT_BLOCK)
    scale = HEAD_DIM ** -0.5
    units = [(b, kk) for b in range(nb) for kk in range(N_KV_HEADS)]

    def heads_of(kk):
        return [kk * GQA_GROUP + g for g in range(GQA_GROUP)]

    for n, (b, kk) in enumerate(units):
        rows = slice(b * ATT_BLOCK, (b + 1) * ATT_BLOCK)
        kw = kx[b * ATT_BLOCK:b * ATT_BLOCK + win, kk * HEAD_DIM:(kk + 1) * HEAD_DIM]
        qs = jnp.concatenate([q_ref[rows, hd * HEAD_DIM:(hd + 1) * HEAD_DIM] for hd in heads_of(kk)], axis=0)
        s_scr[n] = _dot_nt(qs, kw)
    for n, (b, kk) in enumerate(units):
        valid = band
        if b == 0:
            valid = valid & (kj >= jnp.where(first, ATT_BLOCK, 0))
        if b == nb - 1:
            valid = valid & (kj < jnp.where(last, 2 * ATT_BLOCK, win))
        bias = jnp.where(valid, 0.0, -jnp.inf).astype(F32)
        s = s_scr[n] * scale
        s = (s.reshape(GQA_GROUP, ATT_BLOCK, win) + bias[None]).reshape(GQA_GROUP * ATT_BLOCK, win)
        sk = sink_ref[kk][:, 0:1]
        m = jnp.maximum(jnp.max(s, axis=-1, keepdims=True), sk)
        p = jnp.exp(s - m)
        denom = jnp.sum(p, axis=-1, keepdims=True) + jnp.exp(sk - m)
        p_scr[n] = p.astype(BF16)
        r_scr[n] = jnp.broadcast_to(1.0 / denom, r_scr.shape[1:])
    for n, (b, kk) in enumerate(units):
        rows = slice(b * ATT_BLOCK, (b + 1) * ATT_BLOCK)
        vw = vx[b * ATT_BLOCK:b * ATT_BLOCK + win, kk * HEAD_DIM:(kk + 1) * HEAD_DIM]
        o = _dot(p_scr[n], vw) * r_scr[n]
        for g, hd in enumerate(heads_of(kk)):
            o_ref[rows, hd * HEAD_DIM:(hd + 1) * HEAD_DIM] = o[g * ATT_BLOCK:(g + 1) * ATT_BLOCK].astype(BF16)


def _attention(q, k, v, sink_b, meta, tm):
    t = q.shape[0]
    n_tiles = t // tm
    r = tm // ATT_BLOCK
    n_blk = t // ATT_BLOCK
    cur = pl.BlockSpec((tm, KV_W), lambda i, *_: (i, 0))
    prev = pl.BlockSpec((ATT_BLOCK, KV_W), lambda i, *_: (jnp.maximum(i * r - 1, 0), 0))
    nxt = pl.BlockSpec((ATT_BLOCK, KV_W), lambda i, *_: (jnp.minimum((i + 1) * r, n_blk - 1), 0))
    grid_spec = pltpu.PrefetchScalarGridSpec(
        num_scalar_prefetch=2,
        grid=(n_tiles,),
        in_specs=[_rows(tm, Q_W), cur, prev, nxt, cur, prev, nxt, _resident(sink_b.shape)],
        out_specs=_rows(tm, Q_W),
        scratch_shapes=[pltpu.VMEM((tm + 2 * ATT_BLOCK, KV_W), BF16),
                        pltpu.VMEM((tm + 2 * ATT_BLOCK, KV_W), BF16),
                        pltpu.VMEM((r * N_KV_HEADS, GQA_GROUP * ATT_BLOCK, 3 * ATT_BLOCK), F32),
                        pltpu.VMEM((r * N_KV_HEADS, GQA_GROUP * ATT_BLOCK, 3 * ATT_BLOCK), BF16),
                        pltpu.VMEM((r * N_KV_HEADS, GQA_GROUP * ATT_BLOCK, HEAD_DIM), F32)],
    )
    return pl.pallas_call(
        _attn_kernel,
        grid_spec=grid_spec,
        out_shape=jax.ShapeDtypeStruct((t, Q_W), BF16),
        compiler_params=_params(("arbitrary",)),
        name="banded_attention",
    )(meta["pos"], meta["len"], q, k, k, k, v, v, v, sink_b)


def _mix0_kernel(tp, tl, ts, xa_ref, xb_ref, a_ref, pc_ref, pp_ref, pn_ref, gate_ref, pw_ref, ps_ref, wo_ref,
                 *rest, n_a):
    route_in, o_ref = rest[:N_ROUTE_IN], rest[N_ROUTE_IN]
    route_out, (pext, carry) = rest[N_ROUTE_IN + 1:N_ROUTE_IN + 1 + N_ROUTE_OUT], rest[-2:]
    i = pl.program_id(0)
    tm = xa_ref.shape[0]
    first = tp[i] == 0
    last = tp[i] + tm == tl[i]
    pext[0:POOL_HALO] = jnp.where(first, 0.0, pp_ref[...])
    pext[POOL_HALO:POOL_HALO + tm] = pc_ref[...]
    pext[POOL_HALO + tm:] = jnp.where(last, 0.0, pn_ref[...])
    n_g = len(POOL_WINDOWS)
    gw = pc_ref.shape[1] // n_g
    pos = tp[i] + lax.broadcasted_iota(jnp.int32, (tm, gw), 0)
    seq_len = tl[i]
    ms = []
    for g, w in enumerate(POOL_WINDOWS):
        cols = slice(g * gw, (g + 1) * gw)
        acc = pext[POOL_HALO - w // 2:POOL_HALO - w // 2 + tm, cols]
        for off in range(-w // 2 + 1, w // 2):
            acc = acc + pext[POOL_HALO + off:POOL_HALO + off + tm, cols]
        cnt = (jnp.minimum(pos + w // 2, seq_len) - jnp.maximum(pos - w // 2, 0)).astype(F32)
        dlt = acc / cnt - pc_ref[:, cols]
        ms.append((_dot(dlt.astype(BF16), pw_ref[g]) * ps_ref[:, cols]).astype(BF16))
    m = jnp.concatenate(ms, axis=1)
    w_a = a_ref.shape[1]
    mix = _dot(a_ref[...], wo_ref[0:w_a, :]) + _dot(m, wo_ref[w_a:, :])
    x = jnp.where(i < n_a, xa_ref[...], xb_ref[...])
    x_new = x + gate_ref[...] * mix
    o_ref[...] = x_new
    _route_tile(x_new, *route_in, *route_out, carry)


def _mix0(x, a, p, meta, mod, pool_w_bf, pool_scale, w_out_bf, route, tm):
    xa, xb, n_a = _as_pair(x, tm)
    d = xa.shape[1]
    t, pw = p.shape
    n_tiles = t // tm
    r = tm // POOL_HALO
    n_hb = t // POOL_HALO
    prev = pl.BlockSpec((POOL_HALO, pw), lambda i, *_: (jnp.maximum(i * r - 1, 0), 0))
    nxt = pl.BlockSpec((POOL_HALO, pw), lambda i, *_: (jnp.minimum((i + 1) * r, n_hb - 1), 0))
    r_in, r_out, r_shapes, r_scratch = _route_specs(tm, t, d, seq_arg=2)
    grid_spec = pltpu.PrefetchScalarGridSpec(
        num_scalar_prefetch=3,
        grid=(n_tiles,),
        in_specs=_split_rows(tm, d, n_a) + [
            _rows(tm, a.shape[1]), _rows(tm, pw), prev, nxt, _mod_spec(2, d, seq_arg=2),
            _resident(pool_w_bf.shape), _resident((1, pw)), _resident(w_out_bf.shape)] + r_in,
        out_specs=[_rows(tm, d)] + r_out,
        scratch_shapes=[pltpu.VMEM((tm + 2 * POOL_HALO, pw), F32), r_scratch],
    )
    return pl.pallas_call(
        functools.partial(_mix0_kernel, n_a=n_a),
        grid_spec=grid_spec,
        out_shape=[jax.ShapeDtypeStruct((t, d), F32)] + r_shapes,
        compiler_params=_params(("arbitrary",)),
        name="pool_outproj_route",
    )(meta["pos"], meta["len"], meta["seq"], xa, xb, a, p, p, p, mod, pool_w_bf,
      pool_scale.reshape(1, pw), w_out_bf, *_route_operands(route, mod, d))


def _gelu(z):
    return 0.5 * z * (1.0 + lax.erf(z * np.float32(np.sqrt(0.5))))


def _gmlp_in_kernel(ts, x_ref, g_ref, sc_ref, sh_ref, w_ref, lg_ref, lb_ref, u_ref, v_ref):
    h = _norm_mod(x_ref[...], g_ref[...], sc_ref[...], sh_ref[...]).astype(BF16)
    half = u_ref.shape[1]
    u_ref[...] = _gelu(_dot(h, w_ref[:, 0:half]))
    zv = _gelu(_dot(h, w_ref[:, half:]))
    zc = zv - jnp.mean(zv, axis=-1, keepdims=True)
    r = lax.rsqrt(jnp.mean(zc * zc, axis=-1, keepdims=True) + EPS)
    v_ref[...] = (zc * r * lg_ref[...] + lb_ref[...]).astype(BF16)


def _gmlp_in(x, meta, mod, norm_g, w_in_bf, ln_g, ln_b, tm):
    t, d = x.shape
    half = w_in_bf.shape[1] // 2
    grid_spec = pltpu.PrefetchScalarGridSpec(
        num_scalar_prefetch=1,
        grid=(t // tm,),
        in_specs=[_rows(tm, d), _resident((1, d)), _mod_spec(1, d), _mod_spec(0, d),
                  _resident(w_in_bf.shape), _resident((1, half)), _resident((1, half))],
        out_specs=[_rows(tm, half), _rows(tm, half)],
    )
    return pl.pallas_call(
        _gmlp_in_kernel,
        grid_spec=grid_spec,
        out_shape=[jax.ShapeDtypeStruct((t, half), F32), jax.ShapeDtypeStruct((t, half), BF16)],
        compiler_params=_params(("arbitrary",)),
        name="gmlp_in",
    )(meta["seq"], x, norm_g.reshape(1, d), mod, mod, w_in_bf, ln_g.reshape(1, half), ln_b.reshape(1, half))


def _gmlp_out_kernel(ts, x_ref, u_ref, v_ref, gate_ref, sw_ref, sb_ref, wo_ref, *rest):
    route_in, o_ref = rest[:N_ROUTE_IN], rest[N_ROUTE_IN]
    route_out, (gated, carry) = rest[N_ROUTE_IN + 1:N_ROUTE_IN + 1 + N_ROUTE_OUT], rest[-2:]
    tm = x_ref.shape[0]
    hd = u_ref.shape[1] // SGU_HEADS
    for c in range(tm // CHUNK):
        rows = slice(c * CHUNK, (c + 1) * CHUNK)
        for hh in range(SGU_HEADS):
            cols = slice(hh * hd, (hh + 1) * hd)
            s = _dot(sw_ref[hh], v_ref[rows, cols]) + jnp.tile(sb_ref[hh], (1, hd // LANES))
            gated[rows, cols] = (u_ref[rows, cols] * s).astype(BF16)
    x_new = x_ref[...] + gate_ref[...] * _dot(gated[...], wo_ref[...])
    o_ref[...] = x_new
    _route_tile(x_new, *route_in, *route_out, carry)


def _gmlp_out(x, u, v, meta, mod, sgu_w_bf, sgu_b_b, w_out_bf, route, tm):
    t, d = x.shape
    w = u.shape[1]
    r_in, r_out, r_shapes, r_scratch = _route_specs(tm, t, d, seq_arg=0)
    grid_spec = pltpu.PrefetchScalarGridSpec(
        num_scalar_prefetch=1,
        grid=(t // tm,),
        in_specs=[_rows(tm, d), _rows(tm, w), _rows(tm, w), _mod_spec(2, d),
                  _resident(sgu_w_bf.shape), _resident(sgu_b_b.shape), _resident(w_out_bf.shape)] + r_in,
        out_specs=[_rows(tm, d)] + r_out,
        scratch_shapes=[pltpu.VMEM((tm, w), BF16), r_scratch],
    )
    return pl.pallas_call(
        _gmlp_out_kernel,
        grid_spec=grid_spec,
        out_shape=[jax.ShapeDtypeStruct((t, d), F32)] + r_shapes,
        compiler_params=_params(("arbitrary",)),
        name="gmlp_out_route",
    )(meta["seq"], x, u, v, mod, sgu_w_bf, sgu_b_b, w_out_bf, *_route_operands(route, mod, d))


ROUTER_ROWS = 48


def _route_tile(x, g_ref, sc_ref, sh_ref, wr_ref, br_ref, h_ref, ri_ref, rw_ref, cnt_ref, carry):
    i = pl.program_id(0)

    @pl.when(i == 0)
    def _():
        carry[...] = jnp.zeros_like(carry)

    tm = x.shape[0]
    hb = _norm_mod(x, g_ref[...], sc_ref[...], sh_ref[...]).astype(BF16)
    h_ref[...] = _pack_bf16_pair(hb)
    lg = _dot_nt(wr_ref[...], hb) + jnp.tile(br_ref[...], (1, tm // LANES))
    rows8 = lax.broadcasted_iota(jnp.int32, (EXP_PER_GROUP, tm), 0)

    def first_argmax(vals, vmax):
        return jnp.min(jnp.where(vals == vmax, rows8, EXP_PER_GROUP), axis=0, keepdims=True)

    gl = lg[0:8]
    gmax = jnp.max(gl, axis=0, keepdims=True)
    gidx = first_argmax(gl, gmax)
    g_w = 1.0 / jnp.sum(jnp.exp(gl - gmax), axis=0, keepdims=True)
    esel = jnp.zeros((EXP_PER_GROUP, tm), F32)
    for g in range(N_GROUPS):
        esel = jnp.where(gidx == g, lg[8 + g * EXP_PER_GROUP:8 + (g + 1) * EXP_PER_GROUP], esel)
    emax = jnp.max(esel, axis=0, keepdims=True)
    pe = jnp.exp(esel - emax)
    prob = pe / jnp.sum(pe, axis=0, keepdims=True)
    p1 = jnp.max(prob, axis=0, keepdims=True)
    i1 = first_argmax(prob, p1)
    rest = jnp.where(rows8 == i1, -1.0, prob)
    p2 = jnp.max(rest, axis=0, keepdims=True)
    i2 = first_argmax(rest, p2)
    den = p1 + p2
    w0 = g_w * (p1 / den)
    w1 = g_w * (p2 / den)
    e0 = gidx * EXP_PER_GROUP + i1
    e1 = gidx * EXP_PER_GROUP + i2

    rows_e = lax.broadcasted_iota(jnp.int32, (N_EXPERTS, tm), 0)
    oh0 = rows_e == e0
    oh1 = rows_e == e1
    both = jnp.where(oh0, 1.0, 0.0) + jnp.where(oh1, 1.0, 0.0)
    ri = lax.broadcasted_iota(jnp.int32, (tm, tm), 0)
    ci = lax.broadcasted_iota(jnp.int32, (tm, tm), 1)
    upper = jnp.where(ri < ci, 1.0, 0.0).astype(BF16)
    before = _dot(both.astype(BF16), upper) + carry[:, 0:1]
    r0 = jnp.sum(jnp.where(oh0, before, 0.0), axis=0, keepdims=True)
    r1 = jnp.sum(jnp.where(oh1, before, 0.0), axis=0, keepdims=True)
    new_cnt = carry[...] + jnp.sum(both, axis=1, keepdims=True)
    carry[...] = new_cnt
    cnt_ref[...] = new_cnt
    zi = jnp.zeros((4, tm), jnp.int32)
    ri_ref[...] = jnp.concatenate([e0, e1, r0.astype(jnp.int32), r1.astype(jnp.int32), zi], axis=0)
    rw_ref[...] = jnp.concatenate([w0, w1, jnp.zeros((6, tm), F32)], axis=0)


N_ROUTE_IN, N_ROUTE_OUT = 5, 4


def _route_specs(tm, t, d, seq_arg):
    in_specs = [_resident((1, d)), _mod_spec(4, d, seq_arg), _mod_spec(3, d, seq_arg),
                _resident((ROUTER_ROWS, d)), _resident((ROUTER_ROWS, LANES))]
    out_specs = [_rows(tm, d // 2),
                 pl.BlockSpec((8, tm), lambda i, *_: (0, i)),
                 pl.BlockSpec((8, tm), lambda i, *_: (0, i)),
                 pl.BlockSpec((N_EXPERTS, LANES), lambda i, *_: (0, 0))]
    out_shapes = [jax.ShapeDtypeStruct((t, d // 2), jnp.uint32),
                  jax.ShapeDtypeStruct((8, t), jnp.int32),
                  jax.ShapeDtypeStruct((8, t), F32),
                  jax.ShapeDtypeStruct((N_EXPERTS, LANES), F32)]
    return in_specs, out_specs, out_shapes, pltpu.VMEM((N_EXPERTS, LANES), F32)


def _route_operands(route, mod, d):
    norm_g, wr_bf, br_b = route
    return [norm_g.reshape(1, d), mod, mod, wr_bf, br_b]


SUBLANES = 8


def _row_copy(src, s_grp, s_sub, dst, d_grp, d_sub, sem):
    return pltpu.make_async_copy(src.at[s_grp, pl.ds(s_sub, 1)], dst.at[d_grp, pl.ds(d_sub, 1)], sem)


def _split_row_index(idx):
    return jnp.concatenate([idx >> 3, idx & 7], axis=-1)


def _dispatch_kernel(pad_lo, pad_hi, slot_ref, hp_ref, xs_hbm, zrow, sem, zsem):
    i = pl.program_id(0)
    ch = slot_ref.shape[2] // 4
    n_grp = ch // SUBLANES

    @pl.when(i == 0)
    def _():
        zrow[...] = jnp.zeros_like(zrow)

        def per_range(e, c):
            lo, hi = pad_lo[e], pad_hi[e]
            lo8 = jnp.minimum(((lo + SUBLANES - 1) >> 3) << 3, hi)

            def row(r):
                return _row_copy(zrow, 0, 0, xs_hbm, r >> 3, r & 7, zsem)

            def grp(g):
                return pltpu.make_async_copy(zrow, xs_hbm.at[pl.ds(g, 1)], zsem)

            lax.fori_loop(lo, lo8, lambda r, c2: (row(r).start(), c2)[1], 0)
            lax.fori_loop(lo8 >> 3, hi >> 3, lambda g, c2: (grp(g).start(), c2)[1], 0)
            lax.fori_loop(lo, lo8, lambda r, c2: (row(r).wait(), c2)[1], 0)
            lax.fori_loop(lo8 >> 3, hi >> 3, lambda g, c2: (grp(g).wait(), c2)[1], 0)
            return c

        lax.fori_loop(0, pad_lo.shape[0], per_range, 0)

    def issue(q, c):
        for u in range(SUBLANES):
            r = q * SUBLANES + u
            for k in range(2):
                _row_copy(hp_ref, q, u, xs_hbm, slot_ref[0, 0, k * ch + r], slot_ref[0, 0, (2 + k) * ch + r],
                          sem).start()
        return c

    lax.fori_loop(0, n_grp, issue, 0)
    for k in range(2):
        pltpu.make_async_copy(hp_ref, xs_hbm.at[pl.ds(0, n_grp)], sem).wait()


def _dispatch(hp, slots, pad_lo, pad_hi, n_rows, ch):
    t, w = hp.shape
    grid_spec = pltpu.PrefetchScalarGridSpec(
        num_scalar_prefetch=2,
        grid=(t // ch,),
        in_specs=[
            pl.BlockSpec((1, 1, 4 * ch), lambda i, *_: (i, 0, 0), memory_space=pltpu.SMEM),
            pl.BlockSpec((ch // SUBLANES, SUBLANES, w), lambda i, *_: (i, 0, 0)),
        ],
        out_specs=pl.BlockSpec(memory_space=pl.ANY),
        scratch_shapes=[pltpu.VMEM((1, SUBLANES, w), hp.dtype), pltpu.SemaphoreType.DMA,
                        pltpu.SemaphoreType.DMA],
    )
    xs = pl.pallas_call(
        _dispatch_kernel,
        grid_spec=grid_spec,
        out_shape=jax.ShapeDtypeStruct((n_rows // SUBLANES, SUBLANES, w), hp.dtype),
        compiler_params=_params(("arbitrary",)),
        name="moe_dispatch",
    )(pad_lo, pad_hi, _split_row_index(slots), hp.reshape(t // SUBLANES, SUBLANES, w))
    return xs.reshape(n_rows, w)


FFN_ROW_STEPS = 4


def _pack_bf16_pair(x):
    half = x.shape[1] // 2
    bits = pltpu.bitcast(x.astype(BF16).astype(F32), jnp.uint32)
    return (bits[:, :half] >> 16) | bits[:, half:]


def _unpack_bf16_pair(xp):
    return pltpu.bitcast(xp << 16, F32), pltpu.bitcast(xp & jnp.uint32(0xFFFF0000), F32)


def _ffn_kernel(be, nv, jm, xs_ref, wg_ref, wu_ref, wd_ref, o_ref, xb, acc, *, n_ff):
    i = pl.program_id(0)
    j = pl.program_id(1)
    rb = o_ref.shape[0]
    step = rb // FFN_ROW_STEPS
    n = nv[i]

    @pl.when((j == 0) & (n > 0))
    def _():
        lo, hi = _unpack_bf16_pair(xs_ref[...])
        half = lo.shape[1]
        xb[:, 0:half] = lo.astype(BF16)
        xb[:, half:] = hi.astype(BF16)

    def ffn_rows(rows):
        x = xb[0:rows]
        hg = _dot(x, wg_ref[...].astype(BF16))
        hu = _dot(x, wu_ref[...].astype(BF16))
        act = (hg * (1.0 / (1.0 + jnp.exp(-hg))) * hu).astype(BF16)
        part = _dot(act, wd_ref[...].astype(BF16))

        def finish(total):
            o_ref[0:rows] = _pack_bf16_pair(total)
            if rows < rb:
                o_ref[rows:] = jnp.zeros((rb - rows, o_ref.shape[1]), o_ref.dtype)

        if n_ff == 1:
            finish(part)
            return

        @pl.when(j == 0)
        def _():
            acc[0:rows] = part

        @pl.when((j > 0) & (j < n_ff - 1))
        def _():
            acc[0:rows] += part

        @pl.when(j == n_ff - 1)
        def _():
            finish(acc[0:rows] + part)

    @pl.when((n == 0) & (j == 0))
    def _():
        o_ref[...] = jnp.zeros_like(o_ref)

    for k in range(1, FFN_ROW_STEPS + 1):
        pl.when((n > (k - 1) * step) & (n <= k * step))(functools.partial(ffn_rows, k * step))


FF_FORWARD, FF_BACKWARD, FF_HOLD_FIRST, FF_HOLD_LAST = 0, 1, 2, 3


def _expert_ffn(xs, blk_expert, blk_valid, blk_ff_order, layer, wg, wu, wd, rb, ft):
    n_rows, w = xs.shape
    d = 2 * w
    n_blocks = n_rows // rb
    ff = wg.shape[3]
    n_ff = ff // ft

    def ffj(i, j, jm):
        m = jm[i]
        return jnp.where(m == FF_FORWARD, j,
                         jnp.where(m == FF_BACKWARD, n_ff - 1 - j, jnp.where(m == FF_HOLD_FIRST, 0, n_ff - 1)))

    grid_spec = pltpu.PrefetchScalarGridSpec(
        num_scalar_prefetch=3,
        grid=(n_blocks, n_ff),
        in_specs=[
            pl.BlockSpec((rb, w), lambda i, j, be, nv, jm: (i, 0)),
            pl.BlockSpec((None, None, d, ft), lambda i, j, be, nv, jm: (layer, be[i], 0, ffj(i, j, jm))),
            pl.BlockSpec((None, None, d, ft), lambda i, j, be, nv, jm: (layer, be[i], 0, ffj(i, j, jm))),
            pl.BlockSpec((None, None, ft, d), lambda i, j, be, nv, jm: (layer, be[i], ffj(i, j, jm), 0)),
        ],
        out_specs=pl.BlockSpec((rb, w), lambda i, j, be, nv, jm: (i, 0)),
        scratch_shapes=[pltpu.VMEM((rb, d), BF16), pltpu.VMEM((rb, d), F32)],
    )
    return pl.pallas_call(
        functools.partial(_ffn_kernel, n_ff=n_ff),
        grid_spec=grid_spec,
        out_shape=jax.ShapeDtypeStruct((n_rows, w), jnp.uint32),
        compiler_params=_params(("arbitrary", "arbitrary")),
        name="expert_ffn",
    )(blk_expert, blk_valid, blk_ff_order, xs, wg, wu, wd)


def _combine_kernel(ts, idx_ref, nxt_ref, x_ref, wt_ref, gate_ref, ys_hbm, *rest, n_a):
    outs, (buf, sem) = rest[:-2], rest[-2:]
    i = pl.program_id(0)
    tm = x_ref.shape[0]
    cur = i % 2

    n_grp = 2 * tm // SUBLANES

    def fetch(idx, b):
        def issue(q, c):
            for u in range(SUBLANES):
                r = q * SUBLANES + u
                _row_copy(ys_hbm, idx[0, 0, r], idx[0, 0, 2 * tm + r], buf.at[b], q, u, sem.at[b]).start()
            return c

        lax.fori_loop(0, n_grp, issue, 0)

    @pl.when(i == 0)
    def _():
        fetch(idx_ref, 0)

    @pl.when(i + 1 < pl.num_programs(0))
    def _():
        fetch(nxt_ref, 1 - cur)

    pltpu.make_async_copy(buf.at[cur], buf.at[cur], sem.at[cur]).wait()
    w = buf.shape[-1]
    y0 = _unpack_bf16_pair(buf[cur, 0:n_grp // 2].reshape(tm, w))
    y1 = _unpack_bf16_pair(buf[cur, n_grp // 2:n_grp].reshape(tm, w))
    w0, w1 = wt_ref[:, 0:1], wt_ref[:, 1:2]
    y = jnp.concatenate([w0 * y0[0] + w1 * y1[0], w0 * y0[1] + w1 * y1[1]], axis=1)
    res = x_ref[...] + gate_ref[...] * y
    if n_a is None:
        outs[0][...] = res
    else:
        @pl.when(i < n_a)
        def _():
            outs[0][...] = res

        @pl.when(i >= n_a)
        def _():
            outs[1][...] = res


def _combine(x, ys, slots, wt, meta, mod, tm, split_rows=None):
    t, d = x.shape
    n_tiles = t // tm
    if split_rows is None:
        n_a, out_specs = None, _rows(tm, d)
        out_shape = jax.ShapeDtypeStruct((t, d), F32)
    else:
        n_a, out_specs = split_rows // tm, _split_rows(tm, d, split_rows // tm)
        out_shape = [jax.ShapeDtypeStruct((split_rows, d), F32), jax.ShapeDtypeStruct((t - split_rows, d), F32)]
    gate_spec = pl.BlockSpec((None, None, 1, d), lambda i, ts: (5, ts[i], 0, 0))
    grid_spec = pltpu.PrefetchScalarGridSpec(
        num_scalar_prefetch=1,
        grid=(n_tiles,),
        in_specs=[
            pl.BlockSpec((1, 1, 4 * tm), lambda i, ts: (i, 0, 0), memory_space=pltpu.SMEM),
            pl.BlockSpec((1, 1, 4 * tm), lambda i, ts: (jnp.minimum(i + 1, n_tiles - 1), 0, 0),
                         memory_space=pltpu.SMEM),
            _rows(tm, d),
            _rows(tm, wt.shape[1]),
            gate_spec,
            pl.BlockSpec(memory_space=pl.ANY),
        ],
        out_specs=out_specs,
        scratch_shapes=[pltpu.VMEM((2, 2 * tm // SUBLANES, SUBLANES, ys.shape[1]), ys.dtype),
                        pltpu.SemaphoreType.DMA((2,))],
    )
    slots = _split_row_index(slots)
    ys = ys.reshape(ys.shape[0] // SUBLANES, SUBLANES, ys.shape[1])
    return pl.pallas_call(
        functools.partial(_combine_kernel, n_a=n_a),
        grid_spec=grid_spec,
        out_shape=out_shape,
        compiler_params=_params(("arbitrary",)),
        name="moe_combine",
    )(meta["seq"], slots, slots, x, wt, mod, ys)


def _tile_meta(seq_lens, tm):
    seq, pos, ln = [], [], []
    for s, n in enumerate(seq_lens):
        assert n % tm == 0
        for k in range(n // tm):
            seq.append(s)
            pos.append(k * tm)
            ln.append(n)
    return {k: jnp.asarray(np.asarray(v, np.int32)) for k, v in (("seq", seq), ("pos", pos), ("len", ln))}


def _rope_tables(s_max):
    half = HEAD_DIM // 2
    inv = ROPE_THETA ** (-jnp.arange(half, dtype=F32) / half)
    ang = jnp.arange(s_max, dtype=F32)[:, None] * inv[None, :]
    cos, sin = jnp.cos(ang), jnp.sin(ang)
    return jnp.concatenate([cos, cos], axis=1), jnp.concatenate([-sin, sin], axis=1)


def _tile_slots(slot, tm):
    t = slot.shape[1]
    return slot.reshape(2, t // tm, tm).transpose(1, 0, 2).reshape(t // tm, 1, 2 * tm)


def _route_params(norm_g, rg_w, rg_b, re_w, re_b):
    d = rg_w.shape[0]
    wr = jnp.zeros((ROUTER_ROWS, d), F32).at[0:N_GROUPS].set(rg_w.T).at[8:8 + N_EXPERTS].set(re_w.T)
    br = jnp.zeros((ROUTER_ROWS,), F32).at[N_GROUPS:8].set(NEG_BIG).at[0:N_GROUPS].set(rg_b)
    br = br.at[8:8 + N_EXPERTS].set(re_b)
    return norm_g, wr.astype(BF16), jnp.broadcast_to(br[:, None], (ROUTER_ROWS, LANES))


def _moe(x, routed, metas, mod, layer, wg, wu, wd, cfg, split_rows=None):
    t, d = x.shape
    hp, ri, rw, cnt = routed

    rb = cfg["rb"]
    expert = ri[0:2]
    rank = ri[2:4]
    counts = cnt[:, 0].astype(jnp.int32)
    nblk = (counts + rb - 1) // rb
    blk_end = jnp.cumsum(nblk)
    row_start = (blk_end - nblk) * rb
    ids = jnp.arange(N_EXPERTS, dtype=jnp.int32)
    start_of = jnp.sum(jnp.where(expert[:, :, None] == ids, row_start, 0), axis=-1)
    slot = start_of + rank
    n_blocks = -(-2 * t // rb) + N_EXPERTS
    n_used = blk_end[-1]
    blk = jnp.arange(n_blocks, dtype=jnp.int32)
    blk_expert = jnp.minimum(jnp.sum(blk[:, None] >= blk_end[None, :], axis=1), N_EXPERTS - 1).astype(jnp.int32)
    last_used = jnp.sum(jnp.where(blk == n_used - 1, blk_expert, 0))
    blk_expert = jnp.where(blk < n_used, blk_expert, last_used)
    onehot = blk_expert[:, None] == ids
    row_end_of = jnp.sum(jnp.where(onehot, row_start + counts, 0), axis=1)
    blk_valid = jnp.where(blk < n_used, jnp.clip(row_end_of - blk * rb, 0, rb), 0).astype(jnp.int32)
    first_blk_of = jnp.sum(jnp.where(onehot, blk_end - nblk, 0), axis=1)
    order = (blk - first_blk_of) & 1
    last_order = jnp.sum(jnp.where(blk == n_used - 1, order, 0))
    hold = jnp.where(last_order == FF_FORWARD, FF_HOLD_LAST, FF_HOLD_FIRST)
    blk_ff_order = jnp.where(blk < n_used, order, hold).astype(jnp.int32)
    pad_lo = jnp.concatenate([row_start + counts, (n_used * rb).reshape(1)])
    pad_hi = jnp.concatenate([blk_end * rb, jnp.full((1,), n_blocks * rb, jnp.int32)])

    ch = cfg["ch_dispatch"]
    xs = _dispatch(hp, _tile_slots(slot, ch), pad_lo.astype(jnp.int32), pad_hi.astype(jnp.int32),
                   n_blocks * rb, ch)
    ys = _expert_ffn(xs, blk_expert, blk_valid, blk_ff_order, layer, wg, wu, wd, rb, cfg["ft"])

    tmc = cfg["tm_combine"]
    wt = rw.T
    return _combine(x, ys, _tile_slots(slot, tmc), wt, metas[tmc], mod, tmc, split_rows)


def _config(seq_lens):
    g = int(np.gcd.reduce(np.asarray(seq_lens)))
    tm = min(512, g)
    return {"tm": tm, "tm_combine": min(512, g), "tm_gmlp": min(512, g), "rb": 512, "ft": 1024,
            "ch_dispatch": min(1024, g)}


def _forward(x_prompt, x_sample, c_prompt, c_sample, ada_w, ada_b, norm_mix_g, norm_ffn_g,
             ab_w_in, q_norm_g, k_norm_g, attn_sink, pool_w, pool_scale, ab_w_out,
             c_w_in, sgu_ln_g, sgu_ln_b, sgu_w, sgu_b, c_w_out,
             router_group_w, router_group_b, router_expert_w, router_expert_b,
             exp_w_gate, exp_w_up, exp_w_down, cfg=None):
    bp, sp, d = x_prompt.shape
    bs, ss, _ = x_sample.shape
    seq_lens = [sp] * bp + [ss] * bs
    n_seq = len(seq_lens)
    if cfg is None:
        cfg = _config(seq_lens)
    depth = ada_w.shape[0]
    x = (x_prompt.reshape(bp * sp, d), x_sample.reshape(bs * ss, d))
    metas ={tm: _tile_meta(seq_lens, tm) for tm in
             {cfg["tm"], cfg["tm_combine"], cfg["tm_gmlp"]}}

    n_pad = -(-n_seq // 8) * 8
    c_pad = jnp.zeros((n_pad, d), F32).at[0:n_seq].set(jnp.concatenate([c_prompt, c_sample], axis=0))
    mod_all = _ada_mod(c_pad, ada_w, ada_b)
    cos_t, sin_t = _rope_tables(max(seq_lens))
    exp_w_gate, exp_w_up, exp_w_down = (w.astype(BF16) for w in (exp_w_gate, exp_w_up, exp_w_down))

    for l in range(depth):
        mod = mod_all[l, 0:n_seq].reshape(n_seq, 6, d).transpose(1, 0, 2).reshape(6, n_seq, 1, d)
        i = l // 2
        route = _route_params(norm_ffn_g[l], router_group_w[l], router_group_b[l],
                              router_expert_w[l], router_expert_b[l])
        if l % 2 == 0:
            tm = cfg["tm"]
            q, k, v, p = _inproj(x, metas[tm], mod, norm_mix_g[l], ab_w_in[i].astype(BF16),
                                 q_norm_g[i], k_norm_g[i], cos_t, sin_t, tm)
            sink_b = jnp.broadcast_to(
                jnp.repeat(attn_sink[i].reshape(N_KV_HEADS, GQA_GROUP), ATT_BLOCK, axis=1)[:, :, None],
                (N_KV_HEADS, GQA_GROUP * ATT_BLOCK, LANES)).astype(F32)
            a = _attention(q, k, v, sink_b, metas[tm], tm)
            x, *routed = _mix0(x, a, p, metas[tm], mod, pool_w[i].astype(BF16), pool_scale[i],
                               ab_w_out[i].astype(BF16), route, tm)
        else:
            tm = cfg["tm_gmlp"]
            if isinstance(x, tuple):
                x = jnp.concatenate(x, axis=0)
            u, v = _gmlp_in(x,metas[tm], mod, norm_mix_g[l], c_w_in[i].astype(BF16),
                            sgu_ln_g[i], sgu_ln_b[i], tm)
            sgu_b_b = jnp.broadcast_to(sgu_b[i][:, :, None], (SGU_HEADS, CHUNK, LANES)).astype(F32)
            x, *routed = _gmlp_out(x, u, v, metas[tm], mod, sgu_w[i].astype(BF16), sgu_b_b,
                                   c_w_out[i].astype(BF16), route, tm)
        x = _moe(x, routed, metas, mod, l, exp_w_gate, exp_w_up, exp_w_down, cfg,
                 split_rows=bp * sp if l == depth - 1 else None)

    y_prompt, y_sample = x
    return (y_prompt.reshape(bp, sp, d), y_sample.reshape(bs, ss, d))


def kernel(x_prompt, x_sample, c_prompt, c_sample, ada_w, ada_b, norm_mix_g, norm_ffn_g, ab_w_in, q_norm_g,
           k_norm_g, attn_sink, pool_w, pool_scale, ab_w_out, c_w_in, sgu_ln_g, sgu_ln_b, sgu_w, sgu_b,
           c_w_out, router_group_w, router_group_b, router_expert_w, router_expert_b, exp_w_gate, exp_w_up,
           exp_w_down):
    return _forward(x_prompt, x_sample, c_prompt, c_sample, ada_w, ada_b, norm_mix_g, norm_ffn_g, ab_w_in,
                    q_norm_g, k_norm_g, attn_sink, pool_w, pool_scale, ab_w_out, c_w_in, sgu_ln_g, sgu_ln_b,
                    sgu_w, sgu_b, c_w_out, router_group_w, router_group_b, router_expert_w, router_expert_b,
                    exp_w_gate, exp_w_up, exp_w_down)
```

```python
import functools

import numpy as np
import jax
import jax.numpy as jnp
from jax import lax
from jax.experimental import pallas as pl
from jax.experimental.pallas import tpu as pltpu

HEAD_DIM = 128
N_HEADS = 8
N_KV_HEADS = 2
GQA_GROUP = N_HEADS // N_KV_HEADS
ATT_BLOCK = 128
ROPE_THETA = 10000.0
Q_W = N_HEADS * HEAD_DIM
KV_W = N_KV_HEADS * HEAD_DIM
POOL_WINDOWS = (2, 4, 8, 16)
POOL_HALO = 8
CHUNK = 128
SGU_HEADS = 8
N_GROUPS = 4
EXP_PER_GROUP = 8
N_EXPERTS = N_GROUPS * EXP_PER_GROUP
EPS = 1e-6

VMEM_LIMIT_BYTES = 56 * 1024 * 1024
LANES = 128

F32 = jnp.float32
BF16 = jnp.bfloat16
NEG_BIG = -1e30


def _params(sem):
    return pltpu.CompilerParams(dimension_semantics=sem, vmem_limit_bytes=VMEM_LIMIT_BYTES)


def _resident(shape):
    nd = len(shape)
    return pl.BlockSpec(shape, lambda *_: (0,) * nd, pipeline_mode=pl.Buffered(1))


def _rows(tm, width):
    return pl.BlockSpec((tm, width), lambda i, *_: (i, 0))


def _split_rows(tm, width, n_a):
    return [pl.BlockSpec((tm, width), lambda i, *_: (jnp.minimum(i, n_a - 1), 0)),
            pl.BlockSpec((tm, width), lambda i, *_: (jnp.maximum(i - n_a, 0), 0))]


def _as_pair(x, tm):
    if isinstance(x, tuple):
        return x[0], x[1], x[0].shape[0] // tm
    return x, x, x.shape[0] // tm


def _mod_spec(part, d, seq_arg=0):
    return pl.BlockSpec((None, None, 1, d), lambda i, *pf: (part, pf[seq_arg][i], 0, 0))


def _norm_mod(x, g, sc, sh):
    r = lax.rsqrt(jnp.mean(x * x, axis=-1, keepdims=True) + EPS)
    return x * r * g * (1.0 + sc) + sh


def _dot(a, b):
    return jnp.dot(a, b, preferred_element_type=F32)


def _dot_nt(a, b):
    return lax.dot_general(a, b, (((1,), (1,)), ((), ())), preferred_element_type=F32)


def _ada_kernel(c_ref, w_ref, b_ref, o_ref):
    c = c_ref[...]
    cs = c * (1.0 / (1.0 + jnp.exp(-c)))
    o_ref[...] = _dot(cs.astype(BF16), w_ref[...].astype(BF16)) + b_ref[...]


def _ada_mod(c_pad, ada_w, ada_b):
    depth, d, n = ada_w.shape
    tn = 1024
    return pl.pallas_call(
        _ada_kernel,
        grid=(depth, n // tn),
        in_specs=[
            pl.BlockSpec(c_pad.shape, lambda l, j: (0, 0)),
            pl.BlockSpec((None, d, tn), lambda l, j: (l, 0, j)),
            pl.BlockSpec((None, 1, tn), lambda l, j: (l, 0, j)),
        ],
        out_specs=pl.BlockSpec((None, c_pad.shape[0], tn), lambda l, j: (l, 0, j)),
        out_shape=jax.ShapeDtypeStruct((depth, c_pad.shape[0], n), F32),
        compiler_params=_params(("arbitrary", "arbitrary")),
        name="ada_mod",
    )(c_pad, ada_w, ada_b.reshape(depth, 1, n))


def _inproj_kernel(ts, tp, xa_ref, xb_ref, g_ref, sc_ref, sh_ref, w_ref, qg_ref, kg_ref, cos_ref, sin_ref,
                   q_ref, k_ref, v_ref, p_ref, *, n_a):
    x = jnp.where(pl.program_id(0) < n_a, xa_ref[...], xb_ref[...])
    h = _norm_mod(x, g_ref[...], sc_ref[...], sh_ref[...]).astype(BF16)
    cos = cos_ref[...]
    sin = sin_ref[...]

    def head_norm_rope(y, gain):
        r = lax.rsqrt(jnp.mean(y * y, axis=-1, keepdims=True) + EPS)
        y = y * r * gain
        return y * cos + pltpu.roll(y, HEAD_DIM // 2, 1) * sin

    q = _dot(h, w_ref[:, 0:Q_W])
    for hh in range(N_HEADS):
        sl = slice(hh * HEAD_DIM, (hh + 1) * HEAD_DIM)
        q_ref[:, sl] = head_norm_rope(q[:, sl], qg_ref[...]).astype(BF16)
    kv = _dot(h, w_ref[:, Q_W:Q_W + 2 * KV_W])
    for hh in range(N_KV_HEADS):
        sl = slice(hh * HEAD_DIM, (hh + 1) * HEAD_DIM)
        k_ref[:, sl] = head_norm_rope(kv[:, sl], kg_ref[...]).astype(BF16)
    v_ref[...] = kv[:, KV_W:].astype(BF16)
    p_ref[...] = _dot(h, w_ref[:, Q_W + 2 * KV_W:])


def _inproj(x, meta, mod, norm_g, w_in_bf, q_g, k_g, cos_t, sin_t, tm):
    xa, xb, n_a = _as_pair(x, tm)
    d = xa.shape[1]
    n_tiles = meta["seq"].shape[0]
    t = n_tiles * tm
    pool_w = w_in_bf.shape[1] - Q_W - 2 * KV_W
    rope_spec = pl.BlockSpec((tm, HEAD_DIM), lambda i, ts, tp: (tp[i] // tm, 0))
    grid_spec = pltpu.PrefetchScalarGridSpec(
        num_scalar_prefetch=2,
        grid=(n_tiles,),
        in_specs=_split_rows(tm, d, n_a) + [
            _resident((1, d)),
            _mod_spec(1, d),
            _mod_spec(0, d),
            _resident(w_in_bf.shape),
            _resident((1, HEAD_DIM)),
            _resident((1, HEAD_DIM)),
            rope_spec,
            rope_spec,
        ],
        out_specs=[_rows(tm, Q_W), _rows(tm, KV_W), _rows(tm, KV_W), _rows(tm, pool_w)],
    )
    return pl.pallas_call(
        functools.partial(_inproj_kernel, n_a=n_a),
        grid_spec=grid_spec,
        out_shape=[
            jax.ShapeDtypeStruct((t, Q_W), BF16),
            jax.ShapeDtypeStruct((t, KV_W), BF16),
            jax.ShapeDtypeStruct((t, KV_W), BF16),
            jax.ShapeDtypeStruct((t, pool_w), F32),
        ],
        compiler_params=_params(("arbitrary",)),
        name="attn_pool_inproj",
    )(meta["seq"], meta["pos"], xa, xb, norm_g.reshape(1, d), mod, mod, w_in_bf,
      q_g.reshape(1, HEAD_DIM), k_g.reshape(1, HEAD_DIM), cos_t, sin_t)


def _attn_kernel(tp, tl, q_ref, kc_ref, kp_ref, kn_ref, vc_ref, vp_ref, vn_ref, sink_ref, o_ref,
                 kx, vx, s_scr, p_scr, r_scr):
    i = pl.program_id(0)
    tm = q_ref.shape[0]
    nb = tm // ATT_BLOCK
    first = tp[i] == 0
    last = tp[i] + tm == tl[i]
    kx[0:ATT_BLOCK] = kp_ref[...]
    kx[ATT_BLOCK:ATT_BLOCK + tm] = kc_ref[...]
    kx[ATT_BLOCK + tm:] = kn_ref[...]
    vx[0:ATT_BLOCK] = vp_ref[...]
    vx[ATT_BLOCK:ATT_BLOCK + tm] = vc_ref[...]
    vx[ATT_BLOCK + tm:] = vn_ref[...]
    win = 3 * ATT_BLOCK
    qi = lax.broadcasted_iota(jnp.int32, (ATT_BLOCK, win), 0)
    kj = lax.broadcasted_iota(jnp.int32, (ATT_BLOCK, win), 1)
    band = (kj >= qi) & (kj <= qi + 2 * ATT_BLOCK)
    scale = HEAD_DIM ** -0.5
    units = [(b, kk) for b in range(nb) for kk in range(N_KV_HEADS)]

    def heads_of(kk):
        return [kk * GQA_GROUP + g for g in range(GQA_GROUP)]

    for n, (b, kk) in enumerate(units):
        rows = slice(b * ATT_BLOCK, (b + 1) * ATT_BLOCK)
        kw = kx[b * ATT_BLOCK:b * ATT_BLOCK + win, kk * HEAD_DIM:(kk + 1) * HEAD_DIM]
        qs = jnp.concatenate([q_ref[rows, hd * HEAD_DIM:(hd + 1) * HEAD_DIM] for hd in heads_of(kk)], axis=0)
        s_scr[n] = _dot_nt(qs, kw)
    for n, (b, kk) in enumerate(units):
        valid = band
        if b == 0:
            valid = valid & (kj >= jnp.where(first, ATT_BLOCK, 0))
        if b == nb - 1:
            valid = valid & (kj < jnp.where(last, 2 * ATT_BLOCK, win))
        bias = jnp.where(valid, 0.0, -jnp.inf).astype(F32)
        s = s_scr[n] * scale
        s = (s.reshape(GQA_GROUP, ATT_BLOCK, win) + bias[None]).reshape(GQA_GROUP * ATT_BLOCK, win)
        sk = sink_ref[kk][:, 0:1]
        m = jnp.maximum(jnp.max(s, axis=-1, keepdims=True), sk)
        p = jnp.exp(s - m)
        denom = jnp.sum(p, axis=-1, keepdims=True) + jnp.exp(sk - m)
        p_scr[n] = p.astype(BF16)
        r_scr[n] = jnp.broadcast_to(1.0 / denom, r_scr.shape[1:])
    for n, (b, kk) in enumerate(units):
        rows = slice(b * ATT_BLOCK, (b + 1) * ATT_BLOCK)
        vw = vx[b * ATT_BLOCK:b * ATT_BLOCK + win, kk * HEAD_DIM:(kk + 1) * HEAD_DIM]
        o = _dot(p_scr[n], vw) * r_scr[n]
        for g, hd in enumerate(heads_of(kk)):
            o_ref[rows, hd * HEAD_DIM:(hd + 1) * HEAD_DIM] = o[g * ATT_BLOCK:(g + 1) * ATT_BLOCK].astype(BF16)


def _attention(q, k, v, sink_b, meta, tm):
    t = q.shape[0]
    n_tiles = t // tm
    r = tm // ATT_BLOCK
    n_blk = t // ATT_BLOCK
    cur = pl.BlockSpec((tm, KV_W), lambda i, *_: (i, 0))
    prev = pl.BlockSpec((ATT_BLOCK, KV_W), lambda i, *_: (jnp.maximum(i * r - 1, 0), 0))
    nxt = pl.BlockSpec((ATT_BLOCK, KV_W), lambda i, *_: (jnp.minimum((i + 1) * r, n_blk - 1), 0))
    grid_spec = pltpu.PrefetchScalarGridSpec(
        num_scalar_prefetch=2,
        grid=(n_tiles,),
        in_specs=[_rows(tm, Q_W), cur, prev, nxt, cur, prev, nxt, _resident(sink_b.shape)],
        out_specs=_rows(tm, Q_W),
        scratch_shapes=[pltpu.VMEM((tm + 2 * ATT_BLOCK, KV_W), BF16),
                        pltpu.VMEM((tm + 2 * ATT_BLOCK, KV_W), BF16),
                        pltpu.VMEM((r * N_KV_HEADS, GQA_GROUP * ATT_BLOCK, 3 * ATT_BLOCK), F32),
                        pltpu.VMEM((r * N_KV_HEADS, GQA_GROUP * ATT_BLOCK, 3 * ATT_BLOCK), BF16),
                        pltpu.VMEM((r * N_KV_HEADS, GQA_GROUP * ATT_BLOCK, HEAD_DIM), F32)],
    )
    return pl.pallas_call(
        _attn_kernel,
        grid_spec=grid_spec,
        out_shape=jax.ShapeDtypeStruct((t, Q_W), BF16),
        compiler_params=_params(("arbitrary",)),
        name="banded_attention",
    )(meta["pos"], meta["len"], q, k, k, k, v, v, v, sink_b)


def _mix0_kernel(tp, tl, ts, xa_ref, xb_ref, a_ref, pc_ref, pp_ref, pn_ref, gate_ref, pw_ref, ps_ref, wo_ref,
                 *rest, n_a):
    route_in, o_ref = rest[:N_ROUTE_IN], rest[N_ROUTE_IN]
    route_out, (pext, carry) = rest[N_ROUTE_IN + 1:N_ROUTE_IN + 1 + N_ROUTE_OUT], rest[-2:]
    i = pl.program_id(0)
    tm = xa_ref.shape[0]
    first = tp[i] == 0
    last = tp[i] + tm == tl[i]
    pext[0:POOL_HALO] = jnp.where(first, 0.0, pp_ref[...])
    pext[POOL_HALO:POOL_HALO + tm] = pc_ref[...]
    pext[POOL_HALO + tm:] = jnp.where(last, 0.0, pn_ref[...])
    n_g = len(POOL_WINDOWS)
    gw = pc_ref.shape[1] // n_g
    pos = tp[i] + lax.broadcasted_iota(jnp.int32, (tm, gw), 0)
    seq_len = tl[i]
    ms = []
    for g, w in enumerate(POOL_WINDOWS):
        cols = slice(g * gw, (g + 1) * gw)
        acc = pext[POOL_HALO - w // 2:POOL_HALO - w // 2 + tm, cols]
        for off in range(-w // 2 + 1, w // 2):
            acc = acc + pext[POOL_HALO + off:POOL_HALO + off + tm, cols]
        cnt = (jnp.minimum(pos + w // 2, seq_len) - jnp.maximum(pos - w // 2, 0)).astype(F32)
        dlt = acc / cnt - pc_ref[:, cols]
        ms.append((_dot(dlt.astype(BF16), pw_ref[g]) * ps_ref[:, cols]).astype(BF16))
    m = jnp.concatenate(ms, axis=1)
    w_a = a_ref.shape[1]
    mix = _dot(a_ref[...], wo_ref[0:w_a, :]) + _dot(m, wo_ref[w_a:, :])
    x = jnp.where(i < n_a, xa_ref[...], xb_ref[...])
    x_new = x + gate_ref[...] * mix
    o_ref[...] = x_new
    _route_tile(x_new, *route_in, *route_out, carry)


def _mix0(x, a, p, meta, mod, pool_w_bf, pool_scale, w_out_bf, route, tm):
    xa, xb, n_a = _as_pair(x, tm)
    d = xa.shape[1]
    t, pw = p.shape
    n_tiles = t // tm
    r = tm // POOL_HALO
    n_hb = t // POOL_HALO
    prev = pl.BlockSpec((POOL_HALO, pw), lambda i, *_: (jnp.maximum(i * r - 1, 0), 0))
    nxt = pl.BlockSpec((POOL_HALO, pw), lambda i, *_: (jnp.minimum((i + 1) * r, n_hb - 1), 0))
    r_in, r_out, r_shapes, r_scratch = _route_specs(tm, t, d, seq_arg=2)
    grid_spec = pltpu.PrefetchScalarGridSpec(
        num_scalar_prefetch=3,
        grid=(n_tiles,),
        in_specs=_split_rows(tm, d, n_a) + [
            _rows(tm, a.shape[1]), _rows(tm, pw), prev, nxt, _mod_spec(2, d, seq_arg=2),
            _resident(pool_w_bf.shape), _resident((1, pw)), _resident(w_out_bf.shape)] + r_in,
        out_specs=[_rows(tm, d)] + r_out,
        scratch_shapes=[pltpu.VMEM((tm + 2 * POOL_HALO, pw), F32), r_scratch],
    )
    return pl.pallas_call(
        functools.partial(_mix0_kernel, n_a=n_a),
        grid_spec=grid_spec,
        out_shape=[jax.ShapeDtypeStruct((t, d), F32)] + r_shapes,
        compiler_params=_params(("arbitrary",)),
        name="pool_outproj_route",
    )(meta["pos"], meta["len"], meta["seq"], xa, xb, a, p, p, p, mod, pool_w_bf,
      pool_scale.reshape(1, pw), w_out_bf, *_route_operands(route, mod, d))


def _gelu(z):
    return 0.5 * z * (1.0 + lax.erf(z * np.float32(np.sqrt(0.5))))


def _gmlp_in_kernel(ts, x_ref, g_ref, sc_ref, sh_ref, w_ref, lg_ref, lb_ref, u_ref, v_ref):
    h = _norm_mod(x_ref[...], g_ref[...], sc_ref[...], sh_ref[...]).astype(BF16)
    half = u_ref.shape[1]
    u_ref[...] = _gelu(_dot(h, w_ref[:, 0:half]))
    zv = _gelu(_dot(h, w_ref[:, half:]))
    zc = zv - jnp.mean(zv, axis=-1, keepdims=True)
    r = lax.rsqrt(jnp.mean(zc * zc, axis=-1, keepdims=True) + EPS)
    v_ref[...] = (zc * r * lg_ref[...] + lb_ref[...]).astype(BF16)


def _gmlp_in(x, meta, mod, norm_g, w_in_bf, ln_g, ln_b, tm):
    t, d = x.shape
    half = w_in_bf.shape[1] // 2
    grid_spec = pltpu.PrefetchScalarGridSpec(
        num_scalar_prefetch=1,
        grid=(t // tm,),
        in_specs=[_rows(tm, d), _resident((1, d)), _mod_spec(1, d), _mod_spec(0, d),
                  _resident(w_in_bf.shape), _resident((1, half)), _resident((1, half))],
        out_specs=[_rows(tm, half), _rows(tm, half)],
    )
    return pl.pallas_call(
        _gmlp_in_kernel,
        grid_spec=grid_spec,
        out_shape=[jax.ShapeDtypeStruct((t, half), F32), jax.ShapeDtypeStruct((t, half), BF16)],
        compiler_params=_params(("arbitrary",)),
        name="gmlp_in",
    )(meta["seq"], x, norm_g.reshape(1, d), mod, mod, w_in_bf, ln_g.reshape(1, half), ln_b.reshape(1, half))


def _gmlp_out_kernel(ts, x_ref, u_ref, v_ref, gate_ref, sw_ref, sb_ref, wo_ref, *rest):
    route_in, o_ref = rest[:N_ROUTE_IN], rest[N_ROUTE_IN]
    route_out, (gated, carry) = rest[N_ROUTE_IN + 1:N_ROUTE_IN + 1 + N_ROUTE_OUT], rest[-2:]
    tm = x_ref.shape[0]
    hd = u_ref.shape[1] // SGU_HEADS
    for c in range(tm // CHUNK):
        rows = slice(c * CHUNK, (c + 1) * CHUNK)
        for hh in range(SGU_HEADS):
            cols = slice(hh * hd, (hh + 1) * hd)
            s = _dot(sw_ref[hh], v_ref[rows, cols]) + jnp.tile(sb_ref[hh], (1, hd // LANES))
            gated[rows, cols] = (u_ref[rows, cols] * s).astype(BF16)
    x_new = x_ref[...] + gate_ref[...] * _dot(gated[...], wo_ref[...])
    o_ref[...] = x_new
    _route_tile(x_new, *route_in, *route_out, carry)


def _gmlp_out(x, u, v, meta, mod, sgu_w_bf, sgu_b_b, w_out_bf, route, tm):
    t, d = x.shape
    w = u.shape[1]
    r_in, r_out, r_shapes, r_scratch = _route_specs(tm, t, d, seq_arg=0)
    grid_spec = pltpu.PrefetchScalarGridSpec(
        num_scalar_prefetch=1,
        grid=(t // tm,),
        in_specs=[_rows(tm, d), _rows(tm, w), _rows(tm, w), _mod_spec(2, d),
                  _resident(sgu_w_bf.shape), _resident(sgu_b_b.shape), _resident(w_out_bf.shape)] + r_in,
        out_specs=[_rows(tm, d)] + r_out,
        scratch_shapes=[pltpu.VMEM((tm, w), BF16), r_scratch],
    )
    return pl.pallas_call(
        _gmlp_out_kernel,
        grid_spec=grid_spec,
        out_shape=[jax.ShapeDtypeStruct((t, d), F32)] + r_shapes,
        compiler_params=_params(("arbitrary",)),
        name="gmlp_out_route",
    )(meta["seq"], x, u, v, mod, sgu_w_bf, sgu_b_b, w_out_bf, *_route_operands(route, mod, d))


ROUTER_ROWS = 48


def _route_tile(x, g_ref, sc_ref, sh_ref, wr_ref, br_ref, h_ref, ri_ref, rw_ref, cnt_ref, carry):
    i = pl.program_id(0)

    @pl.when(i == 0)
    def _():
        carry[...] = jnp.zeros_like(carry)

    tm = x.shape[0]
    hb = _norm_mod(x, g_ref[...], sc_ref[...], sh_ref[...]).astype(BF16)
    h_ref[...] = _pack_bf16_pair(hb)
    lg = _dot_nt(wr_ref[...], hb) + jnp.tile(br_ref[...], (1, tm // LANES))
    rows8 = lax.broadcasted_iota(jnp.int32, (EXP_PER_GROUP, tm), 0)

    def first_argmax(vals, vmax):
        return jnp.min(jnp.where(vals == vmax, rows8, EXP_PER_GROUP), axis=0, keepdims=True)

    gl = lg[0:8]
    gmax = jnp.max(gl, axis=0, keepdims=True)
    gidx = first_argmax(gl, gmax)
    g_w = 1.0 / jnp.sum(jnp.exp(gl - gmax), axis=0, keepdims=True)
    esel = jnp.zeros((EXP_PER_GROUP, tm), F32)
    for g in range(N_GROUPS):
        esel = jnp.where(gidx == g, lg[8 + g * EXP_PER_GROUP:8 + (g + 1) * EXP_PER_GROUP], esel)
    emax = jnp.max(esel, axis=0, keepdims=True)
    pe = jnp.exp(esel - emax)
    prob = pe / jnp.sum(pe, axis=0, keepdims=True)
    p1 = jnp.max(prob, axis=0, keepdims=True)
    i1 = first_argmax(prob, p1)
    rest = jnp.where(rows8 == i1, -1.0, prob)
    p2 = jnp.max(rest, axis=0, keepdims=True)
    i2 = first_argmax(rest, p2)
    den = p1 + p2
    w0 = g_w * (p1 / den)
    w1 = g_w * (p2 / den)
    e0 = gidx * EXP_PER_GROUP + i1
    e1 = gidx * EXP_PER_GROUP + i2

    rows_e = lax.broadcasted_iota(jnp.int32, (N_EXPERTS, tm), 0)
    oh0 = rows_e == e0
    oh1 = rows_e == e1
    both = jnp.where(oh0, 1.0, 0.0) + jnp.where(oh1, 1.0, 0.0)
    ri = lax.broadcasted_iota(jnp.int32, (tm, tm), 0)
    ci = lax.broadcasted_iota(jnp.int32, (tm, tm), 1)
    upper = jnp.where(ri < ci, 1.0, 0.0).astype(BF16)
    before = _dot(both.astype(BF16), upper) + carry[:, 0:1]
    r0 = jnp.sum(jnp.where(oh0, before, 0.0), axis=0, keepdims=True)
    r1 = jnp.sum(jnp.where(oh1, before, 0.0), axis=0, keepdims=True)
    new_cnt = carry[...] + jnp.sum(both, axis=1, keepdims=True)
    carry[...] = new_cnt
    cnt_ref[...] = new_cnt
    zi = jnp.zeros((4, tm), jnp.int32)
    ri_ref[...] = jnp.concatenate([e0, e1, r0.astype(jnp.int32), r1.astype(jnp.int32), zi], axis=0)
    rw_ref[...] = jnp.concatenate([w0, w1, jnp.zeros((6, tm), F32)], axis=0)


N_ROUTE_IN, N_ROUTE_OUT = 5, 4


def _route_specs(tm, t, d, seq_arg):
    in_specs = [_resident((1, d)), _mod_spec(4, d, seq_arg), _mod_spec(3, d, seq_arg),
                _resident((ROUTER_ROWS, d)), _resident((ROUTER_ROWS, LANES))]
    out_specs = [_rows(tm, d // 2),
                 pl.BlockSpec((8, tm), lambda i, *_: (0, i)),
                 pl.BlockSpec((8, tm), lambda i, *_: (0, i)),
                 pl.BlockSpec((N_EXPERTS, LANES), lambda i, *_: (0, 0))]
    out_shapes = [jax.ShapeDtypeStruct((t, d // 2), jnp.uint32),
                  jax.ShapeDtypeStruct((8, t), jnp.int32),
                  jax.ShapeDtypeStruct((8, t), F32),
                  jax.ShapeDtypeStruct((N_EXPERTS, LANES), F32)]
    return in_specs, out_specs, out_shapes, pltpu.VMEM((N_EXPERTS, LANES), F32)


def _route_operands(route, mod, d):
    norm_g, wr_bf, br_b = route
    return [norm_g.reshape(1, d), mod, mod, wr_bf, br_b]


SUBLANES = 8


def _row_copy(src, s_grp, s_sub, dst, d_grp, d_sub, sem):
    return pltpu.make_async_copy(src.at[s_grp, pl.ds(s_sub, 1)], dst.at[d_grp, pl.ds(d_sub, 1)], sem)


def _split_row_index(idx):
    return jnp.concatenate([idx >> 3, idx & 7], axis=-1)


N_EXPERT_MATS = 3


def _dispatch_kernel(pad_lo, pad_hi, slot_ref, hp_ref, *rest):
    w_in, xs_hbm = rest[:N_EXPERT_MATS], rest[N_EXPERT_MATS]
    w_out, (zrow, sem, zsem) = rest[N_EXPERT_MATS + 1:2 * N_EXPERT_MATS + 1], rest[-3:]
    i = pl.program_id(0)
    ch = slot_ref.shape[2] // 4
    n_grp = ch // SUBLANES

    @pl.when(i == 0)
    def _():
        zrow[...] = jnp.zeros_like(zrow)

        def per_range(e, c):
            lo, hi = pad_lo[e], pad_hi[e]
            lo8 = jnp.minimum(((lo + SUBLANES - 1) >> 3) << 3, hi)

            def row(r):
                return _row_copy(zrow, 0, 0, xs_hbm, r >> 3, r & 7, zsem)

            def grp(g):
                return pltpu.make_async_copy(zrow, xs_hbm.at[pl.ds(g, 1)], zsem)

            lax.fori_loop(lo, lo8, lambda r, c2: (row(r).start(), c2)[1], 0)
            lax.fori_loop(lo8 >> 3, hi >> 3, lambda g, c2: (grp(g).start(), c2)[1], 0)
            lax.fori_loop(lo, lo8, lambda r, c2: (row(r).wait(), c2)[1], 0)
            lax.fori_loop(lo8 >> 3, hi >> 3, lambda g, c2: (grp(g).wait(), c2)[1], 0)
            return c

        lax.fori_loop(0, pad_lo.shape[0], per_range, 0)

    def issue(q, c):
        for u in range(SUBLANES):
            r = q * SUBLANES + u
            for k in range(2):
                _row_copy(hp_ref, q, u, xs_hbm, slot_ref[0, 0, k * ch + r], slot_ref[0, 0, (2 + k) * ch + r],
                          sem).start()
        return c

    lax.fori_loop(0, n_grp, issue, 0)
    for src, dst in zip(w_in, w_out):
        dst[...] = src[...].astype(BF16)
    for k in range(2):
        pltpu.make_async_copy(hp_ref, xs_hbm.at[pl.ds(0, n_grp)], sem).wait()


def _dispatch(hp, slots, pad_lo, pad_hi, n_rows, ch, layer, weights):
    t, w = hp.shape
    steps = t // ch
    n_e = weights[0].shape[1]
    parts = steps // n_e
    assert steps == parts * n_e
    w_in_specs, w_out_specs, w_out_shapes = [], [], []
    for m in weights:
        rows, cols = m.shape[2] // parts, m.shape[3]
        w_in_specs.append(pl.BlockSpec((None, None, rows, cols), lambda i, *_: (layer, i // parts, i % parts, 0)))
        w_out_specs.append(pl.BlockSpec((None, rows, cols), lambda i, *_: (i // parts, i % parts, 0)))
        w_out_shapes.append(jax.ShapeDtypeStruct(m.shape[1:], BF16))
    grid_spec = pltpu.PrefetchScalarGridSpec(
        num_scalar_prefetch=2,
        grid=(steps,),
        in_specs=[
            pl.BlockSpec((1, 1, 4 * ch), lambda i, *_: (i, 0, 0), memory_space=pltpu.SMEM),
            pl.BlockSpec((ch // SUBLANES, SUBLANES, w), lambda i, *_: (i, 0, 0)),
        ] + w_in_specs,
        out_specs=[pl.BlockSpec(memory_space=pl.ANY)] + w_out_specs,
        scratch_shapes=[pltpu.VMEM((1, SUBLANES, w), hp.dtype), pltpu.SemaphoreType.DMA,
                        pltpu.SemaphoreType.DMA],
    )
    xs, *w_bf = pl.pallas_call(
        _dispatch_kernel,
        grid_spec=grid_spec,
        out_shape=[jax.ShapeDtypeStruct((n_rows // SUBLANES, SUBLANES, w), hp.dtype)] + w_out_shapes,
        compiler_params=_params(("arbitrary",)),
        name="moe_dispatch",
    )(pad_lo, pad_hi, _split_row_index(slots), hp.reshape(t // SUBLANES, SUBLANES, w), *weights)
    return xs.reshape(n_rows, w), w_bf


FFN_ROW_STEPS = 4


def _pack_bf16_pair(x):
    half = x.shape[1] // 2
    bits = pltpu.bitcast(x.astype(BF16).astype(F32), jnp.uint32)
    return (bits[:, :half] >> 16) | bits[:, half:]


def _unpack_bf16_pair(xp):
    return pltpu.bitcast(xp << 16, F32), pltpu.bitcast(xp & jnp.uint32(0xFFFF0000), F32)


def _ffn_kernel(be, nv, xs_ref, wg_ref, wu_ref, wd_ref, o_ref):
    i = pl.program_id(0)
    rb = o_ref.shape[0]
    step = rb // FFN_ROW_STEPS
    n = nv[i]

    def ffn_rows(rows):
        lo, hi = _unpack_bf16_pair(xs_ref[0:rows])
        x = jnp.concatenate([lo.astype(BF16), hi.astype(BF16)], axis=1)
        hg = _dot(x, wg_ref[...])
        hu = _dot(x, wu_ref[...])
        act = (hg * (1.0 / (1.0 + jnp.exp(-hg))) * hu).astype(BF16)
        o_ref[0:rows] = _pack_bf16_pair(_dot(act, wd_ref[...]))
        if rows < rb:
            o_ref[rows:] = jnp.zeros((rb - rows, o_ref.shape[1]), o_ref.dtype)

    @pl.when(n == 0)
    def _():
        o_ref[...] = jnp.zeros_like(o_ref)

    for k in range(1, FFN_ROW_STEPS + 1):
        pl.when((n > (k - 1) * step) & (n <= k * step))(functools.partial(ffn_rows, k * step))


def _expert_ffn(xs, blk_expert, blk_valid, wg, wu, wd, rb):
    n_rows, w = xs.shape
    _, d, ff = wg.shape
    grid_spec = pltpu.PrefetchScalarGridSpec(
        num_scalar_prefetch=2,
        grid=(n_rows // rb,),
        in_specs=[
            pl.BlockSpec((rb, w), lambda i, be, nv: (i, 0)),
            pl.BlockSpec((None, d, ff), lambda i, be, nv: (be[i], 0, 0)),
            pl.BlockSpec((None, d, ff), lambda i, be, nv: (be[i], 0, 0)),
            pl.BlockSpec((None, ff, d), lambda i, be, nv: (be[i], 0, 0)),
        ],
        out_specs=pl.BlockSpec((rb, w), lambda i, be, nv: (i, 0)),
    )
    return pl.pallas_call(
        _ffn_kernel,
        grid_spec=grid_spec,
        out_shape=jax.ShapeDtypeStruct((n_rows, w), jnp.uint32),
        compiler_params=_params(("arbitrary",)),
        name="expert_ffn",
    )(blk_expert, blk_valid, xs, wg, wu, wd)


def _combine_kernel(ts, idx_ref, nxt_ref, x_ref, wt_ref, gate_ref, ys_hbm, *rest, n_a):
    outs, (buf, sem) = rest[:-2], rest[-2:]
    i = pl.program_id(0)
    tm = x_ref.shape[0]
    cur = i % 2

    n_grp = 2 * tm // SUBLANES

    def fetch(idx, b):
        def issue(q, c):
            for u in range(SUBLANES):
                r = q * SUBLANES + u
                _row_copy(ys_hbm, idx[0, 0, r], idx[0, 0, 2 * tm + r], buf.at[b], q, u, sem.at[b]).start()
            return c

        lax.fori_loop(0, n_grp, issue, 0)

    @pl.when(i == 0)
    def _():
        fetch(idx_ref, 0)

    @pl.when(i + 1 < pl.num_programs(0))
    def _():
        fetch(nxt_ref, 1 - cur)

    pltpu.make_async_copy(buf.at[cur], buf.at[cur], sem.at[cur]).wait()
    w = buf.shape[-1]
    y0 = _unpack_bf16_pair(buf[cur, 0:n_grp // 2].reshape(tm, w))
    y1 = _unpack_bf16_pair(buf[cur, n_grp // 2:n_grp].reshape(tm, w))
    w0, w1 = wt_ref[:, 0:1], wt_ref[:, 1:2]
    y = jnp.concatenate([w0 * y0[0] + w1 * y1[0], w0 * y0[1] + w1 * y1[1]], axis=1)
    res = x_ref[...] + gate_ref[...] * y
    if n_a is None:
        outs[0][...] = res
    else:
        @pl.when(i < n_a)
        def _():
            outs[0][...] = res

        @pl.when(i >= n_a)
        def _():
            outs[1][...] = res


def _combine(x, ys, slots, wt, meta, mod, tm, split_rows=None):
    t, d = x.shape
    n_tiles = t // tm
    if split_rows is None:
        n_a, out_specs = None, _rows(tm, d)
        out_shape = jax.ShapeDtypeStruct((t, d), F32)
    else:
        n_a, out_specs = split_rows // tm, _split_rows(tm, d, split_rows // tm)
        out_shape = [jax.ShapeDtypeStruct((split_rows, d), F32), jax.ShapeDtypeStruct((t - split_rows, d), F32)]
    gate_spec = pl.BlockSpec((None, None, 1, d), lambda i, ts: (5, ts[i], 0, 0))
    grid_spec = pltpu.PrefetchScalarGridSpec(
        num_scalar_prefetch=1,
        grid=(n_tiles,),
        in_specs=[
            pl.BlockSpec((1, 1, 4 * tm), lambda i, ts: (i, 0, 0), memory_space=pltpu.SMEM),
            pl.BlockSpec((1, 1, 4 * tm), lambda i, ts: (jnp.minimum(i + 1, n_tiles - 1), 0, 0),
                         memory_space=pltpu.SMEM),
            _rows(tm, d),
            _rows(tm, wt.shape[1]),
            gate_spec,
            pl.BlockSpec(memory_space=pl.ANY),
        ],
        out_specs=out_specs,
        scratch_shapes=[pltpu.VMEM((2, 2 * tm // SUBLANES, SUBLANES, ys.shape[1]), ys.dtype),
                        pltpu.SemaphoreType.DMA((2,))],
    )
    slots = _split_row_index(slots)
    ys = ys.reshape(ys.shape[0] // SUBLANES, SUBLANES, ys.shape[1])
    return pl.pallas_call(
        functools.partial(_combine_kernel, n_a=n_a),
        grid_spec=grid_spec,
        out_shape=out_shape,
        compiler_params=_params(("arbitrary",)),
        name="moe_combine",
    )(meta["seq"], slots, slots, x, wt, mod, ys)


def _tile_meta(seq_lens, tm):
    seq, pos, ln = [], [], []
    for s, n in enumerate(seq_lens):
        assert n % tm == 0
        for k in range(n // tm):
            seq.append(s)
            pos.append(k * tm)
            ln.append(n)
    return {k: jnp.asarray(np.asarray(v, np.int32)) for k, v in (("seq", seq), ("pos", pos), ("len", ln))}


def _rope_tables(s_max):
    half = HEAD_DIM // 2
    inv = ROPE_THETA ** (-jnp.arange(half, dtype=F32) / half)
    ang = jnp.arange(s_max, dtype=F32)[:, None] * inv[None, :]
    cos, sin = jnp.cos(ang), jnp.sin(ang)
    return jnp.concatenate([cos, cos], axis=1), jnp.concatenate([-sin, sin], axis=1)


def _tile_slots(slot, tm):
    t = slot.shape[1]
    return slot.reshape(2, t // tm, tm).transpose(1, 0, 2).reshape(t // tm, 1, 2 * tm)


def _route_params(norm_g, rg_w, rg_b, re_w, re_b):
    d = rg_w.shape[0]
    wr = jnp.zeros((ROUTER_ROWS, d), F32).at[0:N_GROUPS].set(rg_w.T).at[8:8 + N_EXPERTS].set(re_w.T)
    br = jnp.zeros((ROUTER_ROWS,), F32).at[N_GROUPS:8].set(NEG_BIG).at[0:N_GROUPS].set(rg_b)
    br = br.at[8:8 + N_EXPERTS].set(re_b)
    return norm_g, wr.astype(BF16), jnp.broadcast_to(br[:, None], (ROUTER_ROWS, LANES))


def _moe(x, routed, metas, mod, layer, wg, wu, wd, cfg, split_rows=None):
    t, d = x.shape
    hp, ri, rw, cnt = routed

    rb = cfg["rb"]
    expert = ri[0:2]
    rank = ri[2:4]
    counts = cnt[:, 0].astype(jnp.int32)
    nblk = (counts + rb - 1) // rb
    blk_end = jnp.cumsum(nblk)
    row_start = (blk_end - nblk) * rb
    ids = jnp.arange(N_EXPERTS, dtype=jnp.int32)
    start_of = jnp.sum(jnp.where(expert[:, :, None] == ids, row_start, 0), axis=-1)
    slot = start_of + rank
    n_blocks = -(-2 * t // rb) + N_EXPERTS
    n_used = blk_end[-1]
    blk = jnp.arange(n_blocks, dtype=jnp.int32)
    blk_expert = jnp.minimum(jnp.sum(blk[:, None] >= blk_end[None, :], axis=1), N_EXPERTS - 1).astype(jnp.int32)
    last_used = jnp.sum(jnp.where(blk == n_used - 1, blk_expert, 0))
    blk_expert = jnp.where(blk < n_used, blk_expert, last_used)
    onehot = blk_expert[:, None] == ids
    row_end_of = jnp.sum(jnp.where(onehot, row_start + counts, 0), axis=1)
    blk_valid = jnp.where(blk < n_used, jnp.clip(row_end_of - blk * rb, 0, rb), 0).astype(jnp.int32)
    pad_lo = jnp.concatenate([row_start + counts, (n_used * rb).reshape(1)])
    pad_hi = jnp.concatenate([blk_end * rb, jnp.full((1,), n_blocks * rb, jnp.int32)])

    ch = t // (cfg["dispatch_steps_per_expert"] * N_EXPERTS)
    xs, w_bf = _dispatch(hp, _tile_slots(slot, ch), pad_lo.astype(jnp.int32), pad_hi.astype(jnp.int32),
                         n_blocks * rb, ch, layer, (wg, wu, wd))
    ys = _expert_ffn(xs, blk_expert, blk_valid, *w_bf, rb)

    tmc = cfg["tm_combine"]
    wt = rw.T
    return _combine(x, ys, _tile_slots(slot, tmc), wt, metas[tmc], mod, tmc, split_rows)


def _config(seq_lens):
    g = int(np.gcd.reduce(np.asarray(seq_lens)))
    tm = min(512, g)
    t = int(np.sum(seq_lens))
    per_expert = max(1, t // (N_EXPERTS * 640))
    assert t % (per_expert * N_EXPERTS * SUBLANES) == 0
    return {"tm": tm, "tm_combine": min(512, g), "tm_gmlp": min(512, g), "rb": 512,
            "dispatch_steps_per_expert": per_expert}


def _forward(x_prompt, x_sample, c_prompt, c_sample, ada_w, ada_b, norm_mix_g, norm_ffn_g,
             ab_w_in, q_norm_g, k_norm_g, attn_sink, pool_w, pool_scale, ab_w_out,
             c_w_in, sgu_ln_g, sgu_ln_b, sgu_w, sgu_b, c_w_out,
             router_group_w, router_group_b, router_expert_w, router_expert_b,
             exp_w_gate, exp_w_up, exp_w_down, cfg=None):
    bp, sp, d = x_prompt.shape
    bs, ss, _ = x_sample.shape
    seq_lens = [sp] * bp + [ss] * bs
    n_seq = len(seq_lens)
    if cfg is None:
        cfg = _config(seq_lens)
    depth = ada_w.shape[0]
    x = (x_prompt.reshape(bp * sp, d), x_sample.reshape(bs * ss, d))
    metas ={tm: _tile_meta(seq_lens, tm) for tm in
             {cfg["tm"], cfg["tm_combine"], cfg["tm_gmlp"]}}

    n_pad = -(-n_seq // 8) * 8
    c_pad = jnp.zeros((n_pad, d), F32).at[0:n_seq].set(jnp.concatenate([c_prompt, c_sample], axis=0))
    mod_all = _ada_mod(c_pad, ada_w, ada_b)
    cos_t, sin_t = _rope_tables(max(seq_lens))

    for l in range(depth):
        mod = mod_all[l, 0:n_seq].reshape(n_seq, 6, d).transpose(1, 0, 2).reshape(6, n_seq, 1, d)
        i = l // 2
        route = _route_params(norm_ffn_g[l], router_group_w[l], router_group_b[l],
                              router_expert_w[l], router_expert_b[l])
        if l % 2 == 0:
            tm = cfg["tm"]
            q, k, v, p = _inproj(x, metas[tm], mod, norm_mix_g[l], ab_w_in[i].astype(BF16),
                                 q_norm_g[i], k_norm_g[i], cos_t, sin_t, tm)
            sink_b = jnp.broadcast_to(
                jnp.repeat(attn_sink[i].reshape(N_KV_HEADS, GQA_GROUP), ATT_BLOCK, axis=1)[:, :, None],
                (N_KV_HEADS, GQA_GROUP * ATT_BLOCK, LANES)).astype(F32)
            a = _attention(q, k, v, sink_b, metas[tm], tm)
            x, *routed = _mix0(x, a, p, metas[tm], mod, pool_w[i].astype(BF16), pool_scale[i],
                               ab_w_out[i].astype(BF16), route, tm)
        else:
            tm = cfg["tm_gmlp"]
            if isinstance(x, tuple):
                x = jnp.concatenate(x, axis=0)
            u, v = _gmlp_in(x,metas[tm], mod, norm_mix_g[l], c_w_in[i].astype(BF16),
                            sgu_ln_g[i], sgu_ln_b[i], tm)
            sgu_b_b = jnp.broadcast_to(sgu_b[i][:, :, None], (SGU_HEADS, CHUNK, LANES)).astype(F32)
            x, *routed = _gmlp_out(x, u, v, metas[tm], mod, sgu_w[i].astype(BF16), sgu_b_b,
                                   c_w_out[i].astype(BF16), route, tm)
        x = _moe(x, routed, metas, mod, l, exp_w_gate, exp_w_up, exp_w_down, cfg,
                 split_rows=bp * sp if l == depth - 1 else None)

    y_prompt, y_sample = x
    return (y_prompt.reshape(bp, sp, d), y_sample.reshape(bs, ss, d))


def kernel(x_prompt, x_sample, c_prompt, c_sample, ada_w, ada_b, norm_mix_g, norm_ffn_g, ab_w_in, q_norm_g,
           k_norm_g, attn_sink, pool_w, pool_scale, ab_w_out, c_w_in, sgu_ln_g, sgu_ln_b, sgu_w, sgu_b,
           c_w_out, router_group_w, router_group_b, router_expert_w, router_expert_b, exp_w_gate, exp_w_up,
           exp_w_down):
    return _forward(x_prompt, x_sample, c_prompt, c_sample, ada_w, ada_b, norm_mix_g, norm_ffn_g, ab_w_in,
                    q_norm_g, k_norm_g, attn_sink, pool_w, pool_scale, ab_w_out, c_w_in, sgu_ln_g, sgu_ln_b,
                    sgu_w, sgu_b, c_w_out, router_group_w, router_group_b, router_expert_w, router_expert_b,
                    exp_w_gate, exp_w_up, exp_w_down)
```

```python
import functools

import numpy as np
import jax
import jax.numpy as jnp
from jax import lax
from jax.experimental import pallas as pl
from jax.experimental.pallas import tpu as pltpu

HEAD_DIM = 128
N_HEADS = 8
N_KV_HEADS = 2
GQA_GROUP = N_HEADS // N_KV_HEADS
ATT_BLOCK = 128
ROPE_THETA = 10000.0
Q_W = N_HEADS * HEAD_DIM
KV_W = N_KV_HEADS * HEAD_DIM
POOL_WINDOWS = (2, 4, 8, 16)
POOL_HALO = 8
CHUNK = 128
SGU_HEADS = 8
N_GROUPS = 4
EXP_PER_GROUP = 8
N_EXPERTS = N_GROUPS * EXP_PER_GROUP
EPS = 1e-6

VMEM_LIMIT_BYTES = 56 * 1024 * 1024
LANES = 128
SUBLANES = 8

F32 = jnp.float32
BF16 = jnp.bfloat16
NEG_BIG = -1e30


def _params(sem):
    return pltpu.CompilerParams(dimension_semantics=sem, vmem_limit_bytes=VMEM_LIMIT_BYTES)


def _resident(shape):
    nd = len(shape)
    return pl.BlockSpec(shape, lambda *_: (0,) * nd, pipeline_mode=pl.Buffered(1))


def _rows(tm, width):
    return pl.BlockSpec((tm, width), lambda i, *_: (i, 0))


def _split_rows(tm, width, n_a):
    return [pl.BlockSpec((tm, width), lambda i, *_: (jnp.minimum(i, n_a - 1), 0)),
            pl.BlockSpec((tm, width), lambda i, *_: (jnp.maximum(i - n_a, 0), 0))]


def _as_pair(x, tm):
    if isinstance(x, tuple):
        return x[0], x[1], x[0].shape[0] // tm
    return x, x, x.shape[0] // tm


def _mod_spec(part, d, seq_arg=0):
    return pl.BlockSpec((None, None, 1, d), lambda i, *pf: (part, pf[seq_arg][i], 0, 0))


def _norm_mod(x, g, sc, sh):
    r = lax.rsqrt(jnp.mean(x * x, axis=-1, keepdims=True) + EPS)
    return x * r * g * (1.0 + sc) + sh


def _dot(a, b):
    return jnp.dot(a, b, preferred_element_type=F32)


def _dot_nt(a, b):
    return lax.dot_general(a, b, (((1,), (1,)), ((), ())), preferred_element_type=F32)


def _ada_kernel(c_ref, w_ref, b_ref, o_ref):
    c = c_ref[...]
    cs = c * (1.0 / (1.0 + jnp.exp(-c)))
    o_ref[...] = _dot(cs.astype(BF16), w_ref[...].astype(BF16)) + b_ref[...]


def _ada_mod(c_pad, ada_w, ada_b):
    depth, d, n = ada_w.shape
    tn = 1024
    return pl.pallas_call(
        _ada_kernel,
        grid=(depth, n // tn),
        in_specs=[
            pl.BlockSpec(c_pad.shape, lambda l, j: (0, 0)),
            pl.BlockSpec((None, d, tn), lambda l, j: (l, 0, j)),
            pl.BlockSpec((None, 1, tn), lambda l, j: (l, 0, j)),
        ],
        out_specs=pl.BlockSpec((None, c_pad.shape[0], tn), lambda l, j: (l, 0, j)),
        out_shape=jax.ShapeDtypeStruct((depth, c_pad.shape[0], n), F32),
        compiler_params=_params(("arbitrary", "arbitrary")),
        name="ada_mod",
    )(c_pad, ada_w, ada_b.reshape(depth, 1, n))


def _inproj_kernel(ts, tp, xa_ref, xb_ref, g_ref, sc_ref, sh_ref, w_ref, qg_ref, kg_ref, cos_ref, sin_ref,
                   q_ref, k_ref, v_ref, p_ref, *, n_a):
    x = jnp.where(pl.program_id(0) < n_a, xa_ref[...], xb_ref[...])
    h = _norm_mod(x, g_ref[...], sc_ref[...], sh_ref[...]).astype(BF16)
    cos = cos_ref[...]
    sin = sin_ref[...]

    def head_norm_rope(y, gain):
        r = lax.rsqrt(jnp.mean(y * y, axis=-1, keepdims=True) + EPS)
        y = y * r * gain
        return y * cos + pltpu.roll(y, HEAD_DIM // 2, 1) * sin

    q = _dot(h, w_ref[:, 0:Q_W])
    for hh in range(N_HEADS):
        sl = slice(hh * HEAD_DIM, (hh + 1) * HEAD_DIM)
        q_ref[:, sl] = head_norm_rope(q[:, sl], qg_ref[...]).astype(BF16)
    kv = _dot(h, w_ref[:, Q_W:Q_W + 2 * KV_W])
    for hh in range(N_KV_HEADS):
        sl = slice(hh * HEAD_DIM, (hh + 1) * HEAD_DIM)
        k_ref[:, sl] = head_norm_rope(kv[:, sl], kg_ref[...]).astype(BF16)
    v_ref[...] = kv[:, KV_W:].astype(BF16)
    p_ref[...] = _dot(h, w_ref[:, Q_W + 2 * KV_W:])


def _inproj(x, meta, mod, norm_g, w_in_bf, q_g, k_g, cos_t, sin_t, tm):
    xa, xb, n_a = _as_pair(x, tm)
    d = xa.shape[1]
    n_tiles = meta["seq"].shape[0]
    t = n_tiles * tm
    pool_w = w_in_bf.shape[1] - Q_W - 2 * KV_W
    rope_spec = pl.BlockSpec((tm, HEAD_DIM), lambda i, ts, tp: (tp[i] // tm, 0))
    grid_spec = pltpu.PrefetchScalarGridSpec(
        num_scalar_prefetch=2,
        grid=(n_tiles,),
        in_specs=_split_rows(tm, d, n_a) + [
            _resident((1, d)),
            _mod_spec(1, d),
            _mod_spec(0, d),
            _resident(w_in_bf.shape),
            _resident((1, HEAD_DIM)),
            _resident((1, HEAD_DIM)),
            rope_spec,
            rope_spec,
        ],
        out_specs=[_rows(tm, Q_W), _rows(tm, KV_W), _rows(tm, KV_W), _rows(tm, pool_w)],
    )
    return pl.pallas_call(
        functools.partial(_inproj_kernel, n_a=n_a),
        grid_spec=grid_spec,
        out_shape=[
            jax.ShapeDtypeStruct((t, Q_W), BF16),
            jax.ShapeDtypeStruct((t, KV_W), BF16),
            jax.ShapeDtypeStruct((t, KV_W), BF16),
            jax.ShapeDtypeStruct((t, pool_w), F32),
        ],
        compiler_params=_params(("arbitrary",)),
        name="attn_pool_inproj",
    )(meta["seq"], meta["pos"], xa, xb, norm_g.reshape(1, d), mod, mod, w_in_bf,
      q_g.reshape(1, HEAD_DIM), k_g.reshape(1, HEAD_DIM), cos_t, sin_t)


def _attn_kernel(tp, tl, q_ref, kc_ref, kp_ref, kn_ref, vc_ref, vp_ref, vn_ref, sink_ref, o_ref,
                 kx, vx, s_scr, p_scr, r_scr):
    i = pl.program_id(0)
    tm = q_ref.shape[0]
    nb = tm // ATT_BLOCK
    first = tp[i] == 0
    last = tp[i] + tm == tl[i]
    kx[0:ATT_BLOCK] = kp_ref[...]
    kx[ATT_BLOCK:ATT_BLOCK + tm] = kc_ref[...]
    kx[ATT_BLOCK + tm:] = kn_ref[...]
    vx[0:ATT_BLOCK] = vp_ref[...]
    vx[ATT_BLOCK:ATT_BLOCK + tm] = vc_ref[...]
    vx[ATT_BLOCK + tm:] = vn_ref[...]
    win = 3 * ATT_BLOCK
    qi = lax.broadcasted_iota(jnp.int32, (ATT_BLOCK, win), 0)
    kj = lax.broadcasted_iota(jnp.int32, (ATT_BLOCK, win), 1)
    band = (kj >= qi) & (kj <= qi + 2 * ATT_BLOCK)
    scale = HEAD_DIM ** -0.5
    units = [(b, kk) for b in range(nb) for kk in range(N_KV_HEADS)]

    def heads_of(kk):
        return [kk * GQA_GROUP + g for g in range(GQA_GROUP)]

    for n, (b, kk) in enumerate(units):
        rows = slice(b * ATT_BLOCK, (b + 1) * ATT_BLOCK)
        kw = kx[b * ATT_BLOCK:b * ATT_BLOCK + win, kk * HEAD_DIM:(kk + 1) * HEAD_DIM]
        qs = jnp.concatenate([q_ref[rows, hd * HEAD_DIM:(hd + 1) * HEAD_DIM] for hd in heads_of(kk)], axis=0)
        s_scr[n] = _dot_nt(qs, kw)
    for n, (b, kk) in enumerate(units):
        valid = band
        if b == 0:
            valid = valid & (kj >= jnp.where(first, ATT_BLOCK, 0))
        if b == nb - 1:
            valid = valid & (kj < jnp.where(last, 2 * ATT_BLOCK, win))
        bias = jnp.where(valid, 0.0, -jnp.inf).astype(F32)
        s = s_scr[n] * scale
        s = (s.reshape(GQA_GROUP, ATT_BLOCK, win) + bias[None]).reshape(GQA_GROUP * ATT_BLOCK, win)
        sk = sink_ref[kk][:, 0:1]
        m = jnp.maximum(jnp.max(s, axis=-1, keepdims=True), sk)
        p = jnp.exp(s - m)
        denom = jnp.sum(p, axis=-1, keepdims=True) + jnp.exp(sk - m)
        p_scr[n] = p.astype(BF16)
        r_scr[n] = jnp.broadcast_to(1.0 / denom, r_scr.shape[1:])
    for n, (b, kk) in enumerate(units):
        rows = slice(b * ATT_BLOCK, (b + 1) * ATT_BLOCK)
        vw = vx[b * ATT_BLOCK:b * ATT_BLOCK + win, kk * HEAD_DIM:(kk + 1) * HEAD_DIM]
        o = _dot(p_scr[n], vw) * r_scr[n]
        for g, hd in enumerate(heads_of(kk)):
            o_ref[rows, hd * HEAD_DIM:(hd + 1) * HEAD_DIM] = o[g * ATT_BLOCK:(g + 1) * ATT_BLOCK].astype(BF16)


def _attention(q, k, v, sink_b, meta, tm):
    t = q.shape[0]
    n_tiles = t // tm
    r = tm // ATT_BLOCK
    n_blk = t // ATT_BLOCK
    cur = pl.BlockSpec((tm, KV_W), lambda i, *_: (i, 0))
    prev = pl.BlockSpec((ATT_BLOCK, KV_W), lambda i, *_: (jnp.maximum(i * r - 1, 0), 0))
    nxt = pl.BlockSpec((ATT_BLOCK, KV_W), lambda i, *_: (jnp.minimum((i + 1) * r, n_blk - 1), 0))
    grid_spec = pltpu.PrefetchScalarGridSpec(
        num_scalar_prefetch=2,
        grid=(n_tiles,),
        in_specs=[_rows(tm, Q_W), cur, prev, nxt, cur, prev, nxt, _resident(sink_b.shape)],
        out_specs=_rows(tm, Q_W),
        scratch_shapes=[pltpu.VMEM((tm + 2 * ATT_BLOCK, KV_W), BF16),
                        pltpu.VMEM((tm + 2 * ATT_BLOCK, KV_W), BF16),
                        pltpu.VMEM((r * N_KV_HEADS, GQA_GROUP * ATT_BLOCK, 3 * ATT_BLOCK), F32),
                        pltpu.VMEM((r * N_KV_HEADS, GQA_GROUP * ATT_BLOCK, 3 * ATT_BLOCK), BF16),
                        pltpu.VMEM((r * N_KV_HEADS, GQA_GROUP * ATT_BLOCK, HEAD_DIM), F32)],
    )
    return pl.pallas_call(
        _attn_kernel,
        grid_spec=grid_spec,
        out_shape=jax.ShapeDtypeStruct((t, Q_W), BF16),
        compiler_params=_params(("arbitrary",)),
        name="banded_attention",
    )(meta["pos"], meta["len"], q, k, k, k, v, v, v, sink_b)


def _mix0_kernel(tp, tl, ts, xa_ref, xb_ref, a_ref, pc_ref, pp_ref, pn_ref, gate_ref, pw_ref, ps_ref, wo_ref,
                 *rest, n_a):
    route_in, o_ref = rest[:N_ROUTE_IN], rest[N_ROUTE_IN]
    route_out, (pext, carry) = rest[N_ROUTE_IN + 1:N_ROUTE_IN + 1 + N_ROUTE_OUT], rest[-2:]
    i = pl.program_id(0)
    tm = xa_ref.shape[0]
    first = tp[i] == 0
    last = tp[i] + tm == tl[i]
    pext[0:POOL_HALO] = jnp.where(first, 0.0, pp_ref[...])
    pext[POOL_HALO:POOL_HALO + tm] = pc_ref[...]
    pext[POOL_HALO + tm:] = jnp.where(last, 0.0, pn_ref[...])
    n_g = len(POOL_WINDOWS)
    gw = pc_ref.shape[1] // n_g
    pos = tp[i] + lax.broadcasted_iota(jnp.int32, (tm, gw), 0)
    seq_len = tl[i]
    ms = []
    for g, w in enumerate(POOL_WINDOWS):
        cols = slice(g * gw, (g + 1) * gw)
        acc = pext[POOL_HALO - w // 2:POOL_HALO - w // 2 + tm, cols]
        for off in range(-w // 2 + 1, w // 2):
            acc = acc + pext[POOL_HALO + off:POOL_HALO + off + tm, cols]
        cnt = (jnp.minimum(pos + w // 2, seq_len) - jnp.maximum(pos - w // 2, 0)).astype(F32)
        dlt = acc / cnt - pc_ref[:, cols]
        ms.append((_dot(dlt.astype(BF16), pw_ref[g]) * ps_ref[:, cols]).astype(BF16))
    m = jnp.concatenate(ms, axis=1)
    w_a = a_ref.shape[1]
    mix = _dot(a_ref[...], wo_ref[0:w_a, :]) + _dot(m, wo_ref[w_a:, :])
    x = jnp.where(i < n_a, xa_ref[...], xb_ref[...])
    x_new = x + gate_ref[...] * mix
    o_ref[...] = x_new
    _route_tile(x_new, *route_in, *route_out, carry)


def _mix0(x, a, p, meta, mod, pool_w_bf, pool_scale, w_out_bf, route, tm):
    xa, xb, n_a = _as_pair(x, tm)
    d = xa.shape[1]
    t, pw = p.shape
    n_tiles = t // tm
    r = tm // POOL_HALO
    n_hb = t // POOL_HALO
    prev = pl.BlockSpec((POOL_HALO, pw), lambda i, *_: (jnp.maximum(i * r - 1, 0), 0))
    nxt = pl.BlockSpec((POOL_HALO, pw), lambda i, *_: (jnp.minimum((i + 1) * r, n_hb - 1), 0))
    r_in, r_out, r_shapes, r_scratch = _route_specs(tm, t, d, seq_arg=2)
    grid_spec = pltpu.PrefetchScalarGridSpec(
        num_scalar_prefetch=3,
        grid=(n_tiles,),
        in_specs=_split_rows(tm, d, n_a) + [
            _rows(tm, a.shape[1]), _rows(tm, pw), prev, nxt, _mod_spec(2, d, seq_arg=2),
            _resident(pool_w_bf.shape), _resident((1, pw)), _resident(w_out_bf.shape)] + r_in,
        out_specs=[_rows(tm, d)] + r_out,
        scratch_shapes=[pltpu.VMEM((tm + 2 * POOL_HALO, pw), F32), r_scratch],
    )
    return pl.pallas_call(
        functools.partial(_mix0_kernel, n_a=n_a),
        grid_spec=grid_spec,
        out_shape=[jax.ShapeDtypeStruct((t, d), F32)] + r_shapes,
        compiler_params=_params(("arbitrary",)),
        name="pool_outproj_route",
    )(meta["pos"], meta["len"], meta["seq"], xa, xb, a, p, p, p, mod, pool_w_bf,
      pool_scale.reshape(1, pw), w_out_bf, *_route_operands(route, mod, d))


def _gelu(z):
    return 0.5 * z * (1.0 + lax.erf(z * np.float32(np.sqrt(0.5))))


def _gmlp_in_kernel(ts, x_ref, g_ref, sc_ref, sh_ref, w_ref, lg_ref, lb_ref, u_ref, v_ref):
    h = _norm_mod(x_ref[...], g_ref[...], sc_ref[...], sh_ref[...]).astype(BF16)
    half = u_ref.shape[1]
    u_ref[...] = _gelu(_dot(h, w_ref[:, 0:half]))
    zv = _gelu(_dot(h, w_ref[:, half:]))
    zc = zv - jnp.mean(zv, axis=-1, keepdims=True)
    r = lax.rsqrt(jnp.mean(zc * zc, axis=-1, keepdims=True) + EPS)
    v_ref[...] = (zc * r * lg_ref[...] + lb_ref[...]).astype(BF16)


def _gmlp_in(x, meta, mod, norm_g, w_in_bf, ln_g, ln_b, tm):
    t, d = x.shape
    half = w_in_bf.shape[1] // 2
    grid_spec = pltpu.PrefetchScalarGridSpec(
        num_scalar_prefetch=1,
        grid=(t // tm,),
        in_specs=[_rows(tm, d), _resident((1, d)), _mod_spec(1, d), _mod_spec(0, d),
                  _resident(w_in_bf.shape), _resident((1, half)), _resident((1, half))],
        out_specs=[_rows(tm, half), _rows(tm, half)],
    )
    return pl.pallas_call(
        _gmlp_in_kernel,
        grid_spec=grid_spec,
        out_shape=[jax.ShapeDtypeStruct((t, half), F32), jax.ShapeDtypeStruct((t, half), BF16)],
        compiler_params=_params(("arbitrary",)),
        name="gmlp_in",
    )(meta["seq"], x, norm_g.reshape(1, d), mod, mod, w_in_bf, ln_g.reshape(1, half), ln_b.reshape(1, half))


def _gmlp_out_kernel(ts, x_ref, u_ref, v_ref, gate_ref, sw_ref, sb_ref, wo_ref, *rest):
    route_in, o_ref = rest[:N_ROUTE_IN], rest[N_ROUTE_IN]
    route_out, (gated, carry) = rest[N_ROUTE_IN + 1:N_ROUTE_IN + 1 + N_ROUTE_OUT], rest[-2:]
    tm = x_ref.shape[0]
    hd = u_ref.shape[1] // SGU_HEADS
    for c in range(tm // CHUNK):
        rows = slice(c * CHUNK, (c + 1) * CHUNK)
        for hh in range(SGU_HEADS):
            cols = slice(hh * hd, (hh + 1) * hd)
            s = _dot(sw_ref[hh], v_ref[rows, cols]) + jnp.tile(sb_ref[hh], (1, hd // LANES))
            gated[rows, cols] = (u_ref[rows, cols] * s).astype(BF16)
    x_new = x_ref[...] + gate_ref[...] * _dot(gated[...], wo_ref[...])
    o_ref[...] = x_new
    _route_tile(x_new, *route_in, *route_out, carry)


def _gmlp_out(x, u, v, meta, mod, sgu_w_bf, sgu_b_b, w_out_bf, route, tm):
    t, d = x.shape
    w = u.shape[1]
    r_in, r_out, r_shapes, r_scratch = _route_specs(tm, t, d, seq_arg=0)
    grid_spec = pltpu.PrefetchScalarGridSpec(
        num_scalar_prefetch=1,
        grid=(t // tm,),
        in_specs=[_rows(tm, d), _rows(tm, w), _rows(tm, w), _mod_spec(2, d),
                  _resident(sgu_w_bf.shape), _resident(sgu_b_b.shape), _resident(w_out_bf.shape)] + r_in,
        out_specs=[_rows(tm, d)] + r_out,
        scratch_shapes=[pltpu.VMEM((tm, w), BF16), r_scratch],
    )
    return pl.pallas_call(
        _gmlp_out_kernel,
        grid_spec=grid_spec,
        out_shape=[jax.ShapeDtypeStruct((t, d), F32)] + r_shapes,
        compiler_params=_params(("arbitrary",)),
        name="gmlp_out_route",
    )(meta["seq"], x, u, v, mod, sgu_w_bf, sgu_b_b, w_out_bf, *_route_operands(route, mod, d))


ROUTER_ROWS = 48


def _route_tile(x, g_ref, sc_ref, sh_ref, wr_ref, br_ref, h_ref, ri_ref, rw_ref, cnt_ref, carry):
    i = pl.program_id(0)

    @pl.when(i == 0)
    def _():
        carry[...] = jnp.zeros_like(carry)

    tm = x.shape[0]
    hb = _norm_mod(x, g_ref[...], sc_ref[...], sh_ref[...]).astype(BF16)
    h_ref[...] = _rows_to_tiles(_pack_bf16_pair(hb))
    lg = _dot_nt(wr_ref[...], hb) + jnp.tile(br_ref[...], (1, tm // LANES))
    rows8 = lax.broadcasted_iota(jnp.int32, (EXP_PER_GROUP, tm), 0)

    def first_argmax(vals, vmax):
        return jnp.min(jnp.where(vals == vmax, rows8, EXP_PER_GROUP), axis=0, keepdims=True)

    gl = lg[0:8]
    gmax = jnp.max(gl, axis=0, keepdims=True)
    gidx = first_argmax(gl, gmax)
    g_w = 1.0 / jnp.sum(jnp.exp(gl - gmax), axis=0, keepdims=True)
    esel = jnp.zeros((EXP_PER_GROUP, tm), F32)
    for g in range(N_GROUPS):
        esel = jnp.where(gidx == g, lg[8 + g * EXP_PER_GROUP:8 + (g + 1) * EXP_PER_GROUP], esel)
    emax = jnp.max(esel, axis=0, keepdims=True)
    pe = jnp.exp(esel - emax)
    prob = pe / jnp.sum(pe, axis=0, keepdims=True)
    p1 = jnp.max(prob, axis=0, keepdims=True)
    i1 = first_argmax(prob, p1)
    rest = jnp.where(rows8 == i1, -1.0, prob)
    p2 = jnp.max(rest, axis=0, keepdims=True)
    i2 = first_argmax(rest, p2)
    den = p1 + p2
    w0 = g_w * (p1 / den)
    w1 = g_w * (p2 / den)
    e0 = gidx * EXP_PER_GROUP + i1
    e1 = gidx * EXP_PER_GROUP + i2

    rows_e = lax.broadcasted_iota(jnp.int32, (N_EXPERTS, tm), 0)
    oh0 = rows_e == e0
    oh1 = rows_e == e1
    both = jnp.where(oh0, 1.0, 0.0) + jnp.where(oh1, 1.0, 0.0)
    ri = lax.broadcasted_iota(jnp.int32, (tm, tm), 0)
    ci = lax.broadcasted_iota(jnp.int32, (tm, tm), 1)
    upper = jnp.where(ri < ci, 1.0, 0.0).astype(BF16)
    before = _dot(both.astype(BF16), upper) + carry[:, 0:1]
    r0 = jnp.sum(jnp.where(oh0, before, 0.0), axis=0, keepdims=True)
    r1 = jnp.sum(jnp.where(oh1, before, 0.0), axis=0, keepdims=True)
    new_cnt = carry[...] + jnp.sum(both, axis=1, keepdims=True)
    carry[...] = new_cnt
    cnt_ref[...] = new_cnt
    zi = jnp.zeros((4, tm), jnp.int32)
    ri_ref[...] = jnp.concatenate([e0, e1, r0.astype(jnp.int32), r1.astype(jnp.int32), zi], axis=0)
    rw_ref[...] = jnp.concatenate([w0, w1, jnp.zeros((6, tm), F32)], axis=0)


N_ROUTE_IN, N_ROUTE_OUT = 5, 4


def _route_specs(tm, t, d, seq_arg):
    in_specs = [_resident((1, d)), _mod_spec(4, d, seq_arg), _mod_spec(3, d, seq_arg),
                _resident((ROUTER_ROWS, d)), _resident((ROUTER_ROWS, LANES))]
    assert d // 2 == SUBLANES * LANES
    out_specs = [pl.BlockSpec((tm, SUBLANES, LANES), lambda i, *_: (i, 0, 0)),
                 pl.BlockSpec((8, tm), lambda i, *_: (0, i)),
                 pl.BlockSpec((8, tm), lambda i, *_: (0, i)),
                 pl.BlockSpec((N_EXPERTS, LANES), lambda i, *_: (0, 0))]
    out_shapes = [jax.ShapeDtypeStruct((t, SUBLANES, LANES), jnp.uint32),
                  jax.ShapeDtypeStruct((8, t), jnp.int32),
                  jax.ShapeDtypeStruct((8, t), F32),
                  jax.ShapeDtypeStruct((N_EXPERTS, LANES), F32)]
    return in_specs, out_specs, out_shapes, pltpu.VMEM((N_EXPERTS, LANES), F32)


def _route_operands(route, mod, d):
    norm_g, wr_bf, br_b = route
    return [norm_g.reshape(1, d), mod, mod, wr_bf, br_b]


def _row_copy(src, s_row, dst, d_row, sem):
    return pltpu.make_async_copy(src.at[s_row], dst.at[d_row], sem)


N_EXPERT_MATS = 3
ZERO_ROWS = 32


def _dispatch_kernel(pad_lo, pad_hi, slot_ref, hp_ref, *rest):
    w_in, xs_hbm = rest[:N_EXPERT_MATS], rest[N_EXPERT_MATS]
    w_out, (zrow, sem, zsem) = rest[N_EXPERT_MATS + 1:2 * N_EXPERT_MATS + 1], rest[-3:]
    i = pl.program_id(0)
    ch = slot_ref.shape[2] // 2
    zn = zrow.shape[0]

    @pl.when(i == 0)
    def _():
        zrow[...] = jnp.zeros_like(zrow)

        def per_range(e, c):
            lo, hi = pad_lo[e], pad_hi[e]
            lo_al = jnp.minimum(((lo + zn - 1) // zn) * zn, hi)

            def row(r):
                return _row_copy(zrow, 0, xs_hbm, r, zsem)

            def grp(g):
                return pltpu.make_async_copy(zrow, xs_hbm.at[pl.ds(g * zn, zn)], zsem)

            lax.fori_loop(lo, lo_al, lambda r, c2: (row(r).start(), c2)[1], 0)
            lax.fori_loop(lo_al // zn, hi // zn, lambda g, c2: (grp(g).start(), c2)[1], 0)
            lax.fori_loop(lo, lo_al, lambda r, c2: (row(r).wait(), c2)[1], 0)
            lax.fori_loop(lo_al // zn, hi // zn, lambda g, c2: (grp(g).wait(), c2)[1], 0)
            return c

        lax.fori_loop(0, pad_lo.shape[0], per_range, 0)

    def issue(q, c):
        for u in range(SUBLANES):
            r = q * SUBLANES + u
            for k in range(2):
                _row_copy(hp_ref, r, xs_hbm, slot_ref[0, 0, k * ch + r], sem).start()
        return c

    lax.fori_loop(0, ch // SUBLANES, issue, 0)
    for src, dst in zip(w_in, w_out):
        dst[...] = src[...].astype(BF16)
    for k in range(2):
        pltpu.make_async_copy(hp_ref, xs_hbm.at[pl.ds(0, ch)], sem).wait()


def _dispatch(hp, slots, pad_lo, pad_hi, n_rows, ch, layer, weights):
    t = hp.shape[0]
    steps = t // ch
    n_e = weights[0].shape[1]
    parts = steps // n_e
    assert steps == parts * n_e
    w_in_specs, w_out_specs, w_out_shapes = [], [], []
    for m in weights:
        rows, cols = m.shape[2] // parts, m.shape[3]
        w_in_specs.append(pl.BlockSpec((None, None, rows, cols), lambda i, *_: (layer, i // parts, i % parts, 0)))
        w_out_specs.append(pl.BlockSpec((None, rows, cols), lambda i, *_: (i // parts, i % parts, 0)))
        w_out_shapes.append(jax.ShapeDtypeStruct(m.shape[1:], BF16))
    grid_spec = pltpu.PrefetchScalarGridSpec(
        num_scalar_prefetch=2,
        grid=(steps,),
        in_specs=[
            pl.BlockSpec((1, 1, 2 * ch), lambda i, *_: (i, 0, 0), memory_space=pltpu.SMEM),
            pl.BlockSpec((ch,) + hp.shape[1:], lambda i, *_: (i, 0, 0)),
        ] + w_in_specs,
        out_specs=[pl.BlockSpec(memory_space=pl.ANY)] + w_out_specs,
        scratch_shapes=[pltpu.VMEM((ZERO_ROWS,) + hp.shape[1:], hp.dtype), pltpu.SemaphoreType.DMA,
                        pltpu.SemaphoreType.DMA],
    )
    xs, *w_bf = pl.pallas_call(
        _dispatch_kernel,
        grid_spec=grid_spec,
        out_shape=[jax.ShapeDtypeStruct((n_rows,) + hp.shape[1:], hp.dtype)] + w_out_shapes,
        compiler_params=_params(("arbitrary",)),
        name="moe_dispatch",
    )(pad_lo, pad_hi, slots, hp, *weights)
    return xs, w_bf


FFN_ROW_STEPS = 4


def _rows_to_tiles(x):
    n = x.shape[0]
    parts = [x[:, c * LANES:(c + 1) * LANES].reshape(n // SUBLANES, SUBLANES, LANES) for c in range(SUBLANES)]
    a = jnp.swapaxes(jnp.stack(parts, axis=1), 1, 2)
    return a.reshape(n, SUBLANES, LANES)


def _tiles_to_rows(a):
    n = a.shape[0]
    a = jnp.swapaxes(a.reshape(n // SUBLANES, SUBLANES, SUBLANES, LANES), 1, 2)
    return jnp.concatenate([a[:, c].reshape(n, LANES) for c in range(SUBLANES)], axis=1)


def _pack_bf16_pair(x):
    half = x.shape[1] // 2
    bits = pltpu.bitcast(x.astype(BF16).astype(F32), jnp.uint32)
    return (bits[:, :half] >> 16) | bits[:, half:]


def _unpack_bf16_pair(xp):
    return pltpu.bitcast(xp << 16, F32), pltpu.bitcast(xp & jnp.uint32(0xFFFF0000), F32)


def _ffn_kernel(be, nv, xs_ref, wg_ref, wu_ref, wd_ref, o_ref):
    i = pl.program_id(0)
    rb = o_ref.shape[0]
    step = rb // FFN_ROW_STEPS
    n = nv[i]

    def ffn_rows(rows):
        lo, hi = _unpack_bf16_pair(_tiles_to_rows(xs_ref[0:rows]))
        x = jnp.concatenate([lo.astype(BF16), hi.astype(BF16)], axis=1)
        hg = _dot(x, wg_ref[...])
        hu = _dot(x, wu_ref[...])
        act = (hg * (1.0 / (1.0 + jnp.exp(-hg))) * hu).astype(BF16)
        o_ref[0:rows] = _rows_to_tiles(_pack_bf16_pair(_dot(act, wd_ref[...])))
        if rows < rb:
            o_ref[rows:] = jnp.zeros((rb - rows,) + o_ref.shape[1:], o_ref.dtype)

    @pl.when(n == 0)
    def _():
        o_ref[...] = jnp.zeros_like(o_ref)

    for k in range(1, FFN_ROW_STEPS + 1):
        pl.when((n > (k - 1) * step) & (n <= k * step))(functools.partial(ffn_rows, k * step))


def _expert_ffn(xs, blk_expert, blk_valid, wg, wu, wd, rb):
    n_rows = xs.shape[0]
    _, d, ff = wg.shape
    row_block = pl.BlockSpec((rb,) + xs.shape[1:], lambda i, be, nv: (i, 0, 0))
    grid_spec = pltpu.PrefetchScalarGridSpec(
        num_scalar_prefetch=2,
        grid=(n_rows // rb,),
        in_specs=[
            row_block,
            pl.BlockSpec((None, d, ff), lambda i, be, nv: (be[i], 0, 0)),
            pl.BlockSpec((None, d, ff), lambda i, be, nv: (be[i], 0, 0)),
            pl.BlockSpec((None, ff, d), lambda i, be, nv: (be[i], 0, 0)),
        ],
        out_specs=row_block,
    )
    return pl.pallas_call(
        _ffn_kernel,
        grid_spec=grid_spec,
        out_shape=jax.ShapeDtypeStruct(xs.shape, xs.dtype),
        compiler_params=_params(("arbitrary",)),
        name="expert_ffn",
    )(blk_expert, blk_valid, xs, wg, wu, wd)


def _combine_kernel(ts, idx_ref, nxt_ref, x_ref, wt_ref, gate_ref, ys_hbm, *rest, n_a):
    outs, (buf, sem) = rest[:-2], rest[-2:]
    i = pl.program_id(0)
    tm = x_ref.shape[0]
    cur = i % 2

    def fetch(idx, b):
        def issue(q, c):
            for u in range(SUBLANES):
                r = q * SUBLANES + u
                _row_copy(ys_hbm, idx[0, 0, r], buf.at[b], r, sem.at[b]).start()
            return c

        lax.fori_loop(0, 2 * tm // SUBLANES, issue, 0)

    @pl.when(i == 0)
    def _():
        fetch(idx_ref, 0)

    @pl.when(i + 1 < pl.num_programs(0))
    def _():
        fetch(nxt_ref, 1 - cur)

    pltpu.make_async_copy(buf.at[cur], buf.at[cur], sem.at[cur]).wait()
    y0 = _unpack_bf16_pair(_tiles_to_rows(buf[cur, 0:tm]))
    y1 = _unpack_bf16_pair(_tiles_to_rows(buf[cur, tm:2 * tm]))
    w0, w1 = wt_ref[:, 0:1], wt_ref[:, 1:2]
    y = jnp.concatenate([w0 * y0[0] + w1 * y1[0], w0 * y0[1] + w1 * y1[1]], axis=1)
    res = x_ref[...] + gate_ref[...] * y
    if n_a is None:
        outs[0][...] = res
    else:
        @pl.when(i < n_a)
        def _():
            outs[0][...] = res

        @pl.when(i >= n_a)
        def _():
            outs[1][...] = res


def _combine(x, ys, slots, wt, meta, mod, tm, split_rows=None):
    t, d = x.shape
    n_tiles = t // tm
    if split_rows is None:
        n_a, out_specs = None, _rows(tm, d)
        out_shape = jax.ShapeDtypeStruct((t, d), F32)
    else:
        n_a, out_specs = split_rows // tm, _split_rows(tm, d, split_rows // tm)
        out_shape = [jax.ShapeDtypeStruct((split_rows, d), F32), jax.ShapeDtypeStruct((t - split_rows, d), F32)]
    gate_spec = pl.BlockSpec((None, None, 1, d), lambda i, ts: (5, ts[i], 0, 0))
    grid_spec = pltpu.PrefetchScalarGridSpec(
        num_scalar_prefetch=1,
        grid=(n_tiles,),
        in_specs=[
            pl.BlockSpec((1, 1, 2 * tm), lambda i, ts: (i, 0, 0), memory_space=pltpu.SMEM),
            pl.BlockSpec((1, 1, 2 * tm), lambda i, ts: (jnp.minimum(i + 1, n_tiles - 1), 0, 0),
                         memory_space=pltpu.SMEM),
            _rows(tm, d),
            _rows(tm, wt.shape[1]),
            gate_spec,
            pl.BlockSpec(memory_space=pl.ANY),
        ],
        out_specs=out_specs,
        scratch_shapes=[pltpu.VMEM((2, 2 * tm) + ys.shape[1:], ys.dtype), pltpu.SemaphoreType.DMA((2,))],
    )
    return pl.pallas_call(
        functools.partial(_combine_kernel, n_a=n_a),
        grid_spec=grid_spec,
        out_shape=out_shape,
        compiler_params=_params(("arbitrary",)),
        name="moe_combine",
    )(meta["seq"], slots, slots, x, wt, mod, ys)


def _tile_meta(seq_lens, tm):
    seq, pos, ln = [], [], []
    for s, n in enumerate(seq_lens):
        assert n % tm == 0
        for k in range(n // tm):
            seq.append(s)
            pos.append(k * tm)
            ln.append(n)
    return {k: jnp.asarray(np.asarray(v, np.int32)) for k, v in (("seq", seq), ("pos", pos), ("len", ln))}


def _rope_tables(s_max):
    half = HEAD_DIM // 2
    inv = ROPE_THETA ** (-jnp.arange(half, dtype=F32) / half)
    ang = jnp.arange(s_max, dtype=F32)[:, None] * inv[None, :]
    cos, sin = jnp.cos(ang), jnp.sin(ang)
    return jnp.concatenate([cos, cos], axis=1), jnp.concatenate([-sin, sin], axis=1)


def _tile_slots(slot, tm):
    t = slot.shape[1]
    return slot.reshape(2, t // tm, tm).transpose(1, 0, 2).reshape(t // tm, 1, 2 * tm)


def _route_params(norm_g, rg_w, rg_b, re_w, re_b):
    d = rg_w.shape[0]
    wr = jnp.zeros((ROUTER_ROWS, d), F32).at[0:N_GROUPS].set(rg_w.T).at[8:8 + N_EXPERTS].set(re_w.T)
    br = jnp.zeros((ROUTER_ROWS,), F32).at[N_GROUPS:8].set(NEG_BIG).at[0:N_GROUPS].set(rg_b)
    br = br.at[8:8 + N_EXPERTS].set(re_b)
    return norm_g, wr.astype(BF16), jnp.broadcast_to(br[:, None], (ROUTER_ROWS, LANES))


def _moe(x, routed, metas, mod, layer, wg, wu, wd, cfg, split_rows=None):
    t, d = x.shape
    hp, ri, rw, cnt = routed

    rb = cfg["rb"]
    assert rb % ZERO_ROWS == 0
    expert = ri[0:2]
    rank = ri[2:4]
    counts = cnt[:, 0].astype(jnp.int32)
    nblk = (counts + rb - 1) // rb
    blk_end = jnp.cumsum(nblk)
    row_start = (blk_end - nblk) * rb
    ids = jnp.arange(N_EXPERTS, dtype=jnp.int32)
    start_of = jnp.sum(jnp.where(expert[:, :, None] == ids, row_start, 0), axis=-1)
    slot = start_of + rank
    n_blocks = -(-2 * t // rb) + N_EXPERTS
    n_used = blk_end[-1]
    blk = jnp.arange(n_blocks, dtype=jnp.int32)
    blk_expert = jnp.minimum(jnp.sum(blk[:, None] >= blk_end[None, :], axis=1), N_EXPERTS - 1).astype(jnp.int32)
    last_used = jnp.sum(jnp.where(blk == n_used - 1, blk_expert, 0))
    blk_expert = jnp.where(blk < n_used, blk_expert, last_used)
    onehot = blk_expert[:, None] == ids
    row_end_of = jnp.sum(jnp.where(onehot, row_start + counts, 0), axis=1)
    blk_valid = jnp.where(blk < n_used, jnp.clip(row_end_of - blk * rb, 0, rb), 0).astype(jnp.int32)
    pad_lo = jnp.concatenate([row_start + counts, (n_used * rb).reshape(1)])
    pad_hi = jnp.concatenate([blk_end * rb, jnp.full((1,), n_blocks * rb, jnp.int32)])

    ch = t // (cfg["dispatch_steps_per_expert"] * N_EXPERTS)
    xs, w_bf = _dispatch(hp, _tile_slots(slot, ch), pad_lo.astype(jnp.int32), pad_hi.astype(jnp.int32),
                         n_blocks * rb, ch, layer, (wg, wu, wd))
    ys = _expert_ffn(xs, blk_expert, blk_valid, *w_bf, rb)

    tmc = cfg["tm_combine"]
    wt = rw.T
    return _combine(x, ys, _tile_slots(slot, tmc), wt, metas[tmc], mod, tmc, split_rows)


def _config(seq_lens):
    g = int(np.gcd.reduce(np.asarray(seq_lens)))
    tm = min(512, g)
    t = int(np.sum(seq_lens))
    per_expert = max(1, t // (N_EXPERTS * 640))
    assert t % (per_expert * N_EXPERTS * SUBLANES) == 0
    return {"tm": tm, "tm_combine": min(512, g), "tm_gmlp": min(512, g), "rb": 512,
            "dispatch_steps_per_expert": per_expert}


def _forward(x_prompt, x_sample, c_prompt, c_sample, ada_w, ada_b, norm_mix_g, norm_ffn_g,
             ab_w_in, q_norm_g, k_norm_g, attn_sink, pool_w, pool_scale, ab_w_out,
             c_w_in, sgu_ln_g, sgu_ln_b, sgu_w, sgu_b, c_w_out,
             router_group_w, router_group_b, router_expert_w, router_expert_b,
             exp_w_gate, exp_w_up, exp_w_down, cfg=None):
    bp, sp, d = x_prompt.shape
    bs, ss, _ = x_sample.shape
    seq_lens = [sp] * bp + [ss] * bs
    n_seq = len(seq_lens)
    if cfg is None:
        cfg = _config(seq_lens)
    depth = ada_w.shape[0]
    x = (x_prompt.reshape(bp * sp, d), x_sample.reshape(bs * ss, d))
    metas ={tm: _tile_meta(seq_lens, tm) for tm in
             {cfg["tm"], cfg["tm_combine"], cfg["tm_gmlp"]}}

    n_pad = -(-n_seq // 8) * 8
    c_pad = jnp.zeros((n_pad, d), F32).at[0:n_seq].set(jnp.concatenate([c_prompt, c_sample], axis=0))
    mod_all = _ada_mod(c_pad, ada_w, ada_b)
    cos_t, sin_t = _rope_tables(max(seq_lens))

    for l in range(depth):
        mod = mod_all[l, 0:n_seq].reshape(n_seq, 6, d).transpose(1, 0, 2).reshape(6, n_seq, 1, d)
        i = l // 2
        route = _route_params(norm_ffn_g[l], router_group_w[l], router_group_b[l],
                              router_expert_w[l], router_expert_b[l])
        if l % 2 == 0:
            tm = cfg["tm"]
            q, k, v, p = _inproj(x, metas[tm], mod, norm_mix_g[l], ab_w_in[i].astype(BF16),
                                 q_norm_g[i], k_norm_g[i], cos_t, sin_t, tm)
            sink_b = jnp.broadcast_to(
                jnp.repeat(attn_sink[i].reshape(N_KV_HEADS, GQA_GROUP), ATT_BLOCK, axis=1)[:, :, None],
                (N_KV_HEADS, GQA_GROUP * ATT_BLOCK, LANES)).astype(F32)
            a = _attention(q, k, v, sink_b, metas[tm], tm)
            x, *routed = _mix0(x, a, p, metas[tm], mod, pool_w[i].astype(BF16), pool_scale[i],
                               ab_w_out[i].astype(BF16), route, tm)
        else:
            tm = cfg["tm_gmlp"]
            if isinstance(x, tuple):
                x = jnp.concatenate(x, axis=0)
            u, v = _gmlp_in(x,metas[tm], mod, norm_mix_g[l], c_w_in[i].astype(BF16),
                            sgu_ln_g[i], sgu_ln_b[i], tm)
            sgu_b_b = jnp.broadcast_to(sgu_b[i][:, :, None], (SGU_HEADS, CHUNK, LANES)).astype(F32)
            x, *routed = _gmlp_out(x, u, v, metas[tm], mod, sgu_w[i].astype(BF16), sgu_b_b,
                                   c_w_out[i].astype(BF16), route, tm)
        x = _moe(x, routed, metas, mod, l, exp_w_gate, exp_w_up, exp_w_down, cfg,
                 split_rows=bp * sp if l == depth - 1 else None)

    y_prompt, y_sample = x
    return (y_prompt.reshape(bp, sp, d), y_sample.reshape(bs, ss, d))


def kernel(x_prompt, x_sample, c_prompt, c_sample, ada_w, ada_b, norm_mix_g, norm_ffn_g, ab_w_in, q_norm_g,
           k_norm_g, attn_sink, pool_w, pool_scale, ab_w_out, c_w_in, sgu_ln_g, sgu_ln_b, sgu_w, sgu_b,
           c_w_out, router_group_w, router_group_b, router_expert_w, router_expert_b, exp_w_gate, exp_w_up,
           exp_w_down):
    return _forward(x_prompt, x_sample, c_prompt, c_sample, ada_w, ada_b, norm_mix_g, norm_ffn_g, ab_w_in,
                    q_norm_g, k_norm_g, attn_sink, pool_w, pool_scale, ab_w_out, c_w_in, sgu_ln_g, sgu_ln_b,
                    sgu_w, sgu_b, c_w_out, router_group_w, router_group_b, router_expert_w, router_expert_b,
                    exp_w_gate, exp_w_up, exp_w_down)
```

```python
import functools

import numpy as np
import jax
import jax.numpy as jnp
from jax import lax
from jax.experimental import pallas as pl
from jax.experimental.pallas import tpu as pltpu

HEAD_DIM = 128
N_HEADS = 8
N_KV_HEADS = 2
GQA_GROUP = N_HEADS // N_KV_HEADS
ATT_BLOCK = 128
ROPE_THETA = 10000.0
Q_W = N_HEADS * HEAD_DIM
KV_W = N_KV_HEADS * HEAD_DIM
POOL_WINDOWS = (2, 4, 8, 16)
POOL_HALO = 8
CHUNK = 128
SGU_HEADS = 8
N_GROUPS = 4
EXP_PER_GROUP = 8
N_EXPERTS = N_GROUPS * EXP_PER_GROUP
EPS = 1e-6

VMEM_LIMIT_BYTES = 56 * 1024 * 1024
LANES = 128
SUBLANES = 8

F32 = jnp.float32
BF16 = jnp.bfloat16
NEG_BIG = -1e30


def _params(sem):
    return pltpu.CompilerParams(dimension_semantics=sem, vmem_limit_bytes=VMEM_LIMIT_BYTES)


def _resident(shape):
    nd = len(shape)
    return pl.BlockSpec(shape, lambda *_: (0,) * nd, pipeline_mode=pl.Buffered(1))


def _rows(tm, width):
    return pl.BlockSpec((tm, width), lambda i, *_: (i, 0))


def _split_rows(tm, width, n_a):
    return [pl.BlockSpec((tm, width), lambda i, *_: (jnp.minimum(i, n_a - 1), 0)),
            pl.BlockSpec((tm, width), lambda i, *_: (jnp.maximum(i - n_a, 0), 0))]


def _as_pair(x, tm):
    if isinstance(x, tuple):
        return x[0], x[1], x[0].shape[0] // tm
    return x, x, x.shape[0] // tm


def _mod_spec(part, d, seq_arg=0):
    return pl.BlockSpec((None, None, 1, d), lambda i, *pf: (part, pf[seq_arg][i], 0, 0))


def _cast_job(weights, layer, steps):
    ops, ins, outs, shapes = [], [], [], []
    for m in weights:
        n_l, e, r, c = m.shape
        rows = e * r
        blk = -(-(-(-rows // steps)) // 16) * 16
        last = -(-rows // blk) - 1
        ops.append(m.reshape(n_l, rows, c))
        ins.append(pl.BlockSpec((None, blk, c), lambda i, *_, last=last: (layer, jnp.minimum(i, last), 0)))
        outs.append(pl.BlockSpec((blk, c), lambda i, *_, last=last: (jnp.minimum(i, last), 0)))
        shapes.append(jax.ShapeDtypeStruct((rows, c), BF16))
    return ops, ins, outs, shapes


def _cast_tiles(w_in, w_out):
    for src, dst in zip(w_in, w_out):
        dst[...] = src[...].astype(BF16)


def _norm_mod(x, g, sc, sh):
    r = lax.rsqrt(jnp.mean(x * x, axis=-1, keepdims=True) + EPS)
    return x * r * g * (1.0 + sc) + sh


def _dot(a, b):
    return jnp.dot(a, b, preferred_element_type=F32)


def _dot_nt(a, b):
    return lax.dot_general(a, b, (((1,), (1,)), ((), ())), preferred_element_type=F32)


def _ada_kernel(c_ref, w_ref, b_ref, o_ref):
    c = c_ref[...]
    cs = c * (1.0 / (1.0 + jnp.exp(-c)))
    o_ref[...] = _dot(cs.astype(BF16), w_ref[...].astype(BF16)) + b_ref[...]


def _ada_mod(c_pad, ada_w, ada_b):
    depth, d, n = ada_w.shape
    tn = 1024
    return pl.pallas_call(
        _ada_kernel,
        grid=(depth, n // tn),
        in_specs=[
            pl.BlockSpec(c_pad.shape, lambda l, j: (0, 0)),
            pl.BlockSpec((None, d, tn), lambda l, j: (l, 0, j)),
            pl.BlockSpec((None, 1, tn), lambda l, j: (l, 0, j)),
        ],
        out_specs=pl.BlockSpec((None, c_pad.shape[0], tn), lambda l, j: (l, 0, j)),
        out_shape=jax.ShapeDtypeStruct((depth, c_pad.shape[0], n), F32),
        compiler_params=_params(("arbitrary", "arbitrary")),
        name="ada_mod",
    )(c_pad, ada_w, ada_b.reshape(depth, 1, n))


def _inproj_kernel(ts, tp, xa_ref, xb_ref, g_ref, sc_ref, sh_ref, w_ref, qg_ref, kg_ref, cos_ref, sin_ref,
                   q_ref, k_ref, v_ref, p_ref, *, n_a):
    x = jnp.where(pl.program_id(0) < n_a, xa_ref[...], xb_ref[...])
    h = _norm_mod(x, g_ref[...], sc_ref[...], sh_ref[...]).astype(BF16)
    cos = cos_ref[...]
    sin = sin_ref[...]

    def head_norm_rope(y, gain):
        r = lax.rsqrt(jnp.mean(y * y, axis=-1, keepdims=True) + EPS)
        y = y * r * gain
        return y * cos + pltpu.roll(y, HEAD_DIM // 2, 1) * sin

    q = _dot(h, w_ref[:, 0:Q_W])
    for hh in range(N_HEADS):
        sl = slice(hh * HEAD_DIM, (hh + 1) * HEAD_DIM)
        q_ref[:, sl] = head_norm_rope(q[:, sl], qg_ref[...]).astype(BF16)
    kv = _dot(h, w_ref[:, Q_W:Q_W + 2 * KV_W])
    for hh in range(N_KV_HEADS):
        sl = slice(hh * HEAD_DIM, (hh + 1) * HEAD_DIM)
        k_ref[:, sl] = head_norm_rope(kv[:, sl], kg_ref[...]).astype(BF16)
    v_ref[...] = kv[:, KV_W:].astype(BF16)
    p_ref[...] = _dot(h, w_ref[:, Q_W + 2 * KV_W:])


def _inproj(x, meta, mod, norm_g, w_in_bf, q_g, k_g, cos_t, sin_t, tm):
    xa, xb, n_a = _as_pair(x, tm)
    d = xa.shape[1]
    n_tiles = meta["seq"].shape[0]
    t = n_tiles * tm
    pool_w = w_in_bf.shape[1] - Q_W - 2 * KV_W
    rope_spec = pl.BlockSpec((tm, HEAD_DIM), lambda i, ts, tp: (tp[i] // tm, 0))
    grid_spec = pltpu.PrefetchScalarGridSpec(
        num_scalar_prefetch=2,
        grid=(n_tiles,),
        in_specs=_split_rows(tm, d, n_a) + [
            _resident((1, d)),
            _mod_spec(1, d),
            _mod_spec(0, d),
            _resident(w_in_bf.shape),
            _resident((1, HEAD_DIM)),
            _resident((1, HEAD_DIM)),
            rope_spec,
            rope_spec,
        ],
        out_specs=[_rows(tm, Q_W), _rows(tm, KV_W), _rows(tm, KV_W), _rows(tm, pool_w)],
    )
    return pl.pallas_call(
        functools.partial(_inproj_kernel, n_a=n_a),
        grid_spec=grid_spec,
        out_shape=[
            jax.ShapeDtypeStruct((t, Q_W), BF16),
            jax.ShapeDtypeStruct((t, KV_W), BF16),
            jax.ShapeDtypeStruct((t, KV_W), BF16),
            jax.ShapeDtypeStruct((t, pool_w), F32),
        ],
        compiler_params=_params(("arbitrary",)),
        name="attn_pool_inproj",
    )(meta["seq"], meta["pos"], xa, xb, norm_g.reshape(1, d), mod, mod, w_in_bf,
      q_g.reshape(1, HEAD_DIM), k_g.reshape(1, HEAD_DIM), cos_t, sin_t)


def _attn_kernel(tp, tl, q_ref, kc_ref, kp_ref, kn_ref, vc_ref, vp_ref, vn_ref, sink_ref, *rest):
    n_w = (len(rest) - 6) // 2
    w_in, o_ref, w_out = rest[:n_w], rest[n_w], rest[n_w + 1:2 * n_w + 1]
    kx, vx, s_scr, p_scr, r_scr = rest[-5:]
    _cast_tiles(w_in, w_out)
    i = pl.program_id(0)
    tm = q_ref.shape[0]
    nb = tm // ATT_BLOCK
    first = tp[i] == 0
    last = tp[i] + tm == tl[i]
    kx[0:ATT_BLOCK] = kp_ref[...]
    kx[ATT_BLOCK:ATT_BLOCK + tm] = kc_ref[...]
    kx[ATT_BLOCK + tm:] = kn_ref[...]
    vx[0:ATT_BLOCK] = vp_ref[...]
    vx[ATT_BLOCK:ATT_BLOCK + tm] = vc_ref[...]
    vx[ATT_BLOCK + tm:] = vn_ref[...]
    win = 3 * ATT_BLOCK
    qi = lax.broadcasted_iota(jnp.int32, (ATT_BLOCK, win), 0)
    kj = lax.broadcasted_iota(jnp.int32, (ATT_BLOCK, win), 1)
    band = (kj >= qi) & (kj <= qi + 2 * ATT_BLOCK)
    scale = HEAD_DIM ** -0.5
    units = [(b, kk) for b in range(nb) for kk in range(N_KV_HEADS)]

    def heads_of(kk):
        return [kk * GQA_GROUP + g for g in range(GQA_GROUP)]

    for n, (b, kk) in enumerate(units):
        rows = slice(b * ATT_BLOCK, (b + 1) * ATT_BLOCK)
        kw = kx[b * ATT_BLOCK:b * ATT_BLOCK + win, kk * HEAD_DIM:(kk + 1) * HEAD_DIM]
        qs = jnp.concatenate([q_ref[rows, hd * HEAD_DIM:(hd + 1) * HEAD_DIM] for hd in heads_of(kk)], axis=0)
        s_scr[n] = _dot_nt(qs, kw)
    for n, (b, kk) in enumerate(units):
        valid = band
        if b == 0:
            valid = valid & (kj >= jnp.where(first, ATT_BLOCK, 0))
        if b == nb - 1:
            valid = valid & (kj < jnp.where(last, 2 * ATT_BLOCK, win))
        bias = jnp.where(valid, 0.0, -jnp.inf).astype(F32)
        s = s_scr[n] * scale
        s = (s.reshape(GQA_GROUP, ATT_BLOCK, win) + bias[None]).reshape(GQA_GROUP * ATT_BLOCK, win)
        sk = sink_ref[kk][:, 0:1]
        m = jnp.maximum(jnp.max(s, axis=-1, keepdims=True), sk)
        p = jnp.exp(s - m)
        denom = jnp.sum(p, axis=-1, keepdims=True) + jnp.exp(sk - m)
        p_scr[n] = p.astype(BF16)
        r_scr[n] = jnp.broadcast_to(1.0 / denom, r_scr.shape[1:])
    for n, (b, kk) in enumerate(units):
        rows = slice(b * ATT_BLOCK, (b + 1) * ATT_BLOCK)
        vw = vx[b * ATT_BLOCK:b * ATT_BLOCK + win, kk * HEAD_DIM:(kk + 1) * HEAD_DIM]
        o = _dot(p_scr[n], vw) * r_scr[n]
        for g, hd in enumerate(heads_of(kk)):
            o_ref[rows, hd * HEAD_DIM:(hd + 1) * HEAD_DIM] = o[g * ATT_BLOCK:(g + 1) * ATT_BLOCK].astype(BF16)


def _attention(q, k, v, sink_b, meta, tm, cast):
    t = q.shape[0]
    n_tiles = t // tm
    c_ops, c_in, c_out, c_shapes = _cast_job(cast[1], cast[0], n_tiles)
    r = tm // ATT_BLOCK
    n_blk = t // ATT_BLOCK
    cur = pl.BlockSpec((tm, KV_W), lambda i, *_: (i, 0))
    prev = pl.BlockSpec((ATT_BLOCK, KV_W), lambda i, *_: (jnp.maximum(i * r - 1, 0), 0))
    nxt = pl.BlockSpec((ATT_BLOCK, KV_W), lambda i, *_: (jnp.minimum((i + 1) * r, n_blk - 1), 0))
    grid_spec = pltpu.PrefetchScalarGridSpec(
        num_scalar_prefetch=2,
        grid=(n_tiles,),
        in_specs=[_rows(tm, Q_W), cur, prev, nxt, cur, prev, nxt, _resident(sink_b.shape)] + c_in,
        out_specs=[_rows(tm, Q_W)] + c_out,
        scratch_shapes=[pltpu.VMEM((tm + 2 * ATT_BLOCK, KV_W), BF16),
                        pltpu.VMEM((tm + 2 * ATT_BLOCK, KV_W), BF16),
                        pltpu.VMEM((r * N_KV_HEADS, GQA_GROUP * ATT_BLOCK, 3 * ATT_BLOCK), F32),
                        pltpu.VMEM((r * N_KV_HEADS, GQA_GROUP * ATT_BLOCK, 3 * ATT_BLOCK), BF16),
                        pltpu.VMEM((r * N_KV_HEADS, GQA_GROUP * ATT_BLOCK, HEAD_DIM), F32)],
    )
    a, *w_bf = pl.pallas_call(
        _attn_kernel,
        grid_spec=grid_spec,
        out_shape=[jax.ShapeDtypeStruct((t, Q_W), BF16)] + c_shapes,
        compiler_params=_params(("arbitrary",)),
        name="banded_attention",
    )(meta["pos"], meta["len"], q, k, k, k, v, v, v, sink_b, *c_ops)
    return a, [w.reshape(m.shape[1:]) for w, m in zip(w_bf, cast[1])]


def _mix0_kernel(tp, tl, ts, xa_ref, xb_ref, a_ref, pc_ref, pp_ref, pn_ref, gate_ref, pw_ref, ps_ref, wo_ref,
                 *rest, n_a):
    route_in, o_ref = rest[:N_ROUTE_IN], rest[N_ROUTE_IN]
    route_out, (pext, carry) = rest[N_ROUTE_IN + 1:N_ROUTE_IN + 1 + N_ROUTE_OUT], rest[-2:]
    i = pl.program_id(0)
    tm = xa_ref.shape[0]
    first = tp[i] == 0
    last = tp[i] + tm == tl[i]
    pext[0:POOL_HALO] = jnp.where(first, 0.0, pp_ref[...])
    pext[POOL_HALO:POOL_HALO + tm] = pc_ref[...]
    pext[POOL_HALO + tm:] = jnp.where(last, 0.0, pn_ref[...])
    n_g = len(POOL_WINDOWS)
    gw = pc_ref.shape[1] // n_g
    pos = tp[i] + lax.broadcasted_iota(jnp.int32, (tm, gw), 0)
    seq_len = tl[i]
    ms = []
    for g, w in enumerate(POOL_WINDOWS):
        cols = slice(g * gw, (g + 1) * gw)
        acc = pext[POOL_HALO - w // 2:POOL_HALO - w // 2 + tm, cols]
        for off in range(-w // 2 + 1, w // 2):
            acc = acc + pext[POOL_HALO + off:POOL_HALO + off + tm, cols]
        cnt = (jnp.minimum(pos + w // 2, seq_len) - jnp.maximum(pos - w // 2, 0)).astype(F32)
        dlt = acc / cnt - pc_ref[:, cols]
        ms.append((_dot(dlt.astype(BF16), pw_ref[g]) * ps_ref[:, cols]).astype(BF16))
    m = jnp.concatenate(ms, axis=1)
    w_a = a_ref.shape[1]
    mix = _dot(a_ref[...], wo_ref[0:w_a, :]) + _dot(m, wo_ref[w_a:, :])
    x = jnp.where(i < n_a, xa_ref[...], xb_ref[...])
    x_new = x + gate_ref[...] * mix
    o_ref[...] = x_new
    _route_tile(x_new, *route_in, *route_out, carry)


def _mix0(x, a, p, meta, mod, pool_w_bf, pool_scale, w_out_bf, route, tm):
    xa, xb, n_a = _as_pair(x, tm)
    d = xa.shape[1]
    t, pw = p.shape
    n_tiles = t // tm
    r = tm // POOL_HALO
    n_hb = t // POOL_HALO
    prev = pl.BlockSpec((POOL_HALO, pw), lambda i, *_: (jnp.maximum(i * r - 1, 0), 0))
    nxt = pl.BlockSpec((POOL_HALO, pw), lambda i, *_: (jnp.minimum((i + 1) * r, n_hb - 1), 0))
    r_in, r_out, r_shapes, r_scratch = _route_specs(tm, t, d, seq_arg=2)
    grid_spec = pltpu.PrefetchScalarGridSpec(
        num_scalar_prefetch=3,
        grid=(n_tiles,),
        in_specs=_split_rows(tm, d, n_a) + [
            _rows(tm, a.shape[1]), _rows(tm, pw), prev, nxt, _mod_spec(2, d, seq_arg=2),
            _resident(pool_w_bf.shape), _resident((1, pw)), _resident(w_out_bf.shape)] + r_in,
        out_specs=[_rows(tm, d)] + r_out,
        scratch_shapes=[pltpu.VMEM((tm + 2 * POOL_HALO, pw), F32), r_scratch],
    )
    return pl.pallas_call(
        functools.partial(_mix0_kernel, n_a=n_a),
        grid_spec=grid_spec,
        out_shape=[jax.ShapeDtypeStruct((t, d), F32)] + r_shapes,
        compiler_params=_params(("arbitrary",)),
        name="pool_outproj_route",
    )(meta["pos"], meta["len"], meta["seq"], xa, xb, a, p, p, p, mod, pool_w_bf,
      pool_scale.reshape(1, pw), w_out_bf, *_route_operands(route, mod, d))


def _gelu(z):
    return 0.5 * z * (1.0 + lax.erf(z * np.float32(np.sqrt(0.5))))


def _gmlp_in_kernel(ts, x_ref, g_ref, sc_ref, sh_ref, w_ref, lg_ref, lb_ref, u_ref, v_ref):
    h = _norm_mod(x_ref[...], g_ref[...], sc_ref[...], sh_ref[...]).astype(BF16)
    half = u_ref.shape[1]
    u_ref[...] = _gelu(_dot(h, w_ref[:, 0:half]))
    zv = _gelu(_dot(h, w_ref[:, half:]))
    zc = zv - jnp.mean(zv, axis=-1, keepdims=True)
    r = lax.rsqrt(jnp.mean(zc * zc, axis=-1, keepdims=True) + EPS)
    v_ref[...] = (zc * r * lg_ref[...] + lb_ref[...]).astype(BF16)


def _gmlp_in(x, meta, mod, norm_g, w_in_bf, ln_g, ln_b, tm):
    t, d = x.shape
    half = w_in_bf.shape[1] // 2
    grid_spec = pltpu.PrefetchScalarGridSpec(
        num_scalar_prefetch=1,
        grid=(t // tm,),
        in_specs=[_rows(tm, d), _resident((1, d)), _mod_spec(1, d), _mod_spec(0, d),
                  _resident(w_in_bf.shape), _resident((1, half)), _resident((1, half))],
        out_specs=[_rows(tm, half), _rows(tm, half)],
    )
    return pl.pallas_call(
        _gmlp_in_kernel,
        grid_spec=grid_spec,
        out_shape=[jax.ShapeDtypeStruct((t, half), F32), jax.ShapeDtypeStruct((t, half), BF16)],
        compiler_params=_params(("arbitrary",)),
        name="gmlp_in",
    )(meta["seq"], x, norm_g.reshape(1, d), mod, mod, w_in_bf, ln_g.reshape(1, half), ln_b.reshape(1, half))


def _gmlp_out_kernel(ts, x_ref, u_ref, v_ref, gate_ref, sw_ref, sb_ref, wo_ref, *rest):
    route_in, o_ref = rest[:N_ROUTE_IN], rest[N_ROUTE_IN]
    route_out, (gated, carry) = rest[N_ROUTE_IN + 1:N_ROUTE_IN + 1 + N_ROUTE_OUT], rest[-2:]
    tm = x_ref.shape[0]
    hd = u_ref.shape[1] // SGU_HEADS
    for c in range(tm // CHUNK):
        rows = slice(c * CHUNK, (c + 1) * CHUNK)
        for hh in range(SGU_HEADS):
            cols = slice(hh * hd, (hh + 1) * hd)
            s = _dot(sw_ref[hh], v_ref[rows, cols]) + jnp.tile(sb_ref[hh], (1, hd // LANES))
            gated[rows, cols] = (u_ref[rows, cols] * s).astype(BF16)
    x_new = x_ref[...] + gate_ref[...] * _dot(gated[...], wo_ref[...])
    o_ref[...] = x_new
    _route_tile(x_new, *route_in, *route_out, carry)


def _gmlp_out(x, u, v, meta, mod, sgu_w_bf, sgu_b_b, w_out_bf, route, tm):
    t, d = x.shape
    w = u.shape[1]
    r_in, r_out, r_shapes, r_scratch = _route_specs(tm, t, d, seq_arg=0)
    grid_spec = pltpu.PrefetchScalarGridSpec(
        num_scalar_prefetch=1,
        grid=(t // tm,),
        in_specs=[_rows(tm, d), _rows(tm, w), _rows(tm, w), _mod_spec(2, d),
                  _resident(sgu_w_bf.shape), _resident(sgu_b_b.shape), _resident(w_out_bf.shape)] + r_in,
        out_specs=[_rows(tm, d)] + r_out,
        scratch_shapes=[pltpu.VMEM((tm, w), BF16), r_scratch],
    )
    return pl.pallas_call(
        _gmlp_out_kernel,
        grid_spec=grid_spec,
        out_shape=[jax.ShapeDtypeStruct((t, d), F32)] + r_shapes,
        compiler_params=_params(("arbitrary",)),
        name="gmlp_out_route",
    )(meta["seq"], x, u, v, mod, sgu_w_bf, sgu_b_b, w_out_bf, *_route_operands(route, mod, d))


ROUTER_ROWS = 48


def _route_tile(x, g_ref, sc_ref, sh_ref, wr_ref, br_ref, h_ref, ri_ref, rw_ref, cnt_ref, carry):
    i = pl.program_id(0)

    @pl.when(i == 0)
    def _():
        carry[...] = jnp.zeros_like(carry)

    tm = x.shape[0]
    hb = _norm_mod(x, g_ref[...], sc_ref[...], sh_ref[...]).astype(BF16)
    h_ref[...] = _rows_to_tiles(_pack_bf16_pair(hb))
    lg = _dot_nt(wr_ref[...], hb) + jnp.tile(br_ref[...], (1, tm // LANES))
    rows8 = lax.broadcasted_iota(jnp.int32, (EXP_PER_GROUP, tm), 0)

    def first_argmax(vals, vmax):
        return jnp.min(jnp.where(vals == vmax, rows8, EXP_PER_GROUP), axis=0, keepdims=True)

    gl = lg[0:8]
    gmax = jnp.max(gl, axis=0, keepdims=True)
    gidx = first_argmax(gl, gmax)
    g_w = 1.0 / jnp.sum(jnp.exp(gl - gmax), axis=0, keepdims=True)
    esel = jnp.zeros((EXP_PER_GROUP, tm), F32)
    for g in range(N_GROUPS):
        esel = jnp.where(gidx == g, lg[8 + g * EXP_PER_GROUP:8 + (g + 1) * EXP_PER_GROUP], esel)
    emax = jnp.max(esel, axis=0, keepdims=True)
    pe = jnp.exp(esel - emax)
    prob = pe / jnp.sum(pe, axis=0, keepdims=True)
    p1 = jnp.max(prob, axis=0, keepdims=True)
    i1 = first_argmax(prob, p1)
    rest = jnp.where(rows8 == i1, -1.0, prob)
    p2 = jnp.max(rest, axis=0, keepdims=True)
    i2 = first_argmax(rest, p2)
    den = p1 + p2
    w0 = g_w * (p1 / den)
    w1 = g_w * (p2 / den)
    e0 = gidx * EXP_PER_GROUP + i1
    e1 = gidx * EXP_PER_GROUP + i2

    rows_e = lax.broadcasted_iota(jnp.int32, (N_EXPERTS, tm), 0)
    oh0 = rows_e == e0
    oh1 = rows_e == e1
    both = jnp.where(oh0, 1.0, 0.0) + jnp.where(oh1, 1.0, 0.0)
    ri = lax.broadcasted_iota(jnp.int32, (tm, tm), 0)
    ci = lax.broadcasted_iota(jnp.int32, (tm, tm), 1)
    upper = jnp.where(ri < ci, 1.0, 0.0).astype(BF16)
    before = _dot(both.astype(BF16), upper) + carry[:, 0:1]
    r0 = jnp.sum(jnp.where(oh0, before, 0.0), axis=0, keepdims=True)
    r1 = jnp.sum(jnp.where(oh1, before, 0.0), axis=0, keepdims=True)
    new_cnt = carry[...] + jnp.sum(both, axis=1, keepdims=True)
    carry[...] = new_cnt
    cnt_ref[...] = new_cnt
    zi = jnp.zeros((4, tm), jnp.int32)
    ri_ref[...] = jnp.concatenate([e0, e1, r0.astype(jnp.int32), r1.astype(jnp.int32), zi], axis=0)
    rw_ref[...] = jnp.concatenate([w0, w1, jnp.zeros((6, tm), F32)], axis=0)


N_ROUTE_IN, N_ROUTE_OUT = 5, 4


def _route_specs(tm, t, d, seq_arg):
    in_specs = [_resident((1, d)), _mod_spec(4, d, seq_arg), _mod_spec(3, d, seq_arg),
                _resident((ROUTER_ROWS, d)), _resident((ROUTER_ROWS, LANES))]
    assert d // 2 == SUBLANES * LANES
    out_specs = [pl.BlockSpec((tm, SUBLANES, LANES), lambda i, *_: (i, 0, 0)),
                 pl.BlockSpec((8, tm), lambda i, *_: (0, i)),
                 pl.BlockSpec((8, tm), lambda i, *_: (0, i)),
                 pl.BlockSpec((N_EXPERTS, LANES), lambda i, *_: (0, 0))]
    out_shapes = [jax.ShapeDtypeStruct((t, SUBLANES, LANES), jnp.uint32),
                  jax.ShapeDtypeStruct((8, t), jnp.int32),
                  jax.ShapeDtypeStruct((8, t), F32),
                  jax.ShapeDtypeStruct((N_EXPERTS, LANES), F32)]
    return in_specs, out_specs, out_shapes, pltpu.VMEM((N_EXPERTS, LANES), F32)


def _route_operands(route, mod, d):
    norm_g, wr_bf, br_b = route
    return [norm_g.reshape(1, d), mod, mod, wr_bf, br_b]


def _row_copy(src, s_row, dst, d_row, sem):
    return pltpu.make_async_copy(src.at[s_row], dst.at[d_row], sem)


ZERO_ROWS = 32


def _dispatch_kernel(pad_lo, pad_hi, slot_ref, hp_ref, *rest):
    n_w = (len(rest) - 4) // 2
    w_in, xs_hbm, w_out = rest[:n_w], rest[n_w], rest[n_w + 1:2 * n_w + 1]
    zrow, sem, zsem = rest[-3:]
    i = pl.program_id(0)
    ch = slot_ref.shape[2] // 2
    zn = zrow.shape[0]

    @pl.when(i == 0)
    def _():
        zrow[...] = jnp.zeros_like(zrow)

        def per_range(e, c):
            lo, hi = pad_lo[e], pad_hi[e]
            lo_al = jnp.minimum(((lo + zn - 1) // zn) * zn, hi)

            def row(r):
                return _row_copy(zrow, 0, xs_hbm, r, zsem)

            def grp(g):
                return pltpu.make_async_copy(zrow, xs_hbm.at[pl.ds(g * zn, zn)], zsem)

            lax.fori_loop(lo, lo_al, lambda r, c2: (row(r).start(), c2)[1], 0)
            lax.fori_loop(lo_al // zn, hi // zn, lambda g, c2: (grp(g).start(), c2)[1], 0)
            lax.fori_loop(lo, lo_al, lambda r, c2: (row(r).wait(), c2)[1], 0)
            lax.fori_loop(lo_al // zn, hi // zn, lambda g, c2: (grp(g).wait(), c2)[1], 0)
            return c

        lax.fori_loop(0, pad_lo.shape[0], per_range, 0)

    def issue(q, c):
        for u in range(SUBLANES):
            r = q * SUBLANES + u
            for k in range(2):
                _row_copy(hp_ref, r, xs_hbm, slot_ref[0, 0, k * ch + r], sem).start()
        return c

    lax.fori_loop(0, ch // SUBLANES, issue, 0)
    _cast_tiles(w_in, w_out)
    for k in range(2):
        pltpu.make_async_copy(hp_ref, xs_hbm.at[pl.ds(0, ch)], sem).wait()


def _dispatch(hp, slots, pad_lo, pad_hi, n_rows, ch, cast=None):
    t = hp.shape[0]
    steps = t // ch
    weights = cast[1] if cast else ()
    w_ops, w_in_specs, w_out_specs, w_out_shapes = _cast_job(weights, cast[0] if cast else 0, steps)
    grid_spec = pltpu.PrefetchScalarGridSpec(
        num_scalar_prefetch=2,
        grid=(steps,),
        in_specs=[
            pl.BlockSpec((1, 1, 2 * ch), lambda i, *_: (i, 0, 0), memory_space=pltpu.SMEM),
            pl.BlockSpec((ch,) + hp.shape[1:], lambda i, *_: (i, 0, 0)),
        ] + w_in_specs,
        out_specs=[pl.BlockSpec(memory_space=pl.ANY)] + w_out_specs,
        scratch_shapes=[pltpu.VMEM((ZERO_ROWS,) + hp.shape[1:], hp.dtype), pltpu.SemaphoreType.DMA,
                        pltpu.SemaphoreType.DMA],
    )
    xs, *w_bf = pl.pallas_call(
        _dispatch_kernel,
        grid_spec=grid_spec,
        out_shape=[jax.ShapeDtypeStruct((n_rows,) + hp.shape[1:], hp.dtype)] + w_out_shapes,
        compiler_params=_params(("arbitrary",)),
        name="moe_dispatch",
    )(pad_lo, pad_hi, slots, hp, *w_ops)
    return xs, [w.reshape(m.shape[1:]) for w, m in zip(w_bf, weights)]


FFN_ROW_STEPS = 4


def _rows_to_tiles(x):
    n = x.shape[0]
    parts = [x[:, c * LANES:(c + 1) * LANES].reshape(n // SUBLANES, SUBLANES, LANES) for c in range(SUBLANES)]
    a = jnp.swapaxes(jnp.stack(parts, axis=1), 1, 2)
    return a.reshape(n, SUBLANES, LANES)


def _tiles_to_rows(a):
    n = a.shape[0]
    a = jnp.swapaxes(a.reshape(n // SUBLANES, SUBLANES, SUBLANES, LANES), 1, 2)
    return jnp.concatenate([a[:, c].reshape(n, LANES) for c in range(SUBLANES)], axis=1)


def _pack_bf16_pair(x):
    half = x.shape[1] // 2
    bits = pltpu.bitcast(x.astype(BF16).astype(F32), jnp.uint32)
    return (bits[:, :half] >> 16) | bits[:, half:]


def _unpack_bf16_pair(xp):
    return pltpu.bitcast(xp << 16, F32), pltpu.bitcast(xp & jnp.uint32(0xFFFF0000), F32)


def _ffn_kernel(be, nv, xs_ref, wg_ref, wu_ref, wd_ref, o_ref):
    i = pl.program_id(0)
    rb = o_ref.shape[0]
    step = rb // FFN_ROW_STEPS
    n = nv[i]

    def ffn_rows(rows):
        lo, hi = _unpack_bf16_pair(_tiles_to_rows(xs_ref[0:rows]))
        x = jnp.concatenate([lo.astype(BF16), hi.astype(BF16)], axis=1)
        hg = _dot(x, wg_ref[...])
        hu = _dot(x, wu_ref[...])
        act = (hg * (1.0 / (1.0 + jnp.exp(-hg))) * hu).astype(BF16)
        o_ref[0:rows] = _rows_to_tiles(_pack_bf16_pair(_dot(act, wd_ref[...])))
        if rows < rb:
            o_ref[rows:] = jnp.zeros((rb - rows,) + o_ref.shape[1:], o_ref.dtype)

    @pl.when(n == 0)
    def _():
        o_ref[...] = jnp.zeros_like(o_ref)

    for k in range(1, FFN_ROW_STEPS + 1):
        pl.when((n > (k - 1) * step) & (n <= k * step))(functools.partial(ffn_rows, k * step))


def _expert_ffn(xs, blk_expert, blk_valid, wg, wu, wd, rb):
    n_rows = xs.shape[0]
    _, d, ff = wg.shape
    row_block = pl.BlockSpec((rb,) + xs.shape[1:], lambda i, be, nv: (i, 0, 0))
    grid_spec = pltpu.PrefetchScalarGridSpec(
        num_scalar_prefetch=2,
        grid=(n_rows // rb,),
        in_specs=[
            row_block,
            pl.BlockSpec((None, d, ff), lambda i, be, nv: (be[i], 0, 0)),
            pl.BlockSpec((None, d, ff), lambda i, be, nv: (be[i], 0, 0)),
            pl.BlockSpec((None, ff, d), lambda i, be, nv: (be[i], 0, 0)),
        ],
        out_specs=row_block,
    )
    return pl.pallas_call(
        _ffn_kernel,
        grid_spec=grid_spec,
        out_shape=jax.ShapeDtypeStruct(xs.shape, xs.dtype),
        compiler_params=_params(("arbitrary",)),
        name="expert_ffn",
    )(blk_expert, blk_valid, xs, wg, wu, wd)


def _combine_kernel(ts, idx_ref, nxt_ref, x_ref, wt_ref, gate_ref, ys_hbm, *rest, n_a):
    outs, (buf, sem) = rest[:-2], rest[-2:]
    i = pl.program_id(0)
    tm = x_ref.shape[0]
    cur = i % 2

    def fetch(idx, b):
        def issue(q, c):
            for u in range(SUBLANES):
                r = q * SUBLANES + u
                _row_copy(ys_hbm, idx[0, 0, r], buf.at[b], r, sem.at[b]).start()
            return c

        lax.fori_loop(0, 2 * tm // SUBLANES, issue, 0)

    @pl.when(i == 0)
    def _():
        fetch(idx_ref, 0)

    @pl.when(i + 1 < pl.num_programs(0))
    def _():
        fetch(nxt_ref, 1 - cur)

    pltpu.make_async_copy(buf.at[cur], buf.at[cur], sem.at[cur]).wait()
    y0 = _unpack_bf16_pair(_tiles_to_rows(buf[cur, 0:tm]))
    y1 = _unpack_bf16_pair(_tiles_to_rows(buf[cur, tm:2 * tm]))
    w0, w1 = wt_ref[:, 0:1], wt_ref[:, 1:2]
    y = jnp.concatenate([w0 * y0[0] + w1 * y1[0], w0 * y0[1] + w1 * y1[1]], axis=1)
    res = x_ref[...] + gate_ref[...] * y
    if n_a is None:
        outs[0][...] = res
    else:
        @pl.when(i < n_a)
        def _():
            outs[0][...] = res

        @pl.when(i >= n_a)
        def _():
            outs[1][...] = res


def _combine(x, ys, slots, wt, meta, mod, tm, split_rows=None):
    t, d = x.shape
    n_tiles = t // tm
    if split_rows is None:
        n_a, out_specs = None, _rows(tm, d)
        out_shape = jax.ShapeDtypeStruct((t, d), F32)
    else:
        n_a, out_specs = split_rows // tm, _split_rows(tm, d, split_rows // tm)
        out_shape = [jax.ShapeDtypeStruct((split_rows, d), F32), jax.ShapeDtypeStruct((t - split_rows, d), F32)]
    gate_spec = pl.BlockSpec((None, None, 1, d), lambda i, ts: (5, ts[i], 0, 0))
    grid_spec = pltpu.PrefetchScalarGridSpec(
        num_scalar_prefetch=1,
        grid=(n_tiles,),
        in_specs=[
            pl.BlockSpec((1, 1, 2 * tm), lambda i, ts: (i, 0, 0), memory_space=pltpu.SMEM),
            pl.BlockSpec((1, 1, 2 * tm), lambda i, ts: (jnp.minimum(i + 1, n_tiles - 1), 0, 0),
                         memory_space=pltpu.SMEM),
            _rows(tm, d),
            _rows(tm, wt.shape[1]),
            gate_spec,
            pl.BlockSpec(memory_space=pl.ANY),
        ],
        out_specs=out_specs,
        scratch_shapes=[pltpu.VMEM((2, 2 * tm) + ys.shape[1:], ys.dtype), pltpu.SemaphoreType.DMA((2,))],
    )
    return pl.pallas_call(
        functools.partial(_combine_kernel, n_a=n_a),
        grid_spec=grid_spec,
        out_shape=out_shape,
        compiler_params=_params(("arbitrary",)),
        name="moe_combine",
    )(meta["seq"], slots, slots, x, wt, mod, ys)


def _tile_meta(seq_lens, tm):
    seq, pos, ln = [], [], []
    for s, n in enumerate(seq_lens):
        assert n % tm == 0
        for k in range(n // tm):
            seq.append(s)
            pos.append(k * tm)
            ln.append(n)
    return {k: jnp.asarray(np.asarray(v, np.int32)) for k, v in (("seq", seq), ("pos", pos), ("len", ln))}


def _rope_tables(s_max):
    half = HEAD_DIM // 2
    inv = ROPE_THETA ** (-jnp.arange(half, dtype=F32) / half)
    ang = jnp.arange(s_max, dtype=F32)[:, None] * inv[None, :]
    cos, sin = jnp.cos(ang), jnp.sin(ang)
    return jnp.concatenate([cos, cos], axis=1), jnp.concatenate([-sin, sin], axis=1)


def _tile_slots(slot, tm):
    t = slot.shape[1]
    return slot.reshape(2, t // tm, tm).transpose(1, 0, 2).reshape(t // tm, 1, 2 * tm)


def _route_params(norm_g, rg_w, rg_b, re_w, re_b):
    d = rg_w.shape[0]
    wr = jnp.zeros((ROUTER_ROWS, d), F32).at[0:N_GROUPS].set(rg_w.T).at[8:8 + N_EXPERTS].set(re_w.T)
    br = jnp.zeros((ROUTER_ROWS,), F32).at[N_GROUPS:8].set(NEG_BIG).at[0:N_GROUPS].set(rg_b)
    br = br.at[8:8 + N_EXPERTS].set(re_b)
    return norm_g, wr.astype(BF16), jnp.broadcast_to(br[:, None], (ROUTER_ROWS, LANES))


def _moe(x, routed, metas, mod, layer, weights, w_bf, cfg, split_rows=None):
    t, d = x.shape
    hp, ri, rw, cnt = routed

    rb = cfg["rb"]
    assert rb % ZERO_ROWS == 0
    expert = ri[0:2]
    rank = ri[2:4]
    counts = cnt[:, 0].astype(jnp.int32)
    nblk = (counts + rb - 1) // rb
    blk_end = jnp.cumsum(nblk)
    row_start = (blk_end - nblk) * rb
    ids = jnp.arange(N_EXPERTS, dtype=jnp.int32)
    start_of = jnp.sum(jnp.where(expert[:, :, None] == ids, row_start, 0), axis=-1)
    slot = start_of + rank
    n_blocks = -(-2 * t // rb) + N_EXPERTS
    n_used = blk_end[-1]
    blk = jnp.arange(n_blocks, dtype=jnp.int32)
    blk_expert = jnp.minimum(jnp.sum(blk[:, None] >= blk_end[None, :], axis=1), N_EXPERTS - 1).astype(jnp.int32)
    last_used = jnp.sum(jnp.where(blk == n_used - 1, blk_expert, 0))
    blk_expert = jnp.where(blk < n_used, blk_expert, last_used)
    onehot = blk_expert[:, None] == ids
    row_end_of = jnp.sum(jnp.where(onehot, row_start + counts, 0), axis=1)
    blk_valid = jnp.where(blk < n_used, jnp.clip(row_end_of - blk * rb, 0, rb), 0).astype(jnp.int32)
    pad_lo = jnp.concatenate([row_start + counts, (n_used * rb).reshape(1)])
    pad_hi = jnp.concatenate([blk_end * rb, jnp.full((1,), n_blocks * rb, jnp.int32)])

    ch = t // (cfg["dispatch_steps_per_expert"] * N_EXPERTS)
    xs, cast_here = _dispatch(hp, _tile_slots(slot, ch), pad_lo.astype(jnp.int32), pad_hi.astype(jnp.int32),
                              n_blocks * rb, ch, None if w_bf else (layer, weights))
    ys = _expert_ffn(xs, blk_expert, blk_valid, *(w_bf or cast_here), rb)

    tmc = cfg["tm_combine"]
    wt = rw.T
    return _combine(x, ys, _tile_slots(slot, tmc), wt, metas[tmc], mod, tmc, split_rows)


def _config(seq_lens):
    g = int(np.gcd.reduce(np.asarray(seq_lens)))
    tm = min(512, g)
    t = int(np.sum(seq_lens))
    per_expert = max(1, t // (N_EXPERTS * 640))
    assert t % (per_expert * N_EXPERTS * SUBLANES) == 0
    return {"tm": tm, "tm_combine": min(512, g), "tm_gmlp": min(512, g), "rb": 512,
            "dispatch_steps_per_expert": per_expert}


def _forward(x_prompt, x_sample, c_prompt, c_sample, ada_w, ada_b, norm_mix_g, norm_ffn_g,
             ab_w_in, q_norm_g, k_norm_g, attn_sink, pool_w, pool_scale, ab_w_out,
             c_w_in, sgu_ln_g, sgu_ln_b, sgu_w, sgu_b, c_w_out,
             router_group_w, router_group_b, router_expert_w, router_expert_b,
             exp_w_gate, exp_w_up, exp_w_down, cfg=None):
    bp, sp, d = x_prompt.shape
    bs, ss, _ = x_sample.shape
    seq_lens = [sp] * bp + [ss] * bs
    n_seq = len(seq_lens)
    if cfg is None:
        cfg = _config(seq_lens)
    depth = ada_w.shape[0]
    x = (x_prompt.reshape(bp * sp, d), x_sample.reshape(bs * ss, d))
    metas ={tm: _tile_meta(seq_lens, tm) for tm in
             {cfg["tm"], cfg["tm_combine"], cfg["tm_gmlp"]}}

    n_pad = -(-n_seq // 8) * 8
    c_pad = jnp.zeros((n_pad, d), F32).at[0:n_seq].set(jnp.concatenate([c_prompt, c_sample], axis=0))
    mod_all = _ada_mod(c_pad, ada_w, ada_b)
    cos_t, sin_t = _rope_tables(max(seq_lens))

    expert_w = (exp_w_gate, exp_w_up, exp_w_down)
    for l in range(depth):
        w_bf = None
        mod = mod_all[l, 0:n_seq].reshape(n_seq, 6, d).transpose(1, 0, 2).reshape(6, n_seq, 1, d)
        i = l // 2
        route = _route_params(norm_ffn_g[l], router_group_w[l], router_group_b[l],
                              router_expert_w[l], router_expert_b[l])
        if l % 2 == 0:
            tm = cfg["tm"]
            q, k, v, p = _inproj(x, metas[tm], mod, norm_mix_g[l], ab_w_in[i].astype(BF16),
                                 q_norm_g[i], k_norm_g[i], cos_t, sin_t, tm)
            sink_b = jnp.broadcast_to(
                jnp.repeat(attn_sink[i].reshape(N_KV_HEADS, GQA_GROUP), ATT_BLOCK, axis=1)[:, :, None],
                (N_KV_HEADS, GQA_GROUP * ATT_BLOCK, LANES)).astype(F32)
            a, w_bf = _attention(q, k, v, sink_b, metas[tm], tm, (l, expert_w))
            x, *routed = _mix0(x, a, p, metas[tm], mod, pool_w[i].astype(BF16), pool_scale[i],
                               ab_w_out[i].astype(BF16), route, tm)
        else:
            tm = cfg["tm_gmlp"]
            if isinstance(x, tuple):
                x = jnp.concatenate(x, axis=0)
            u, v = _gmlp_in(x,metas[tm], mod, norm_mix_g[l], c_w_in[i].astype(BF16),
                            sgu_ln_g[i], sgu_ln_b[i], tm)
            sgu_b_b = jnp.broadcast_to(sgu_b[i][:, :, None], (SGU_HEADS, CHUNK, LANES)).astype(F32)
            x, *routed = _gmlp_out(x, u, v, metas[tm], mod, sgu_w[i].astype(BF16), sgu_b_b,
                                   c_w_out[i].astype(BF16), route, tm)
        x = _moe(x, routed, metas, mod, l, expert_w, w_bf, cfg,
                 split_rows=bp * sp if l == depth - 1 else None)

    y_prompt, y_sample = x
    return (y_prompt.reshape(bp, sp, d), y_sample.reshape(bs, ss, d))


def kernel(x_prompt, x_sample, c_prompt, c_sample, ada_w, ada_b, norm_mix_g, norm_ffn_g, ab_w_in, q_norm_g,
           k_norm_g, attn_sink, pool_w, pool_scale, ab_w_out, c_w_in, sgu_ln_g, sgu_ln_b, sgu_w, sgu_b,
           c_w_out, router_group_w, router_group_b, router_expert_w, router_expert_b, exp_w_gate, exp_w_up,
           exp_w_down):
    return _forward(x_prompt, x_sample, c_prompt, c_sample, ada_w, ada_b, norm_mix_g, norm_ffn_g, ab_w_in,
                    q_norm_g, k_norm_g, attn_sink, pool_w, pool_scale, ab_w_out, c_w_in, sgu_ln_g, sgu_ln_b,
                    sgu_w, sgu_b, c_w_out, router_group_w, router_group_b, router_expert_w, router_expert_b,
                    exp_w_gate, exp_w_up, exp_w_down)
```

```python
import functools

import numpy as np
import jax
import jax.numpy as jnp
from jax import lax
from jax.experimental import pallas as pl
from jax.experimental.pallas import tpu as pltpu

HEAD_DIM = 128
N_HEADS = 8
N_KV_HEADS = 2
GQA_GROUP = N_HEADS // N_KV_HEADS
ATT_BLOCK = 128
ROPE_THETA = 10000.0
Q_W = N_HEADS * HEAD_DIM
KV_W = N_KV_HEADS * HEAD_DIM
POOL_WINDOWS = (2, 4, 8, 16)
POOL_HALO = 8
CHUNK = 128
SGU_HEADS = 8
N_GROUPS = 4
EXP_PER_GROUP = 8
N_EXPERTS = N_GROUPS * EXP_PER_GROUP
EPS = 1e-6

VMEM_LIMIT_BYTES = 56 * 1024 * 1024
LANES = 128
SUBLANES = 8

F32 = jnp.float32
BF16 = jnp.bfloat16
NEG_BIG = -1e30


def _params(sem):
    return pltpu.CompilerParams(dimension_semantics=sem, vmem_limit_bytes=VMEM_LIMIT_BYTES)


def _resident(shape):
    nd = len(shape)
    return pl.BlockSpec(shape, lambda *_: (0,) * nd, pipeline_mode=pl.Buffered(1))


def _rows(tm, width):
    return pl.BlockSpec((tm, width), lambda i, *_: (i, 0))


def _split_rows(tm, width, n_a):
    return [pl.BlockSpec((tm, width), lambda i, *_: (jnp.minimum(i, n_a - 1), 0)),
            pl.BlockSpec((tm, width), lambda i, *_: (jnp.maximum(i - n_a, 0), 0))]


def _as_pair(x, tm):
    if isinstance(x, tuple):
        return x[0], x[1], x[0].shape[0] // tm
    return x, x, x.shape[0] // tm


def _mod_spec(part, d, seq_arg=0):
    return pl.BlockSpec((None, None, 1, d), lambda i, *pf: (part, pf[seq_arg][i], 0, 0))


def _cast_job(weights, layer, steps):
    ops, ins, outs, shapes = [], [], [], []
    for m in weights:
        n_l, e, r, c = m.shape
        rows = e * r
        blk = -(-(-(-rows // steps)) // 16) * 16
        last = -(-rows // blk) - 1
        ops.append(m.reshape(n_l, rows, c))
        ins.append(pl.BlockSpec((None, blk, c), lambda i, *_, last=last: (layer, jnp.minimum(i, last), 0)))
        outs.append(pl.BlockSpec((blk, c), lambda i, *_, last=last: (jnp.minimum(i, last), 0)))
        shapes.append(jax.ShapeDtypeStruct((rows, c), BF16))
    return ops, ins, outs, shapes


def _cast_tiles(w_in, w_out):
    for src, dst in zip(w_in, w_out):
        dst[...] = src[...].astype(BF16)


def _norm_mod(x, g, sc, sh):
    r = lax.rsqrt(jnp.mean(x * x, axis=-1, keepdims=True) + EPS)
    return x * r * g * (1.0 + sc) + sh


def _dot(a, b):
    return jnp.dot(a, b, preferred_element_type=F32)


def _dot_nt(a, b):
    return lax.dot_general(a, b, (((1,), (1,)), ((), ())), preferred_element_type=F32)


def _ada_kernel(c_ref, w_ref, b_ref, o_ref):
    c = c_ref[...]
    cs = c * (1.0 / (1.0 + jnp.exp(-c)))
    o_ref[...] = _dot(cs.astype(BF16), w_ref[...].astype(BF16)) + b_ref[...]


def _ada_mod(c_pad, ada_w, ada_b):
    depth, d, n = ada_w.shape
    tn = 1024
    return pl.pallas_call(
        _ada_kernel,
        grid=(depth, n // tn),
        in_specs=[
            pl.BlockSpec(c_pad.shape, lambda l, j: (0, 0)),
            pl.BlockSpec((None, d, tn), lambda l, j: (l, 0, j)),
            pl.BlockSpec((None, 1, tn), lambda l, j: (l, 0, j)),
        ],
        out_specs=pl.BlockSpec((None, c_pad.shape[0], tn), lambda l, j: (l, 0, j)),
        out_shape=jax.ShapeDtypeStruct((depth, c_pad.shape[0], n), F32),
        compiler_params=_params(("arbitrary", "arbitrary")),
        name="ada_mod",
    )(c_pad, ada_w, ada_b.reshape(depth, 1, n))


def _inproj_kernel(ts, tp, xa_ref, xb_ref, g_ref, sc_ref, sh_ref, w_ref, qg_ref, kg_ref, cos_ref, sin_ref,
                   q_ref, k_ref, v_ref, p_ref, *, n_a):
    x = jnp.where(pl.program_id(0) < n_a, xa_ref[...], xb_ref[...])
    h = _norm_mod(x, g_ref[...], sc_ref[...], sh_ref[...]).astype(BF16)
    cos = cos_ref[...]
    sin = sin_ref[...]

    def head_norm_rope(y, gain):
        r = lax.rsqrt(jnp.mean(y * y, axis=-1, keepdims=True) + EPS)
        y = y * r * gain
        return y * cos + pltpu.roll(y, HEAD_DIM // 2, 1) * sin

    q = _dot(h, w_ref[:, 0:Q_W])
    for hh in range(N_HEADS):
        sl = slice(hh * HEAD_DIM, (hh + 1) * HEAD_DIM)
        q_ref[:, sl] = head_norm_rope(q[:, sl], qg_ref[...]).astype(BF16)
    kv = _dot(h, w_ref[:, Q_W:Q_W + 2 * KV_W])
    for hh in range(N_KV_HEADS):
        sl = slice(hh * HEAD_DIM, (hh + 1) * HEAD_DIM)
        k_ref[:, sl] = head_norm_rope(kv[:, sl], kg_ref[...]).astype(BF16)
    v_ref[...] = kv[:, KV_W:].astype(BF16)
    p_ref[...] = _dot(h, w_ref[:, Q_W + 2 * KV_W:])


def _inproj(x, meta, mod, norm_g, w_in_bf, q_g, k_g, cos_t, sin_t, tm):
    xa, xb, n_a = _as_pair(x, tm)
    d = xa.shape[1]
    n_tiles = meta["seq"].shape[0]
    t = n_tiles * tm
    pool_w = w_in_bf.shape[1] - Q_W - 2 * KV_W
    rope_spec = pl.BlockSpec((tm, HEAD_DIM), lambda i, ts, tp: (tp[i] // tm, 0))
    grid_spec = pltpu.PrefetchScalarGridSpec(
        num_scalar_prefetch=2,
        grid=(n_tiles,),
        in_specs=_split_rows(tm, d, n_a) + [
            _resident((1, d)),
            _mod_spec(1, d),
            _mod_spec(0, d),
            _resident(w_in_bf.shape),
            _resident((1, HEAD_DIM)),
            _resident((1, HEAD_DIM)),
            rope_spec,
            rope_spec,
        ],
        out_specs=[_rows(tm, Q_W), _rows(tm, KV_W), _rows(tm, KV_W), _rows(tm, pool_w)],
    )
    return pl.pallas_call(
        functools.partial(_inproj_kernel, n_a=n_a),
        grid_spec=grid_spec,
        out_shape=[
            jax.ShapeDtypeStruct((t, Q_W), BF16),
            jax.ShapeDtypeStruct((t, KV_W), BF16),
            jax.ShapeDtypeStruct((t, KV_W), BF16),
            jax.ShapeDtypeStruct((t, pool_w), F32),
        ],
        compiler_params=_params(("arbitrary",)),
        name="attn_pool_inproj",
    )(meta["seq"], meta["pos"], xa, xb, norm_g.reshape(1, d), mod, mod, w_in_bf,
      q_g.reshape(1, HEAD_DIM), k_g.reshape(1, HEAD_DIM), cos_t, sin_t)


def _attn_kernel(tp, tl, q_ref, kc_ref, kp_ref, kn_ref, vc_ref, vp_ref, vn_ref, sink_ref, *rest):
    n_w = (len(rest) - 6) // 2
    w_in, o_ref, w_out = rest[:n_w], rest[n_w], rest[n_w + 1:2 * n_w + 1]
    kx, vx, s_scr, p_scr, r_scr = rest[-5:]
    _cast_tiles(w_in, w_out)
    i = pl.program_id(0)
    tm = q_ref.shape[0]
    nb = tm // ATT_BLOCK
    first = tp[i] == 0
    last = tp[i] + tm == tl[i]
    kx[0:ATT_BLOCK] = kp_ref[...]
    kx[ATT_BLOCK:ATT_BLOCK + tm] = kc_ref[...]
    kx[ATT_BLOCK + tm:] = kn_ref[...]
    vx[0:ATT_BLOCK] = vp_ref[...]
    vx[ATT_BLOCK:ATT_BLOCK + tm] = vc_ref[...]
    vx[ATT_BLOCK + tm:] = vn_ref[...]
    win = 3 * ATT_BLOCK
    qi = lax.broadcasted_iota(jnp.int32, (ATT_BLOCK, win), 0)
    kj = lax.broadcasted_iota(jnp.int32, (ATT_BLOCK, win), 1)
    band = (kj >= qi) & (kj <= qi + 2 * ATT_BLOCK)
    scale = HEAD_DIM ** -0.5
    units = [(b, kk) for b in range(nb) for kk in range(N_KV_HEADS)]

    def heads_of(kk):
        return [kk * GQA_GROUP + g for g in range(GQA_GROUP)]

    for n, (b, kk) in enumerate(units):
        rows = slice(b * ATT_BLOCK, (b + 1) * ATT_BLOCK)
        kw = kx[b * ATT_BLOCK:b * ATT_BLOCK + win, kk * HEAD_DIM:(kk + 1) * HEAD_DIM]
        qs = jnp.concatenate([q_ref[rows, hd * HEAD_DIM:(hd + 1) * HEAD_DIM] for hd in heads_of(kk)], axis=0)
        s_scr[n] = _dot_nt(qs, kw)
    for n, (b, kk) in enumerate(units):
        valid = band
        if b == 0:
            valid = valid & (kj >= jnp.where(first, ATT_BLOCK, 0))
        if b == nb - 1:
            valid = valid & (kj < jnp.where(last, 2 * ATT_BLOCK, win))
        bias = jnp.where(valid, 0.0, -jnp.inf).astype(F32)
        s = s_scr[n] * scale
        s = (s.reshape(GQA_GROUP, ATT_BLOCK, win) + bias[None]).reshape(GQA_GROUP * ATT_BLOCK, win)
        sk = sink_ref[kk][:, 0:1]
        m = jnp.maximum(jnp.max(s, axis=-1, keepdims=True), sk)
        p = jnp.exp(s - m)
        denom = jnp.sum(p, axis=-1, keepdims=True) + jnp.exp(sk - m)
        p_scr[n] = p.astype(BF16)
        r_scr[n] = jnp.broadcast_to(1.0 / denom, r_scr.shape[1:])
    for n, (b, kk) in enumerate(units):
        rows = slice(b * ATT_BLOCK, (b + 1) * ATT_BLOCK)
        vw = vx[b * ATT_BLOCK:b * ATT_BLOCK + win, kk * HEAD_DIM:(kk + 1) * HEAD_DIM]
        o = _dot(p_scr[n], vw) * r_scr[n]
        for g, hd in enumerate(heads_of(kk)):
            o_ref[rows, hd * HEAD_DIM:(hd + 1) * HEAD_DIM] = o[g * ATT_BLOCK:(g + 1) * ATT_BLOCK].astype(BF16)


def _attention(q, k, v, sink_b, meta, tm, cast):
    t = q.shape[0]
    n_tiles = t // tm
    c_ops, c_in, c_out, c_shapes = _cast_job(cast[1], cast[0], n_tiles)
    r = tm // ATT_BLOCK
    n_blk = t // ATT_BLOCK
    cur = pl.BlockSpec((tm, KV_W), lambda i, *_: (i, 0))
    prev = pl.BlockSpec((ATT_BLOCK, KV_W), lambda i, *_: (jnp.maximum(i * r - 1, 0), 0))
    nxt = pl.BlockSpec((ATT_BLOCK, KV_W), lambda i, *_: (jnp.minimum((i + 1) * r, n_blk - 1), 0))
    grid_spec = pltpu.PrefetchScalarGridSpec(
        num_scalar_prefetch=2,
        grid=(n_tiles,),
        in_specs=[_rows(tm, Q_W), cur, prev, nxt, cur, prev, nxt, _resident(sink_b.shape)] + c_in,
        out_specs=[_rows(tm, Q_W)] + c_out,
        scratch_shapes=[pltpu.VMEM((tm + 2 * ATT_BLOCK, KV_W), BF16),
                        pltpu.VMEM((tm + 2 * ATT_BLOCK, KV_W), BF16),
                        pltpu.VMEM((r * N_KV_HEADS, GQA_GROUP * ATT_BLOCK, 3 * ATT_BLOCK), F32),
                        pltpu.VMEM((r * N_KV_HEADS, GQA_GROUP * ATT_BLOCK, 3 * ATT_BLOCK), BF16),
                        pltpu.VMEM((r * N_KV_HEADS, GQA_GROUP * ATT_BLOCK, HEAD_DIM), F32)],
    )
    a, *w_bf = pl.pallas_call(
        _attn_kernel,
        grid_spec=grid_spec,
        out_shape=[jax.ShapeDtypeStruct((t, Q_W), BF16)] + c_shapes,
        compiler_params=_params(("arbitrary",)),
        name="banded_attention",
    )(meta["pos"], meta["len"], q, k, k, k, v, v, v, sink_b, *c_ops)
    return a, [w.reshape(m.shape[1:]) for w, m in zip(w_bf, cast[1])]


def _mix0_kernel(tp, tl, ts, xa_ref, xb_ref, a_ref, pc_ref, pp_ref, pn_ref, gate_ref, pw_ref, ps_ref, wo_ref,
                 *rest, n_a):
    route_in, o_ref = rest[:N_ROUTE_IN], rest[N_ROUTE_IN]
    route_out, (pext, carry) = rest[N_ROUTE_IN + 1:N_ROUTE_IN + 1 + N_ROUTE_OUT], rest[-2:]
    i = pl.program_id(0)
    tm = xa_ref.shape[0]
    first = tp[i] == 0
    last = tp[i] + tm == tl[i]
    pext[0:POOL_HALO] = jnp.where(first, 0.0, pp_ref[...])
    pext[POOL_HALO:POOL_HALO + tm] = pc_ref[...]
    pext[POOL_HALO + tm:] = jnp.where(last, 0.0, pn_ref[...])
    n_g = len(POOL_WINDOWS)
    gw = pc_ref.shape[1] // n_g
    pos = tp[i] + lax.broadcasted_iota(jnp.int32, (tm, gw), 0)
    seq_len = tl[i]
    ms = []
    for g, w in enumerate(POOL_WINDOWS):
        cols = slice(g * gw, (g + 1) * gw)
        acc = pext[POOL_HALO - w // 2:POOL_HALO - w // 2 + tm, cols]
        for off in range(-w // 2 + 1, w // 2):
            acc = acc + pext[POOL_HALO + off:POOL_HALO + off + tm, cols]
        cnt = (jnp.minimum(pos + w // 2, seq_len) - jnp.maximum(pos - w // 2, 0)).astype(F32)
        dlt = acc / cnt - pc_ref[:, cols]
        ms.append((_dot(dlt.astype(BF16), pw_ref[g]) * ps_ref[:, cols]).astype(BF16))
    m = jnp.concatenate(ms, axis=1)
    w_a = a_ref.shape[1]
    mix = _dot(a_ref[...], wo_ref[0:w_a, :]) + _dot(m, wo_ref[w_a:, :])
    x = jnp.where(i < n_a, xa_ref[...], xb_ref[...])
    x_new = x + gate_ref[...] * mix
    o_ref[...] = x_new
    _route_tile(x_new, *route_in, *route_out, carry)


def _mix0(x, a, p, meta, mod, pool_w_bf, pool_scale, w_out_bf, route, tm):
    xa, xb, n_a = _as_pair(x, tm)
    d = xa.shape[1]
    t, pw = p.shape
    n_tiles = t // tm
    r = tm // POOL_HALO
    n_hb = t // POOL_HALO
    prev = pl.BlockSpec((POOL_HALO, pw), lambda i, *_: (jnp.maximum(i * r - 1, 0), 0))
    nxt = pl.BlockSpec((POOL_HALO, pw), lambda i, *_: (jnp.minimum((i + 1) * r, n_hb - 1), 0))
    r_in, r_out, r_shapes, r_scratch = _route_specs(tm, t, d, seq_arg=2)
    grid_spec = pltpu.PrefetchScalarGridSpec(
        num_scalar_prefetch=3,
        grid=(n_tiles,),
        in_specs=_split_rows(tm, d, n_a) + [
            _rows(tm, a.shape[1]), _rows(tm, pw), prev, nxt, _mod_spec(2, d, seq_arg=2),
            _resident(pool_w_bf.shape), _resident((1, pw)), _resident(w_out_bf.shape)] + r_in,
        out_specs=[_rows(tm, d)] + r_out,
        scratch_shapes=[pltpu.VMEM((tm + 2 * POOL_HALO, pw), F32), r_scratch],
    )
    return pl.pallas_call(
        functools.partial(_mix0_kernel, n_a=n_a),
        grid_spec=grid_spec,
        out_shape=[jax.ShapeDtypeStruct((t, d), F32)] + r_shapes,
        compiler_params=_params(("arbitrary",)),
        name="pool_outproj_route",
    )(meta["pos"], meta["len"], meta["seq"], xa, xb, a, p, p, p, mod, pool_w_bf,
      pool_scale.reshape(1, pw), w_out_bf, *_route_operands(route, mod, d))


def _gelu(z):
    return 0.5 * z * (1.0 + lax.erf(z * np.float32(np.sqrt(0.5))))


def _gmlp_in_kernel(ts, x_ref, g_ref, sc_ref, sh_ref, w_ref, lg_ref, lb_ref, u_ref, v_ref):
    h = _norm_mod(x_ref[...], g_ref[...], sc_ref[...], sh_ref[...]).astype(BF16)
    half = u_ref.shape[1]
    u_ref[...] = _gelu(_dot(h, w_ref[:, 0:half]))
    zv = _gelu(_dot(h, w_ref[:, half:]))
    zc = zv - jnp.mean(zv, axis=-1, keepdims=True)
    r = lax.rsqrt(jnp.mean(zc * zc, axis=-1, keepdims=True) + EPS)
    v_ref[...] = (zc * r * lg_ref[...] + lb_ref[...]).astype(BF16)


def _gmlp_in(x, meta, mod, norm_g, w_in_bf, ln_g, ln_b, tm):
    t, d = x.shape
    half = w_in_bf.shape[1] // 2
    grid_spec = pltpu.PrefetchScalarGridSpec(
        num_scalar_prefetch=1,
        grid=(t // tm,),
        in_specs=[_rows(tm, d), _resident((1, d)), _mod_spec(1, d), _mod_spec(0, d),
                  _resident(w_in_bf.shape), _resident((1, half)), _resident((1, half))],
        out_specs=[_rows(tm, half), _rows(tm, half)],
    )
    return pl.pallas_call(
        _gmlp_in_kernel,
        grid_spec=grid_spec,
        out_shape=[jax.ShapeDtypeStruct((t, half), F32), jax.ShapeDtypeStruct((t, half), BF16)],
        compiler_params=_params(("arbitrary",)),
        name="gmlp_in",
    )(meta["seq"], x, norm_g.reshape(1, d), mod, mod, w_in_bf, ln_g.reshape(1, half), ln_b.reshape(1, half))


def _gmlp_out_kernel(ts, x_ref, u_ref, v_ref, gate_ref, sw_ref, sb_ref, wo_ref, *rest):
    route_in, o_ref = rest[:N_ROUTE_IN], rest[N_ROUTE_IN]
    route_out, (gated, carry) = rest[N_ROUTE_IN + 1:N_ROUTE_IN + 1 + N_ROUTE_OUT], rest[-2:]
    tm = x_ref.shape[0]
    hd = u_ref.shape[1] // SGU_HEADS
    for c in range(tm // CHUNK):
        rows = slice(c * CHUNK, (c + 1) * CHUNK)
        for hh in range(SGU_HEADS):
            cols = slice(hh * hd, (hh + 1) * hd)
            s = _dot(sw_ref[hh], v_ref[rows, cols]) + jnp.tile(sb_ref[hh], (1, hd // LANES))
            gated[rows, cols] = (u_ref[rows, cols] * s).astype(BF16)
    x_new = x_ref[...] + gate_ref[...] * _dot(gated[...], wo_ref[...])
    o_ref[...] = x_new
    _route_tile(x_new, *route_in, *route_out, carry)


def _gmlp_out(x, u, v, meta, mod, sgu_w_bf, sgu_b_b, w_out_bf, route, tm):
    t, d = x.shape
    w = u.shape[1]
    r_in, r_out, r_shapes, r_scratch = _route_specs(tm, t, d, seq_arg=0)
    grid_spec = pltpu.PrefetchScalarGridSpec(
        num_scalar_prefetch=1,
        grid=(t // tm,),
        in_specs=[_rows(tm, d), _rows(tm, w), _rows(tm, w), _mod_spec(2, d),
                  _resident(sgu_w_bf.shape), _resident(sgu_b_b.shape), _resident(w_out_bf.shape)] + r_in,
        out_specs=[_rows(tm, d)] + r_out,
        scratch_shapes=[pltpu.VMEM((tm, w), BF16), r_scratch],
    )
    return pl.pallas_call(
        _gmlp_out_kernel,
        grid_spec=grid_spec,
        out_shape=[jax.ShapeDtypeStruct((t, d), F32)] + r_shapes,
        compiler_params=_params(("arbitrary",)),
        name="gmlp_out_route",
    )(meta["seq"], x, u, v, mod, sgu_w_bf, sgu_b_b, w_out_bf, *_route_operands(route, mod, d))


ROUTER_ROWS = 48


def _route_tile(x, g_ref, sc_ref, sh_ref, wr_ref, br_ref, h_ref, ri_ref, rw_ref, cnt_ref, carry):
    i = pl.program_id(0)

    @pl.when(i == 0)
    def _():
        carry[...] = jnp.zeros_like(carry)

    tm = x.shape[0]
    hb = _norm_mod(x, g_ref[...], sc_ref[...], sh_ref[...]).astype(BF16)
    h_ref[...] = _rows_to_tiles(_pack_bf16_pair(hb))
    lg = _dot_nt(wr_ref[...], hb) + jnp.tile(br_ref[...], (1, tm // LANES))
    rows8 = lax.broadcasted_iota(jnp.int32, (EXP_PER_GROUP, tm), 0)

    def first_argmax(vals, vmax):
        return jnp.min(jnp.where(vals == vmax, rows8, EXP_PER_GROUP), axis=0, keepdims=True)

    gl = lg[0:8]
    gmax = jnp.max(gl, axis=0, keepdims=True)
    gidx = first_argmax(gl, gmax)
    g_w = 1.0 / jnp.sum(jnp.exp(gl - gmax), axis=0, keepdims=True)
    esel = jnp.zeros((EXP_PER_GROUP, tm), F32)
    for g in range(N_GROUPS):
        esel = jnp.where(gidx == g, lg[8 + g * EXP_PER_GROUP:8 + (g + 1) * EXP_PER_GROUP], esel)
    emax = jnp.max(esel, axis=0, keepdims=True)
    pe = jnp.exp(esel - emax)
    prob = pe / jnp.sum(pe, axis=0, keepdims=True)
    p1 = jnp.max(prob, axis=0, keepdims=True)
    i1 = first_argmax(prob, p1)
    rest = jnp.where(rows8 == i1, -1.0, prob)
    p2 = jnp.max(rest, axis=0, keepdims=True)
    i2 = first_argmax(rest, p2)
    den = p1 + p2
    w0 = g_w * (p1 / den)
    w1 = g_w * (p2 / den)
    e0 = gidx * EXP_PER_GROUP + i1
    e1 = gidx * EXP_PER_GROUP + i2

    rows_e = lax.broadcasted_iota(jnp.int32, (N_EXPERTS, tm), 0)
    oh0 = rows_e == e0
    oh1 = rows_e == e1
    both = jnp.where(oh0, 1.0, 0.0) + jnp.where(oh1, 1.0, 0.0)
    ri = lax.broadcasted_iota(jnp.int32, (tm, tm), 0)
    ci = lax.broadcasted_iota(jnp.int32, (tm, tm), 1)
    upper = jnp.where(ri < ci, 1.0, 0.0).astype(BF16)
    before = _dot(both.astype(BF16), upper) + carry[:, 0:1]
    r0 = jnp.sum(jnp.where(oh0, before, 0.0), axis=0, keepdims=True)
    r1 = jnp.sum(jnp.where(oh1, before, 0.0), axis=0, keepdims=True)
    new_cnt = carry[...] + jnp.sum(both, axis=1, keepdims=True)
    carry[...] = new_cnt
    cnt_ref[...] = new_cnt
    zi = jnp.zeros((4, tm), jnp.int32)
    ri_ref[...] = jnp.concatenate([e0, e1, r0.astype(jnp.int32), r1.astype(jnp.int32), zi], axis=0)
    rw_ref[...] = jnp.concatenate([w0, w1, jnp.zeros((6, tm), F32)], axis=0)


N_ROUTE_IN, N_ROUTE_OUT = 5, 4


def _route_specs(tm, t, d, seq_arg):
    in_specs = [_resident((1, d)), _mod_spec(4, d, seq_arg), _mod_spec(3, d, seq_arg),
                _resident((ROUTER_ROWS, d)), _resident((ROUTER_ROWS, LANES))]
    assert d // 2 == SUBLANES * LANES
    out_specs = [pl.BlockSpec((tm, SUBLANES, LANES), lambda i, *_: (i, 0, 0)),
                 pl.BlockSpec((8, tm), lambda i, *_: (0, i)),
                 pl.BlockSpec((8, tm), lambda i, *_: (0, i)),
                 pl.BlockSpec((N_EXPERTS, LANES), lambda i, *_: (0, 0))]
    out_shapes = [jax.ShapeDtypeStruct((t, SUBLANES, LANES), jnp.uint32),
                  jax.ShapeDtypeStruct((8, t), jnp.int32),
                  jax.ShapeDtypeStruct((8, t), F32),
                  jax.ShapeDtypeStruct((N_EXPERTS, LANES), F32)]
    return in_specs, out_specs, out_shapes, pltpu.VMEM((N_EXPERTS, LANES), F32)


def _route_operands(route, mod, d):
    norm_g, wr_bf, br_b = route
    return [norm_g.reshape(1, d), mod, mod, wr_bf, br_b]


def _row_copy(src, s_row, dst, d_row, sem):
    return pltpu.make_async_copy(src.at[s_row], dst.at[d_row], sem)


ZERO_ROWS = 32


def _dispatch_kernel(pad_lo, pad_hi, slot_ref, hp_ref, *rest):
    n_w = (len(rest) - 4) // 2
    w_in, xs_hbm, w_out = rest[:n_w], rest[n_w], rest[n_w + 1:2 * n_w + 1]
    zrow, sem, zsem = rest[-3:]
    i = pl.program_id(0)
    ch = slot_ref.shape[2] // 2
    zn = zrow.shape[0]

    @pl.when(i == 0)
    def _():
        zrow[...] = jnp.zeros_like(zrow)

        def per_range(e, c):
            lo, hi = pad_lo[e], pad_hi[e]
            lo_al = jnp.minimum(((lo + zn - 1) // zn) * zn, hi)

            def row(r):
                return _row_copy(zrow, 0, xs_hbm, r, zsem)

            def grp(g):
                return pltpu.make_async_copy(zrow, xs_hbm.at[pl.ds(g * zn, zn)], zsem)

            lax.fori_loop(lo, lo_al, lambda r, c2: (row(r).start(), c2)[1], 0)
            lax.fori_loop(lo_al // zn, hi // zn, lambda g, c2: (grp(g).start(), c2)[1], 0)
            lax.fori_loop(lo, lo_al, lambda r, c2: (row(r).wait(), c2)[1], 0)
            lax.fori_loop(lo_al // zn, hi // zn, lambda g, c2: (grp(g).wait(), c2)[1], 0)
            return c

        lax.fori_loop(0, pad_lo.shape[0], per_range, 0)

    def issue(q, c):
        for u in range(SUBLANES):
            r = q * SUBLANES + u
            for k in range(2):
                _row_copy(hp_ref, r, xs_hbm, slot_ref[0, 0, k * ch + r], sem).start(priority=k)
        return c

    lax.fori_loop(0, ch // SUBLANES, issue, 0)
    _cast_tiles(w_in, w_out)
    for k in range(2):
        pltpu.make_async_copy(hp_ref, xs_hbm.at[pl.ds(0, ch)], sem).wait()


def _dispatch(hp, slots, pad_lo, pad_hi, n_rows, ch, cast=None):
    t = hp.shape[0]
    steps = t // ch
    weights = cast[1] if cast else ()
    w_ops, w_in_specs, w_out_specs, w_out_shapes = _cast_job(weights, cast[0] if cast else 0, steps)
    grid_spec = pltpu.PrefetchScalarGridSpec(
        num_scalar_prefetch=2,
        grid=(steps,),
        in_specs=[
            pl.BlockSpec((1, 1, 2 * ch), lambda i, *_: (i, 0, 0), memory_space=pltpu.SMEM),
            pl.BlockSpec((ch,) + hp.shape[1:], lambda i, *_: (i, 0, 0)),
        ] + w_in_specs,
        out_specs=[pl.BlockSpec(memory_space=pl.ANY)] + w_out_specs,
        scratch_shapes=[pltpu.VMEM((ZERO_ROWS,) + hp.shape[1:], hp.dtype), pltpu.SemaphoreType.DMA,
                        pltpu.SemaphoreType.DMA],
    )
    xs, *w_bf = pl.pallas_call(
        _dispatch_kernel,
        grid_spec=grid_spec,
        out_shape=[jax.ShapeDtypeStruct((n_rows,) + hp.shape[1:], hp.dtype)] + w_out_shapes,
        compiler_params=_params(("arbitrary",)),
        name="moe_dispatch",
    )(pad_lo, pad_hi, slots, hp, *w_ops)
    return xs, [w.reshape(m.shape[1:]) for w, m in zip(w_bf, weights)]


FFN_ROW_STEPS = 4


def _rows_to_tiles(x):
    n = x.shape[0]
    parts = [x[:, c * LANES:(c + 1) * LANES].reshape(n // SUBLANES, SUBLANES, LANES) for c in range(SUBLANES)]
    a = jnp.swapaxes(jnp.stack(parts, axis=1), 1, 2)
    return a.reshape(n, SUBLANES, LANES)


def _tiles_to_rows(a):
    n = a.shape[0]
    a = jnp.swapaxes(a.reshape(n // SUBLANES, SUBLANES, SUBLANES, LANES), 1, 2)
    return jnp.concatenate([a[:, c].reshape(n, LANES) for c in range(SUBLANES)], axis=1)


def _pack_bf16_pair(x):
    half = x.shape[1] // 2
    bits = pltpu.bitcast(x.astype(BF16).astype(F32), jnp.uint32)
    return (bits[:, :half] >> 16) | bits[:, half:]


def _unpack_bf16_pair(xp):
    return pltpu.bitcast(xp << 16, F32), pltpu.bitcast(xp & jnp.uint32(0xFFFF0000), F32)


def _ffn_kernel(be, nv, xs_ref, wg_ref, wu_ref, wd_ref, o_ref):
    i = pl.program_id(0)
    rb = o_ref.shape[0]
    step = rb // FFN_ROW_STEPS
    n = nv[i]

    def ffn_rows(rows):
        lo, hi = _unpack_bf16_pair(_tiles_to_rows(xs_ref[0:rows]))
        x = jnp.concatenate([lo.astype(BF16), hi.astype(BF16)], axis=1)
        hg = _dot(x, wg_ref[...])
        hu = _dot(x, wu_ref[...])
        act = (hg * (1.0 / (1.0 + jnp.exp(-hg))) * hu).astype(BF16)
        o_ref[0:rows] = _rows_to_tiles(_pack_bf16_pair(_dot(act, wd_ref[...])))
        if rows < rb:
            o_ref[rows:] = jnp.zeros((rb - rows,) + o_ref.shape[1:], o_ref.dtype)

    @pl.when(n == 0)
    def _():
        o_ref[...] = jnp.zeros_like(o_ref)

    for k in range(1, FFN_ROW_STEPS + 1):
        pl.when((n > (k - 1) * step) & (n <= k * step))(functools.partial(ffn_rows, k * step))


def _expert_ffn(xs, blk_expert, blk_valid, wg, wu, wd, rb):
    n_rows = xs.shape[0]
    _, d, ff = wg.shape
    row_block = pl.BlockSpec((rb,) + xs.shape[1:], lambda i, be, nv: (i, 0, 0))
    grid_spec = pltpu.PrefetchScalarGridSpec(
        num_scalar_prefetch=2,
        grid=(n_rows // rb,),
        in_specs=[
            row_block,
            pl.BlockSpec((None, d, ff), lambda i, be, nv: (be[i], 0, 0)),
            pl.BlockSpec((None, d, ff), lambda i, be, nv: (be[i], 0, 0)),
            pl.BlockSpec((None, ff, d), lambda i, be, nv: (be[i], 0, 0)),
        ],
        out_specs=row_block,
    )
    return pl.pallas_call(
        _ffn_kernel,
        grid_spec=grid_spec,
        out_shape=jax.ShapeDtypeStruct(xs.shape, xs.dtype),
        compiler_params=_params(("arbitrary",)),
        name="expert_ffn",
    )(blk_expert, blk_valid, xs, wg, wu, wd)


def _combine_kernel(ts, idx_ref, nxt_ref, x_ref, wt_ref, gate_ref, ys_hbm, *rest, n_a):
    outs, (buf, sem) = rest[:-2], rest[-2:]
    i = pl.program_id(0)
    tm = x_ref.shape[0]
    cur = i % 2

    def fetch(idx, b):
        def issue(q, c):
            for u in range(SUBLANES):
                r = q * SUBLANES + u
                _row_copy(ys_hbm, idx[0, 0, r], buf.at[b], r, sem.at[b]).start(priority=u % 2)
            return c

        lax.fori_loop(0, 2 * tm // SUBLANES, issue, 0)

    @pl.when(i == 0)
    def _():
        fetch(idx_ref, 0)

    @pl.when(i + 1 < pl.num_programs(0))
    def _():
        fetch(nxt_ref, 1 - cur)

    pltpu.make_async_copy(buf.at[cur], buf.at[cur], sem.at[cur]).wait()
    y0 = _unpack_bf16_pair(_tiles_to_rows(buf[cur, 0:tm]))
    y1 = _unpack_bf16_pair(_tiles_to_rows(buf[cur, tm:2 * tm]))
    w0, w1 = wt_ref[:, 0:1], wt_ref[:, 1:2]
    y = jnp.concatenate([w0 * y0[0] + w1 * y1[0], w0 * y0[1] + w1 * y1[1]], axis=1)
    res = x_ref[...] + gate_ref[...] * y
    if n_a is None:
        outs[0][...] = res
    else:
        @pl.when(i < n_a)
        def _():
            outs[0][...] = res

        @pl.when(i >= n_a)
        def _():
            outs[1][...] = res


def _combine(x, ys, slots, wt, meta, mod, tm, split_rows=None):
    t, d = x.shape
    n_tiles = t // tm
    if split_rows is None:
        n_a, out_specs = None, _rows(tm, d)
        out_shape = jax.ShapeDtypeStruct((t, d), F32)
    else:
        n_a, out_specs = split_rows // tm, _split_rows(tm, d, split_rows // tm)
        out_shape = [jax.ShapeDtypeStruct((split_rows, d), F32), jax.ShapeDtypeStruct((t - split_rows, d), F32)]
    gate_spec = pl.BlockSpec((None, None, 1, d), lambda i, ts: (5, ts[i], 0, 0))
    grid_spec = pltpu.PrefetchScalarGridSpec(
        num_scalar_prefetch=1,
        grid=(n_tiles,),
        in_specs=[
            pl.BlockSpec((1, 1, 2 * tm), lambda i, ts: (i, 0, 0), memory_space=pltpu.SMEM),
            pl.BlockSpec((1, 1, 2 * tm), lambda i, ts: (jnp.minimum(i + 1, n_tiles - 1), 0, 0),
                         memory_space=pltpu.SMEM),
            _rows(tm, d),
            _rows(tm, wt.shape[1]),
            gate_spec,
            pl.BlockSpec(memory_space=pl.ANY),
        ],
        out_specs=out_specs,
        scratch_shapes=[pltpu.VMEM((2, 2 * tm) + ys.shape[1:], ys.dtype), pltpu.SemaphoreType.DMA((2,))],
    )
    return pl.pallas_call(
        functools.partial(_combine_kernel, n_a=n_a),
        grid_spec=grid_spec,
        out_shape=out_shape,
        compiler_params=_params(("arbitrary",)),
        name="moe_combine",
    )(meta["seq"], slots, slots, x, wt, mod, ys)


def _tile_meta(seq_lens, tm):
    seq, pos, ln = [], [], []
    for s, n in enumerate(seq_lens):
        assert n % tm == 0
        for k in range(n // tm):
            seq.append(s)
            pos.append(k * tm)
            ln.append(n)
    return {k: jnp.asarray(np.asarray(v, np.int32)) for k, v in (("seq", seq), ("pos", pos), ("len", ln))}


def _rope_tables(s_max):
    half = HEAD_DIM // 2
    inv = ROPE_THETA ** (-jnp.arange(half, dtype=F32) / half)
    ang = jnp.arange(s_max, dtype=F32)[:, None] * inv[None, :]
    cos, sin = jnp.cos(ang), jnp.sin(ang)
    return jnp.concatenate([cos, cos], axis=1), jnp.concatenate([-sin, sin], axis=1)


def _tile_slots(slot, tm):
    t = slot.shape[1]
    return slot.reshape(2, t // tm, tm).transpose(1, 0, 2).reshape(t // tm, 1, 2 * tm)


def _route_params(norm_g, rg_w, rg_b, re_w, re_b):
    d = rg_w.shape[0]
    wr = jnp.zeros((ROUTER_ROWS, d), F32).at[0:N_GROUPS].set(rg_w.T).at[8:8 + N_EXPERTS].set(re_w.T)
    br = jnp.zeros((ROUTER_ROWS,), F32).at[N_GROUPS:8].set(NEG_BIG).at[0:N_GROUPS].set(rg_b)
    br = br.at[8:8 + N_EXPERTS].set(re_b)
    return norm_g, wr.astype(BF16), jnp.broadcast_to(br[:, None], (ROUTER_ROWS, LANES))


def _moe(x, routed, metas, mod, layer, weights, w_bf, cfg, split_rows=None):
    t, d = x.shape
    hp, ri, rw, cnt = routed

    rb = cfg["rb"]
    assert rb % ZERO_ROWS == 0
    expert = ri[0:2]
    rank = ri[2:4]
    counts = cnt[:, 0].astype(jnp.int32)
    nblk = (counts + rb - 1) // rb
    blk_end = jnp.cumsum(nblk)
    row_start = (blk_end - nblk) * rb
    ids = jnp.arange(N_EXPERTS, dtype=jnp.int32)
    start_of = jnp.sum(jnp.where(expert[:, :, None] == ids, row_start, 0), axis=-1)
    slot = start_of + rank
    n_blocks = -(-2 * t // rb) + N_EXPERTS
    n_used = blk_end[-1]
    blk = jnp.arange(n_blocks, dtype=jnp.int32)
    blk_expert = jnp.minimum(jnp.sum(blk[:, None] >= blk_end[None, :], axis=1), N_EXPERTS - 1).astype(jnp.int32)
    last_used = jnp.sum(jnp.where(blk == n_used - 1, blk_expert, 0))
    blk_expert = jnp.where(blk < n_used, blk_expert, last_used)
    onehot = blk_expert[:, None] == ids
    row_end_of = jnp.sum(jnp.where(onehot, row_start + counts, 0), axis=1)
    blk_valid = jnp.where(blk < n_used, jnp.clip(row_end_of - blk * rb, 0, rb), 0).astype(jnp.int32)
    pad_lo = jnp.concatenate([row_start + counts, (n_used * rb).reshape(1)])
    pad_hi = jnp.concatenate([blk_end * rb, jnp.full((1,), n_blocks * rb, jnp.int32)])

    ch = t // (cfg["dispatch_steps_per_expert"] * N_EXPERTS)
    xs, cast_here = _dispatch(hp, _tile_slots(slot, ch), pad_lo.astype(jnp.int32), pad_hi.astype(jnp.int32),
                              n_blocks * rb, ch, None if w_bf else (layer, weights))
    ys = _expert_ffn(xs, blk_expert, blk_valid, *(w_bf or cast_here), rb)

    tmc = cfg["tm_combine"]
    wt = rw.T
    return _combine(x, ys, _tile_slots(slot, tmc), wt, metas[tmc], mod, tmc, split_rows)


def _config(seq_lens):
    g = int(np.gcd.reduce(np.asarray(seq_lens)))
    tm = min(512, g)
    t = int(np.sum(seq_lens))
    per_expert = max(1, t // (N_EXPERTS * 640))
    assert t % (per_expert * N_EXPERTS * SUBLANES) == 0
    return {"tm": tm, "tm_combine": min(512, g), "tm_gmlp": min(512, g), "rb": 512,
            "dispatch_steps_per_expert": per_expert}


def _forward(x_prompt, x_sample, c_prompt, c_sample, ada_w, ada_b, norm_mix_g, norm_ffn_g,
             ab_w_in, q_norm_g, k_norm_g, attn_sink, pool_w, pool_scale, ab_w_out,
             c_w_in, sgu_ln_g, sgu_ln_b, sgu_w, sgu_b, c_w_out,
             router_group_w, router_group_b, router_expert_w, router_expert_b,
             exp_w_gate, exp_w_up, exp_w_down, cfg=None):
    bp, sp, d = x_prompt.shape
    bs, ss, _ = x_sample.shape
    seq_lens = [sp] * bp + [ss] * bs
    n_seq = len(seq_lens)
    if cfg is None:
        cfg = _config(seq_lens)
    depth = ada_w.shape[0]
    x = (x_prompt.reshape(bp * sp, d), x_sample.reshape(bs * ss, d))
    metas ={tm: _tile_meta(seq_lens, tm) for tm in
             {cfg["tm"], cfg["tm_combine"], cfg["tm_gmlp"]}}

    n_pad = -(-n_seq // 8) * 8
    c_pad = jnp.zeros((n_pad, d), F32).at[0:n_seq].set(jnp.concatenate([c_prompt, c_sample], axis=0))
    mod_all = _ada_mod(c_pad, ada_w, ada_b)
    cos_t, sin_t = _rope_tables(max(seq_lens))

    expert_w = (exp_w_gate, exp_w_up, exp_w_down)
    for l in range(depth):
        w_bf = None
        mod = mod_all[l, 0:n_seq].reshape(n_seq, 6, d).transpose(1, 0, 2).reshape(6, n_seq, 1, d)
        i = l // 2
        route = _route_params(norm_ffn_g[l], router_group_w[l], router_group_b[l],
                              router_expert_w[l], router_expert_b[l])
        if l % 2 == 0:
            tm = cfg["tm"]
            q, k, v, p = _inproj(x, metas[tm], mod, norm_mix_g[l], ab_w_in[i].astype(BF16),
                                 q_norm_g[i], k_norm_g[i], cos_t, sin_t, tm)
            sink_b = jnp.broadcast_to(
                jnp.repeat(attn_sink[i].reshape(N_KV_HEADS, GQA_GROUP), ATT_BLOCK, axis=1)[:, :, None],
                (N_KV_HEADS, GQA_GROUP * ATT_BLOCK, LANES)).astype(F32)
            a, w_bf = _attention(q, k, v, sink_b, metas[tm], tm, (l, expert_w))
            x, *routed = _mix0(x, a, p, metas[tm], mod, pool_w[i].astype(BF16), pool_scale[i],
                               ab_w_out[i].astype(BF16), route, tm)
        else:
            tm = cfg["tm_gmlp"]
            if isinstance(x, tuple):
                x = jnp.concatenate(x, axis=0)
            u, v = _gmlp_in(x,metas[tm], mod, norm_mix_g[l], c_w_in[i].astype(BF16),
                            sgu_ln_g[i], sgu_ln_b[i], tm)
            sgu_b_b = jnp.broadcast_to(sgu_b[i][:, :, None], (SGU_HEADS, CHUNK, LANES)).astype(F32)
            x, *routed = _gmlp_out(x, u, v, metas[tm], mod, sgu_w[i].astype(BF16), sgu_b_b,
                                   c_w_out[i].astype(BF16), route, tm)
        x = _moe(x, routed, metas, mod, l, expert_w, w_bf, cfg,
                 split_rows=bp * sp if l == depth - 1 else None)

    y_prompt, y_sample = x
    return (y_prompt.reshape(bp, sp, d), y_sample.reshape(bs, ss, d))


def kernel(x_prompt, x_sample, c_prompt, c_sample, ada_w, ada_b, norm_mix_g, norm_ffn_g, ab_w_in, q_norm_g,
           k_norm_g, attn_sink, pool_w, pool_scale, ab_w_out, c_w_in, sgu_ln_g, sgu_ln_b, sgu_w, sgu_b,
           c_w_out, router_group_w, router_group_b, router_expert_w, router_expert_b, exp_w_gate, exp_w_up,
           exp_w_down):
    return _forward(x_prompt, x_sample, c_prompt, c_sample, ada_w, ada_b, norm_mix_g, norm_ffn_g, ab_w_in,
                    q_norm_g, k_norm_g, attn_sink, pool_w, pool_scale, ab_w_out, c_w_in, sgu_ln_g, sgu_ln_b,
                    sgu_w, sgu_b, c_w_out, router_group_w, router_group_b, router_expert_w, router_expert_b,
                    exp_w_gate, exp_w_up, exp_w_down)
```

```python
import functools

import numpy as np
import jax
import jax.numpy as jnp
from jax import lax
from jax.experimental import pallas as pl
from jax.experimental.pallas import tpu as pltpu

HEAD_DIM = 128
N_HEADS = 8
N_KV_HEADS = 2
GQA_GROUP = N_HEADS // N_KV_HEADS
ATT_BLOCK = 128
ROPE_THETA = 10000.0
Q_W = N_HEADS * HEAD_DIM
KV_W = N_KV_HEADS * HEAD_DIM
POOL_WINDOWS = (2, 4, 8, 16)
POOL_HALO = 8
CHUNK = 128
SGU_HEADS = 8
N_GROUPS = 4
EXP_PER_GROUP = 8
N_EXPERTS = N_GROUPS * EXP_PER_GROUP
EPS = 1e-6

VMEM_LIMIT_BYTES = 56 * 1024 * 1024
LANES = 128
SUBLANES = 8

F32 = jnp.float32
BF16 = jnp.bfloat16
NEG_BIG = -1e30


def _params(sem):
    return pltpu.CompilerParams(dimension_semantics=sem, vmem_limit_bytes=VMEM_LIMIT_BYTES)


def _resident(shape):
    nd = len(shape)
    return pl.BlockSpec(shape, lambda *_: (0,) * nd, pipeline_mode=pl.Buffered(1))


def _rows(tm, width):
    return pl.BlockSpec((tm, width), lambda i, *_: (i, 0))


def _split_rows(tm, width, n_a):
    return [pl.BlockSpec((tm, width), lambda i, *_: (jnp.minimum(i, n_a - 1), 0)),
            pl.BlockSpec((tm, width), lambda i, *_: (jnp.maximum(i - n_a, 0), 0))]


def _as_pair(x, tm):
    if isinstance(x, tuple):
        return x[0], x[1], x[0].shape[0] // tm
    return x, x, x.shape[0] // tm


def _mod_spec(part, d, seq_arg=0):
    return pl.BlockSpec((None, None, 1, d), lambda i, *pf: (part, pf[seq_arg][i], 0, 0))


def _cast_job(weights, layer, steps):
    ops, ins, outs, shapes = [], [], [], []
    for m in weights:
        n_l, e, r, c = m.shape
        rows = e * r
        blk = -(-(-(-rows // steps)) // 16) * 16
        last = -(-rows // blk) - 1
        ops.append(m.reshape(n_l, rows, c))
        ins.append(pl.BlockSpec((None, blk, c), lambda i, *_, last=last: (layer, jnp.minimum(i, last), 0)))
        outs.append(pl.BlockSpec((blk, c), lambda i, *_, last=last: (jnp.minimum(i, last), 0)))
        shapes.append(jax.ShapeDtypeStruct((rows, c), BF16))
    return ops, ins, outs, shapes


def _cast_tiles(w_in, w_out):
    for src, dst in zip(w_in, w_out):
        dst[...] = src[...].astype(BF16)


def _norm_mod(x, g, sc, sh):
    r = lax.rsqrt(jnp.mean(x * x, axis=-1, keepdims=True) + EPS)
    return x * r * g * (1.0 + sc) + sh


def _dot(a, b):
    return jnp.dot(a, b, preferred_element_type=F32)


def _dot_nt(a, b):
    return lax.dot_general(a, b, (((1,), (1,)), ((), ())), preferred_element_type=F32)


def _ada_kernel(c_ref, w_ref, b_ref, o_ref):
    c = c_ref[...]
    cs = c * (1.0 / (1.0 + jnp.exp(-c)))
    o_ref[...] = _dot(cs.astype(BF16), w_ref[...].astype(BF16)) + b_ref[...]


def _ada_mod(c_pad, ada_w, ada_b):
    depth, d, n = ada_w.shape
    tn = 1024
    return pl.pallas_call(
        _ada_kernel,
        grid=(depth, n // tn),
        in_specs=[
            pl.BlockSpec(c_pad.shape, lambda l, j: (0, 0)),
            pl.BlockSpec((None, d, tn), lambda l, j: (l, 0, j)),
            pl.BlockSpec((None, 1, tn), lambda l, j: (l, 0, j)),
        ],
        out_specs=pl.BlockSpec((None, c_pad.shape[0], tn), lambda l, j: (l, 0, j)),
        out_shape=jax.ShapeDtypeStruct((depth, c_pad.shape[0], n), F32),
        compiler_params=_params(("arbitrary", "arbitrary")),
        name="ada_mod",
    )(c_pad, ada_w, ada_b.reshape(depth, 1, n))


def _inproj_kernel(ts, tp, xa_ref, xb_ref, g_ref, sc_ref, sh_ref, w_ref, qg_ref, kg_ref, cos_ref, sin_ref,
                   q_ref, k_ref, v_ref, p_ref, *, n_a):
    x = jnp.where(pl.program_id(0) < n_a, xa_ref[...], xb_ref[...])
    h = _norm_mod(x, g_ref[...], sc_ref[...], sh_ref[...]).astype(BF16)
    cos = cos_ref[...]
    sin = sin_ref[...]

    def head_norm_rope(y, gain):
        r = lax.rsqrt(jnp.mean(y * y, axis=-1, keepdims=True) + EPS)
        y = y * r * gain
        return y * cos + pltpu.roll(y, HEAD_DIM // 2, 1) * sin

    q = _dot(h, w_ref[:, 0:Q_W])
    for hh in range(N_HEADS):
        sl = slice(hh * HEAD_DIM, (hh + 1) * HEAD_DIM)
        q_ref[:, sl] = head_norm_rope(q[:, sl], qg_ref[...]).astype(BF16)
    kv = _dot(h, w_ref[:, Q_W:Q_W + 2 * KV_W])
    for hh in range(N_KV_HEADS):
        sl = slice(hh * HEAD_DIM, (hh + 1) * HEAD_DIM)
        k_ref[:, sl] = head_norm_rope(kv[:, sl], kg_ref[...]).astype(BF16)
    v_ref[...] = kv[:, KV_W:].astype(BF16)
    p_ref[...] = _dot(h, w_ref[:, Q_W + 2 * KV_W:])


def _inproj(x, meta, mod, norm_g, w_in_bf, q_g, k_g, cos_t, sin_t, tm):
    xa, xb, n_a = _as_pair(x, tm)
    d = xa.shape[1]
    n_tiles = meta["seq"].shape[0]
    t = n_tiles * tm
    pool_w = w_in_bf.shape[1] - Q_W - 2 * KV_W
    rope_spec = pl.BlockSpec((tm, HEAD_DIM), lambda i, ts, tp: (tp[i] // tm, 0))
    grid_spec = pltpu.PrefetchScalarGridSpec(
        num_scalar_prefetch=2,
        grid=(n_tiles,),
        in_specs=_split_rows(tm, d, n_a) + [
            _resident((1, d)),
            _mod_spec(1, d),
            _mod_spec(0, d),
            _resident(w_in_bf.shape),
            _resident((1, HEAD_DIM)),
            _resident((1, HEAD_DIM)),
            rope_spec,
            rope_spec,
        ],
        out_specs=[_rows(tm, Q_W), _rows(tm, KV_W), _rows(tm, KV_W), _rows(tm, pool_w)],
    )
    return pl.pallas_call(
        functools.partial(_inproj_kernel, n_a=n_a),
        grid_spec=grid_spec,
        out_shape=[
            jax.ShapeDtypeStruct((t, Q_W), BF16),
            jax.ShapeDtypeStruct((t, KV_W), BF16),
            jax.ShapeDtypeStruct((t, KV_W), BF16),
            jax.ShapeDtypeStruct((t, pool_w), F32),
        ],
        compiler_params=_params(("arbitrary",)),
        name="attn_pool_inproj",
    )(meta["seq"], meta["pos"], xa, xb, norm_g.reshape(1, d), mod, mod, w_in_bf,
      q_g.reshape(1, HEAD_DIM), k_g.reshape(1, HEAD_DIM), cos_t, sin_t)


def _attn_kernel(tp, tl, q_ref, kc_ref, kp_ref, kn_ref, vc_ref, vp_ref, vn_ref, sink_ref, *rest):
    n_w = (len(rest) - 6) // 2
    w_in, o_ref, w_out = rest[:n_w], rest[n_w], rest[n_w + 1:2 * n_w + 1]
    kx, vx, s_scr, p_scr, r_scr = rest[-5:]
    _cast_tiles(w_in, w_out)
    i = pl.program_id(0)
    tm = q_ref.shape[0]
    nb = tm // ATT_BLOCK
    first = tp[i] == 0
    last = tp[i] + tm == tl[i]
    kx[0:ATT_BLOCK] = kp_ref[...]
    kx[ATT_BLOCK:ATT_BLOCK + tm] = kc_ref[...]
    kx[ATT_BLOCK + tm:] = kn_ref[...]
    vx[0:ATT_BLOCK] = vp_ref[...]
    vx[ATT_BLOCK:ATT_BLOCK + tm] = vc_ref[...]
    vx[ATT_BLOCK + tm:] = vn_ref[...]
    win = 3 * ATT_BLOCK
    qi = lax.broadcasted_iota(jnp.int32, (ATT_BLOCK, win), 0)
    kj = lax.broadcasted_iota(jnp.int32, (ATT_BLOCK, win), 1)
    band = (kj >= qi) & (kj <= qi + 2 * ATT_BLOCK)
    scale = HEAD_DIM ** -0.5
    units = [(b, kk) for b in range(nb) for kk in range(N_KV_HEADS)]

    def heads_of(kk):
        return [kk * GQA_GROUP + g for g in range(GQA_GROUP)]

    for n, (b, kk) in enumerate(units):
        rows = slice(b * ATT_BLOCK, (b + 1) * ATT_BLOCK)
        kw = kx[b * ATT_BLOCK:b * ATT_BLOCK + win, kk * HEAD_DIM:(kk + 1) * HEAD_DIM]
        qs = jnp.concatenate([q_ref[rows, hd * HEAD_DIM:(hd + 1) * HEAD_DIM] for hd in heads_of(kk)], axis=0)
        s_scr[n] = _dot_nt(qs, kw)
    for n, (b, kk) in enumerate(units):
        valid = band
        if b == 0:
            valid = valid & (kj >= jnp.where(first, ATT_BLOCK, 0))
        if b == nb - 1:
            valid = valid & (kj < jnp.where(last, 2 * ATT_BLOCK, win))
        bias = jnp.where(valid, 0.0, -jnp.inf).astype(F32)
        s = s_scr[n] * scale
        s = (s.reshape(GQA_GROUP, ATT_BLOCK, win) + bias[None]).reshape(GQA_GROUP * ATT_BLOCK, win)
        sk = sink_ref[kk][:, 0:1]
        m = jnp.maximum(jnp.max(s, axis=-1, keepdims=True), sk)
        p = jnp.exp(s - m)
        denom = jnp.sum(p, axis=-1, keepdims=True) + jnp.exp(sk - m)
        p_scr[n] = p.astype(BF16)
        r_scr[n] = jnp.broadcast_to(1.0 / denom, r_scr.shape[1:])
    for n, (b, kk) in enumerate(units):
        rows = slice(b * ATT_BLOCK, (b + 1) * ATT_BLOCK)
        vw = vx[b * ATT_BLOCK:b * ATT_BLOCK + win, kk * HEAD_DIM:(kk + 1) * HEAD_DIM]
        o = _dot(p_scr[n], vw) * r_scr[n]
        for g, hd in enumerate(heads_of(kk)):
            o_ref[rows, hd * HEAD_DIM:(hd + 1) * HEAD_DIM] = o[g * ATT_BLOCK:(g + 1) * ATT_BLOCK].astype(BF16)


def _attention(q, k, v, sink_b, meta, tm, cast):
    t = q.shape[0]
    n_tiles = t // tm
    weights = cast[1] if cast else ()
    c_ops, c_in, c_out, c_shapes = _cast_job(weights, cast[0] if cast else 0, n_tiles)
    r = tm // ATT_BLOCK
    n_blk = t // ATT_BLOCK
    cur = pl.BlockSpec((tm, KV_W), lambda i, *_: (i, 0))
    prev = pl.BlockSpec((ATT_BLOCK, KV_W), lambda i, *_: (jnp.maximum(i * r - 1, 0), 0))
    nxt = pl.BlockSpec((ATT_BLOCK, KV_W), lambda i, *_: (jnp.minimum((i + 1) * r, n_blk - 1), 0))
    grid_spec = pltpu.PrefetchScalarGridSpec(
        num_scalar_prefetch=2,
        grid=(n_tiles,),
        in_specs=[_rows(tm, Q_W), cur, prev, nxt, cur, prev, nxt, _resident(sink_b.shape)] + c_in,
        out_specs=[_rows(tm, Q_W)] + c_out,
        scratch_shapes=[pltpu.VMEM((tm + 2 * ATT_BLOCK, KV_W), BF16),
                        pltpu.VMEM((tm + 2 * ATT_BLOCK, KV_W), BF16),
                        pltpu.VMEM((r * N_KV_HEADS, GQA_GROUP * ATT_BLOCK, 3 * ATT_BLOCK), F32),
                        pltpu.VMEM((r * N_KV_HEADS, GQA_GROUP * ATT_BLOCK, 3 * ATT_BLOCK), BF16),
                        pltpu.VMEM((r * N_KV_HEADS, GQA_GROUP * ATT_BLOCK, HEAD_DIM), F32)],
    )
    a, *w_bf = pl.pallas_call(
        _attn_kernel,
        grid_spec=grid_spec,
        out_shape=[jax.ShapeDtypeStruct((t, Q_W), BF16)] + c_shapes,
        compiler_params=_params(("arbitrary",)),
        name="banded_attention",
    )(meta["pos"], meta["len"], q, k, k, k, v, v, v, sink_b, *c_ops)
    return a, [w.reshape(m.shape[1:]) for w, m in zip(w_bf, weights)]


def _mix0_kernel(tp, tl, ts, xa_ref, xb_ref, a_ref, pc_ref, pp_ref, pn_ref, gate_ref, pw_ref, ps_ref, wo_ref,
                 *rest, n_a):
    route_in, o_ref = rest[:N_ROUTE_IN], rest[N_ROUTE_IN]
    route_out, (pext, carry) = rest[N_ROUTE_IN + 1:N_ROUTE_IN + 1 + N_ROUTE_OUT], rest[-2:]
    i = pl.program_id(0)
    tm = xa_ref.shape[0]
    first = tp[i] == 0
    last = tp[i] + tm == tl[i]
    pext[0:POOL_HALO] = jnp.where(first, 0.0, pp_ref[...])
    pext[POOL_HALO:POOL_HALO + tm] = pc_ref[...]
    pext[POOL_HALO + tm:] = jnp.where(last, 0.0, pn_ref[...])
    n_g = len(POOL_WINDOWS)
    gw = pc_ref.shape[1] // n_g
    pos = tp[i] + lax.broadcasted_iota(jnp.int32, (tm, gw), 0)
    seq_len = tl[i]
    ms = []
    for g, w in enumerate(POOL_WINDOWS):
        cols = slice(g * gw, (g + 1) * gw)
        acc = pext[POOL_HALO - w // 2:POOL_HALO - w // 2 + tm, cols]
        for off in range(-w // 2 + 1, w // 2):
            acc = acc + pext[POOL_HALO + off:POOL_HALO + off + tm, cols]
        cnt = (jnp.minimum(pos + w // 2, seq_len) - jnp.maximum(pos - w // 2, 0)).astype(F32)
        dlt = acc / cnt - pc_ref[:, cols]
        ms.append((_dot(dlt.astype(BF16), pw_ref[g]) * ps_ref[:, cols]).astype(BF16))
    m = jnp.concatenate(ms, axis=1)
    w_a = a_ref.shape[1]
    mix = _dot(a_ref[...], wo_ref[0:w_a, :]) + _dot(m, wo_ref[w_a:, :])
    x = jnp.where(i < n_a, xa_ref[...], xb_ref[...])
    x_new = x + gate_ref[...] * mix
    o_ref[...] = x_new
    _route_tile(x_new, *route_in, *route_out, carry)


def _mix0(x, a, p, meta, mod, pool_w_bf, pool_scale, w_out_bf, route, tm):
    xa, xb, n_a = _as_pair(x, tm)
    d = xa.shape[1]
    t, pw = p.shape
    n_tiles = t // tm
    r = tm // POOL_HALO
    n_hb = t // POOL_HALO
    prev = pl.BlockSpec((POOL_HALO, pw), lambda i, *_: (jnp.maximum(i * r - 1, 0), 0))
    nxt = pl.BlockSpec((POOL_HALO, pw), lambda i, *_: (jnp.minimum((i + 1) * r, n_hb - 1), 0))
    r_in, r_out, r_shapes, r_scratch = _route_specs(tm, t, d, seq_arg=2)
    grid_spec = pltpu.PrefetchScalarGridSpec(
        num_scalar_prefetch=3,
        grid=(n_tiles,),
        in_specs=_split_rows(tm, d, n_a) + [
            _rows(tm, a.shape[1]), _rows(tm, pw), prev, nxt, _mod_spec(2, d, seq_arg=2),
            _resident(pool_w_bf.shape), _resident((1, pw)), _resident(w_out_bf.shape)] + r_in,
        out_specs=[_rows(tm, d)] + r_out,
        scratch_shapes=[pltpu.VMEM((tm + 2 * POOL_HALO, pw), F32), r_scratch],
    )
    return pl.pallas_call(
        functools.partial(_mix0_kernel, n_a=n_a),
        grid_spec=grid_spec,
        out_shape=[jax.ShapeDtypeStruct((t, d), F32)] + r_shapes,
        compiler_params=_params(("arbitrary",)),
        name="pool_outproj_route",
    )(meta["pos"], meta["len"], meta["seq"], xa, xb, a, p, p, p, mod, pool_w_bf,
      pool_scale.reshape(1, pw), w_out_bf, *_route_operands(route, mod, d))


def _gelu(z):
    return 0.5 * z * (1.0 + lax.erf(z * np.float32(np.sqrt(0.5))))


def _gmlp_in_kernel(ts, x_ref, g_ref, sc_ref, sh_ref, w_ref, lg_ref, lb_ref, u_ref, v_ref):
    h = _norm_mod(x_ref[...], g_ref[...], sc_ref[...], sh_ref[...]).astype(BF16)
    half = u_ref.shape[1]
    u_ref[...] = _gelu(_dot(h, w_ref[:, 0:half]))
    zv = _gelu(_dot(h, w_ref[:, half:]))
    zc = zv - jnp.mean(zv, axis=-1, keepdims=True)
    r = lax.rsqrt(jnp.mean(zc * zc, axis=-1, keepdims=True) + EPS)
    v_ref[...] = (zc * r * lg_ref[...] + lb_ref[...]).astype(BF16)


def _gmlp_in(x, meta, mod, norm_g, w_in_bf, ln_g, ln_b, tm):
    t, d = x.shape
    half = w_in_bf.shape[1] // 2
    grid_spec = pltpu.PrefetchScalarGridSpec(
        num_scalar_prefetch=1,
        grid=(t // tm,),
        in_specs=[_rows(tm, d), _resident((1, d)), _mod_spec(1, d), _mod_spec(0, d),
                  _resident(w_in_bf.shape), _resident((1, half)), _resident((1, half))],
        out_specs=[_rows(tm, half), _rows(tm, half)],
    )
    return pl.pallas_call(
        _gmlp_in_kernel,
        grid_spec=grid_spec,
        out_shape=[jax.ShapeDtypeStruct((t, half), F32), jax.ShapeDtypeStruct((t, half), BF16)],
        compiler_params=_params(("arbitrary",)),
        name="gmlp_in",
    )(meta["seq"], x, norm_g.reshape(1, d), mod, mod, w_in_bf, ln_g.reshape(1, half), ln_b.reshape(1, half))


def _gmlp_out_kernel(ts, x_ref, u_ref, v_ref, gate_ref, sw_ref, sb_ref, wo_ref, *rest):
    route_in, o_ref = rest[:N_ROUTE_IN], rest[N_ROUTE_IN]
    route_out, (gated, carry) = rest[N_ROUTE_IN + 1:N_ROUTE_IN + 1 + N_ROUTE_OUT], rest[-2:]
    tm = x_ref.shape[0]
    hd = u_ref.shape[1] // SGU_HEADS
    for c in range(tm // CHUNK):
        rows = slice(c * CHUNK, (c + 1) * CHUNK)
        for hh in range(SGU_HEADS):
            cols = slice(hh * hd, (hh + 1) * hd)
            s = _dot(sw_ref[hh], v_ref[rows, cols]) + jnp.tile(sb_ref[hh], (1, hd // LANES))
            gated[rows, cols] = (u_ref[rows, cols] * s).astype(BF16)
    x_new = x_ref[...] + gate_ref[...] * _dot(gated[...], wo_ref[...])
    o_ref[...] = x_new
    _route_tile(x_new, *route_in, *route_out, carry)


def _gmlp_out(x, u, v, meta, mod, sgu_w_bf, sgu_b_b, w_out_bf, route, tm):
    t, d = x.shape
    w = u.shape[1]
    r_in, r_out, r_shapes, r_scratch = _route_specs(tm, t, d, seq_arg=0)
    grid_spec = pltpu.PrefetchScalarGridSpec(
        num_scalar_prefetch=1,
        grid=(t // tm,),
        in_specs=[_rows(tm, d), _rows(tm, w), _rows(tm, w), _mod_spec(2, d),
                  _resident(sgu_w_bf.shape), _resident(sgu_b_b.shape), _resident(w_out_bf.shape)] + r_in,
        out_specs=[_rows(tm, d)] + r_out,
        scratch_shapes=[pltpu.VMEM((tm, w), BF16), r_scratch],
    )
    return pl.pallas_call(
        _gmlp_out_kernel,
        grid_spec=grid_spec,
        out_shape=[jax.ShapeDtypeStruct((t, d), F32)] + r_shapes,
        compiler_params=_params(("arbitrary",)),
        name="gmlp_out_route",
    )(meta["seq"], x, u, v, mod, sgu_w_bf, sgu_b_b, w_out_bf, *_route_operands(route, mod, d))


ROUTER_ROWS = 48


def _route_tile(x, g_ref, sc_ref, sh_ref, wr_ref, br_ref, h_ref, ri_ref, rw_ref, cnt_ref, carry):
    i = pl.program_id(0)

    @pl.when(i == 0)
    def _():
        carry[...] = jnp.zeros_like(carry)

    tm = x.shape[0]
    hb = _norm_mod(x, g_ref[...], sc_ref[...], sh_ref[...]).astype(BF16)
    h_ref[...] = _rows_to_tiles(_pack_bf16_pair(hb))
    lg = _dot_nt(wr_ref[...], hb) + jnp.tile(br_ref[...], (1, tm // LANES))
    rows8 = lax.broadcasted_iota(jnp.int32, (EXP_PER_GROUP, tm), 0)

    def first_argmax(vals, vmax):
        return jnp.min(jnp.where(vals == vmax, rows8, EXP_PER_GROUP), axis=0, keepdims=True)

    gl = lg[0:8]
    gmax = jnp.max(gl, axis=0, keepdims=True)
    gidx = first_argmax(gl, gmax)
    g_w = 1.0 / jnp.sum(jnp.exp(gl - gmax), axis=0, keepdims=True)
    esel = jnp.zeros((EXP_PER_GROUP, tm), F32)
    for g in range(N_GROUPS):
        esel = jnp.where(gidx == g, lg[8 + g * EXP_PER_GROUP:8 + (g + 1) * EXP_PER_GROUP], esel)
    emax = jnp.max(esel, axis=0, keepdims=True)
    pe = jnp.exp(esel - emax)
    prob = pe / jnp.sum(pe, axis=0, keepdims=True)
    p1 = jnp.max(prob, axis=0, keepdims=True)
    i1 = first_argmax(prob, p1)
    rest = jnp.where(rows8 == i1, -1.0, prob)
    p2 = jnp.max(rest, axis=0, keepdims=True)
    i2 = first_argmax(rest, p2)
    den = p1 + p2
    w0 = g_w * (p1 / den)
    w1 = g_w * (p2 / den)
    e0 = gidx * EXP_PER_GROUP + i1
    e1 = gidx * EXP_PER_GROUP + i2

    rows_e = lax.broadcasted_iota(jnp.int32, (N_EXPERTS, tm), 0)
    oh0 = rows_e == e0
    oh1 = rows_e == e1
    both = jnp.where(oh0, 1.0, 0.0) + jnp.where(oh1, 1.0, 0.0)
    ri = lax.broadcasted_iota(jnp.int32, (tm, tm), 0)
    ci = lax.broadcasted_iota(jnp.int32, (tm, tm), 1)
    upper = jnp.where(ri < ci, 1.0, 0.0).astype(BF16)
    before = _dot(both.astype(BF16), upper) + carry[:, 0:1]
    r0 = jnp.sum(jnp.where(oh0, before, 0.0), axis=0, keepdims=True)
    r1 = jnp.sum(jnp.where(oh1, before, 0.0), axis=0, keepdims=True)
    new_cnt = carry[...] + jnp.sum(both, axis=1, keepdims=True)
    carry[...] = new_cnt
    cnt_ref[...] = new_cnt
    zi = jnp.zeros((4, tm), jnp.int32)
    ri_ref[...] = jnp.concatenate([e0, e1, r0.astype(jnp.int32), r1.astype(jnp.int32), zi], axis=0)
    rw_ref[...] = jnp.concatenate([w0, w1, jnp.zeros((6, tm), F32)], axis=0)


N_ROUTE_IN, N_ROUTE_OUT = 5, 4


def _route_specs(tm, t, d, seq_arg):
    in_specs = [_resident((1, d)), _mod_spec(4, d, seq_arg), _mod_spec(3, d, seq_arg),
                _resident((ROUTER_ROWS, d)), _resident((ROUTER_ROWS, LANES))]
    assert d // 2 == SUBLANES * LANES
    out_specs = [pl.BlockSpec((tm, SUBLANES, LANES), lambda i, *_: (i, 0, 0)),
                 pl.BlockSpec((8, tm), lambda i, *_: (0, i)),
                 pl.BlockSpec((8, tm), lambda i, *_: (0, i)),
                 pl.BlockSpec((N_EXPERTS, LANES), lambda i, *_: (0, 0))]
    out_shapes = [jax.ShapeDtypeStruct((t, SUBLANES, LANES), jnp.uint32),
                  jax.ShapeDtypeStruct((8, t), jnp.int32),
                  jax.ShapeDtypeStruct((8, t), F32),
                  jax.ShapeDtypeStruct((N_EXPERTS, LANES), F32)]
    return in_specs, out_specs, out_shapes, pltpu.VMEM((N_EXPERTS, LANES), F32)


def _route_operands(route, mod, d):
    norm_g, wr_bf, br_b = route
    return [norm_g.reshape(1, d), mod, mod, wr_bf, br_b]


def _row_copy(src, s_row, dst, d_row, sem):
    return pltpu.make_async_copy(src.at[s_row], dst.at[d_row], sem)


ZERO_ROWS = 32


def _dispatch_kernel(pad_lo, pad_hi, slot_ref, hp_ref, *rest):
    n_w = (len(rest) - 4) // 2
    w_in, xs_hbm, w_out = rest[:n_w], rest[n_w], rest[n_w + 1:2 * n_w + 1]
    zrow, sem, zsem = rest[-3:]
    i = pl.program_id(0)
    ch = slot_ref.shape[2] // 2
    zn = zrow.shape[0]

    @pl.when(i == 0)
    def _():
        zrow[...] = jnp.zeros_like(zrow)

        def per_range(e, c):
            lo, hi = pad_lo[e], pad_hi[e]
            lo_al = jnp.minimum(((lo + zn - 1) // zn) * zn, hi)

            def row(r):
                return _row_copy(zrow, 0, xs_hbm, r, zsem)

            def grp(g):
                return pltpu.make_async_copy(zrow, xs_hbm.at[pl.ds(g * zn, zn)], zsem)

            lax.fori_loop(lo, lo_al, lambda r, c2: (row(r).start(), c2)[1], 0)
            lax.fori_loop(lo_al // zn, hi // zn, lambda g, c2: (grp(g).start(), c2)[1], 0)
            lax.fori_loop(lo, lo_al, lambda r, c2: (row(r).wait(), c2)[1], 0)
            lax.fori_loop(lo_al // zn, hi // zn, lambda g, c2: (grp(g).wait(), c2)[1], 0)
            return c

        lax.fori_loop(0, pad_lo.shape[0], per_range, 0)

    def issue(q, c):
        for u in range(SUBLANES):
            r = q * SUBLANES + u
            for k in range(2):
                _row_copy(hp_ref, r, xs_hbm, slot_ref[0, 0, k * ch + r], sem).start(priority=k)
        return c

    lax.fori_loop(0, ch // SUBLANES, issue, 0)
    _cast_tiles(w_in, w_out)
    for k in range(2):
        pltpu.make_async_copy(hp_ref, xs_hbm.at[pl.ds(0, ch)], sem).wait()


def _dispatch(hp, slots, pad_lo, pad_hi, n_rows, ch, cast=None):
    t = hp.shape[0]
    steps = t // ch
    weights = cast[1] if cast else ()
    w_ops, w_in_specs, w_out_specs, w_out_shapes = _cast_job(weights, cast[0] if cast else 0, steps)
    grid_spec = pltpu.PrefetchScalarGridSpec(
        num_scalar_prefetch=2,
        grid=(steps,),
        in_specs=[
            pl.BlockSpec((1, 1, 2 * ch), lambda i, *_: (i, 0, 0), memory_space=pltpu.SMEM),
            pl.BlockSpec((ch,) + hp.shape[1:], lambda i, *_: (i, 0, 0)),
        ] + w_in_specs,
        out_specs=[pl.BlockSpec(memory_space=pl.ANY)] + w_out_specs,
        scratch_shapes=[pltpu.VMEM((ZERO_ROWS,) + hp.shape[1:], hp.dtype), pltpu.SemaphoreType.DMA,
                        pltpu.SemaphoreType.DMA],
    )
    xs, *w_bf = pl.pallas_call(
        _dispatch_kernel,
        grid_spec=grid_spec,
        out_shape=[jax.ShapeDtypeStruct((n_rows,) + hp.shape[1:], hp.dtype)] + w_out_shapes,
        compiler_params=_params(("arbitrary",)),
        name="moe_dispatch",
    )(pad_lo, pad_hi, slots, hp, *w_ops)
    return xs, [w.reshape(m.shape[1:]) for w, m in zip(w_bf, weights)]


FFN_ROW_STEPS = 4


def _rows_to_tiles(x):
    n = x.shape[0]
    parts = [x[:, c * LANES:(c + 1) * LANES].reshape(n // SUBLANES, SUBLANES, LANES) for c in range(SUBLANES)]
    a = jnp.swapaxes(jnp.stack(parts, axis=1), 1, 2)
    return a.reshape(n, SUBLANES, LANES)


def _tiles_to_rows(a):
    n = a.shape[0]
    a = jnp.swapaxes(a.reshape(n // SUBLANES, SUBLANES, SUBLANES, LANES), 1, 2)
    return jnp.concatenate([a[:, c].reshape(n, LANES) for c in range(SUBLANES)], axis=1)


def _pack_bf16_pair(x):
    half = x.shape[1] // 2
    bits = pltpu.bitcast(x.astype(BF16).astype(F32), jnp.uint32)
    return (bits[:, :half] >> 16) | bits[:, half:]


def _unpack_bf16_pair(xp):
    return pltpu.bitcast(xp << 16, F32), pltpu.bitcast(xp & jnp.uint32(0xFFFF0000), F32)


def _ffn_kernel(be, nv, xs_ref, wg_ref, wu_ref, wd_ref, *rest):
    n_w = (len(rest) - 1) // 2
    w_in, o_ref, w_out = rest[:n_w], rest[n_w], rest[n_w + 1:]
    _cast_tiles(w_in, w_out)
    i = pl.program_id(0)
    rb = o_ref.shape[0]
    step = rb // FFN_ROW_STEPS
    n = nv[i]

    def ffn_rows(rows):
        lo, hi = _unpack_bf16_pair(_tiles_to_rows(xs_ref[0:rows]))
        x = jnp.concatenate([lo.astype(BF16), hi.astype(BF16)], axis=1)
        hg = _dot(x, wg_ref[...])
        hu = _dot(x, wu_ref[...])
        act = (hg * (1.0 / (1.0 + jnp.exp(-hg))) * hu).astype(BF16)
        o_ref[0:rows] = _rows_to_tiles(_pack_bf16_pair(_dot(act, wd_ref[...])))
        if rows < rb:
            o_ref[rows:] = jnp.zeros((rb - rows,) + o_ref.shape[1:], o_ref.dtype)

    @pl.when(n == 0)
    def _():
        o_ref[...] = jnp.zeros_like(o_ref)

    for k in range(1, FFN_ROW_STEPS + 1):
        pl.when((n > (k - 1) * step) & (n <= k * step))(functools.partial(ffn_rows, k * step))


def _expert_ffn(xs, blk_expert, blk_valid, wg, wu, wd, rb, cast=None):
    n_rows = xs.shape[0]
    _, d, ff = wg.shape
    n_blocks = n_rows // rb
    weights = cast[1] if cast else ()
    c_ops, c_in, c_out, c_shapes = _cast_job(weights, cast[0] if cast else 0, n_blocks)
    row_block = pl.BlockSpec((rb,) + xs.shape[1:], lambda i, be, nv: (i, 0, 0))
    grid_spec = pltpu.PrefetchScalarGridSpec(
        num_scalar_prefetch=2,
        grid=(n_blocks,),
        in_specs=[
            row_block,
            pl.BlockSpec((None, d, ff), lambda i, be, nv: (be[i], 0, 0)),
            pl.BlockSpec((None, d, ff), lambda i, be, nv: (be[i], 0, 0)),
            pl.BlockSpec((None, ff, d), lambda i, be, nv: (be[i], 0, 0)),
        ] + c_in,
        out_specs=[row_block] + c_out,
    )
    ys, *w_bf = pl.pallas_call(
        _ffn_kernel,
        grid_spec=grid_spec,
        out_shape=[jax.ShapeDtypeStruct(xs.shape, xs.dtype)] + c_shapes,
        compiler_params=_params(("arbitrary",)),
        name="expert_ffn",
    )(blk_expert, blk_valid, xs, wg, wu, wd, *c_ops)
    return ys, [w.reshape(m.shape[1:]) for w, m in zip(w_bf, weights)]


def _combine_kernel(ts, idx_ref, nxt_ref, x_ref, wt_ref, gate_ref, ys_hbm, *rest, n_a):
    outs, (buf, sem) = rest[:-2], rest[-2:]
    i = pl.program_id(0)
    tm = x_ref.shape[0]
    cur = i % 2

    def fetch(idx, b):
        def issue(q, c):
            for u in range(SUBLANES):
                r = q * SUBLANES + u
                _row_copy(ys_hbm, idx[0, 0, r], buf.at[b], r, sem.at[b]).start(priority=u % 2)
            return c

        lax.fori_loop(0, 2 * tm // SUBLANES, issue, 0)

    @pl.when(i == 0)
    def _():
        fetch(idx_ref, 0)

    @pl.when(i + 1 < pl.num_programs(0))
    def _():
        fetch(nxt_ref, 1 - cur)

    pltpu.make_async_copy(buf.at[cur], buf.at[cur], sem.at[cur]).wait()
    y0 = _unpack_bf16_pair(_tiles_to_rows(buf[cur, 0:tm]))
    y1 = _unpack_bf16_pair(_tiles_to_rows(buf[cur, tm:2 * tm]))
    w0, w1 = wt_ref[:, 0:1], wt_ref[:, 1:2]
    y = jnp.concatenate([w0 * y0[0] + w1 * y1[0], w0 * y0[1] + w1 * y1[1]], axis=1)
    res = x_ref[...] + gate_ref[...] * y
    if n_a is None:
        outs[0][...] = res
    else:
        @pl.when(i < n_a)
        def _():
            outs[0][...] = res

        @pl.when(i >= n_a)
        def _():
            outs[1][...] = res


def _combine(x, ys, slots, wt, meta, mod, tm, split_rows=None):
    t, d = x.shape
    n_tiles = t // tm
    if split_rows is None:
        n_a, out_specs = None, _rows(tm, d)
        out_shape = jax.ShapeDtypeStruct((t, d), F32)
    else:
        n_a, out_specs = split_rows // tm, _split_rows(tm, d, split_rows // tm)
        out_shape = [jax.ShapeDtypeStruct((split_rows, d), F32), jax.ShapeDtypeStruct((t - split_rows, d), F32)]
    gate_spec = pl.BlockSpec((None, None, 1, d), lambda i, ts: (5, ts[i], 0, 0))
    grid_spec = pltpu.PrefetchScalarGridSpec(
        num_scalar_prefetch=1,
        grid=(n_tiles,),
        in_specs=[
            pl.BlockSpec((1, 1, 2 * tm), lambda i, ts: (i, 0, 0), memory_space=pltpu.SMEM),
            pl.BlockSpec((1, 1, 2 * tm), lambda i, ts: (jnp.minimum(i + 1, n_tiles - 1), 0, 0),
                         memory_space=pltpu.SMEM),
            _rows(tm, d),
            _rows(tm, wt.shape[1]),
            gate_spec,
            pl.BlockSpec(memory_space=pl.ANY),
        ],
        out_specs=out_specs,
        scratch_shapes=[pltpu.VMEM((2, 2 * tm) + ys.shape[1:], ys.dtype), pltpu.SemaphoreType.DMA((2,))],
    )
    return pl.pallas_call(
        functools.partial(_combine_kernel, n_a=n_a),
        grid_spec=grid_spec,
        out_shape=out_shape,
        compiler_params=_params(("arbitrary",)),
        name="moe_combine",
    )(meta["seq"], slots, slots, x, wt, mod, ys)


def _tile_meta(seq_lens, tm):
    seq, pos, ln = [], [], []
    for s, n in enumerate(seq_lens):
        assert n % tm == 0
        for k in range(n // tm):
            seq.append(s)
            pos.append(k * tm)
            ln.append(n)
    return {k: jnp.asarray(np.asarray(v, np.int32)) for k, v in (("seq", seq), ("pos", pos), ("len", ln))}


def _rope_tables(s_max):
    half = HEAD_DIM // 2
    inv = ROPE_THETA ** (-jnp.arange(half, dtype=F32) / half)
    ang = jnp.arange(s_max, dtype=F32)[:, None] * inv[None, :]
    cos, sin = jnp.cos(ang), jnp.sin(ang)
    return jnp.concatenate([cos, cos], axis=1), jnp.concatenate([-sin, sin], axis=1)


def _tile_slots(slot, tm):
    t = slot.shape[1]
    return slot.reshape(2, t // tm, tm).transpose(1, 0, 2).reshape(t // tm, 1, 2 * tm)


def _route_params(norm_g, rg_w, rg_b, re_w, re_b):
    d = rg_w.shape[0]
    wr = jnp.zeros((ROUTER_ROWS, d), F32).at[0:N_GROUPS].set(rg_w.T).at[8:8 + N_EXPERTS].set(re_w.T)
    br = jnp.zeros((ROUTER_ROWS,), F32).at[N_GROUPS:8].set(NEG_BIG).at[0:N_GROUPS].set(rg_b)
    br = br.at[8:8 + N_EXPERTS].set(re_b)
    return norm_g, wr.astype(BF16), jnp.broadcast_to(br[:, None], (ROUTER_ROWS, LANES))


def _moe(x, routed, metas, mod, layer, weights, w_bf, cfg, split_rows=None):
    t, d = x.shape
    hp, ri, rw, cnt = routed

    rb = cfg["rb"]
    assert rb % ZERO_ROWS == 0
    expert = ri[0:2]
    rank = ri[2:4]
    counts = cnt[:, 0].astype(jnp.int32)
    nblk = (counts + rb - 1) // rb
    blk_end = jnp.cumsum(nblk)
    row_start = (blk_end - nblk) * rb
    ids = jnp.arange(N_EXPERTS, dtype=jnp.int32)
    start_of = jnp.sum(jnp.where(expert[:, :, None] == ids, row_start, 0), axis=-1)
    slot = start_of + rank
    n_blocks = -(-2 * t // rb) + N_EXPERTS
    n_used = blk_end[-1]
    blk = jnp.arange(n_blocks, dtype=jnp.int32)
    blk_expert = jnp.minimum(jnp.sum(blk[:, None] >= blk_end[None, :], axis=1), N_EXPERTS - 1).astype(jnp.int32)
    last_used = jnp.sum(jnp.where(blk == n_used - 1, blk_expert, 0))
    blk_expert = jnp.where(blk < n_used, blk_expert, last_used)
    onehot = blk_expert[:, None] == ids
    row_end_of = jnp.sum(jnp.where(onehot, row_start + counts, 0), axis=1)
    blk_valid = jnp.where(blk < n_used, jnp.clip(row_end_of - blk * rb, 0, rb), 0).astype(jnp.int32)
    pad_lo = jnp.concatenate([row_start + counts, (n_used * rb).reshape(1)])
    pad_hi = jnp.concatenate([blk_end * rb, jnp.full((1,), n_blocks * rb, jnp.int32)])

    ch = t // (cfg["dispatch_steps_per_expert"] * N_EXPERTS)
    xs, cast_here = _dispatch(hp, _tile_slots(slot, ch), pad_lo.astype(jnp.int32), pad_hi.astype(jnp.int32),
                              n_blocks * rb, ch, None if w_bf else (layer, weights))
    has_next = layer + 1 < weights[0].shape[0]
    ys, w_bf_next = _expert_ffn(xs, blk_expert, blk_valid, *(w_bf or cast_here), rb,
                                cast=(layer + 1, weights) if has_next else None)

    tmc = cfg["tm_combine"]
    wt = rw.T
    x_new = _combine(x, ys, _tile_slots(slot, tmc), wt, metas[tmc], mod, tmc, split_rows)
    return x_new, (w_bf_next or None)


def _config(seq_lens):
    g = int(np.gcd.reduce(np.asarray(seq_lens)))
    tm = min(512, g)
    t = int(np.sum(seq_lens))
    per_expert = max(1, t // (N_EXPERTS * 640))
    assert t % (per_expert * N_EXPERTS * SUBLANES) == 0
    return {"tm": tm, "tm_combine": min(512, g), "tm_gmlp": min(512, g), "rb": 512,
            "dispatch_steps_per_expert": per_expert}


def _forward(x_prompt, x_sample, c_prompt, c_sample, ada_w, ada_b, norm_mix_g, norm_ffn_g,
             ab_w_in, q_norm_g, k_norm_g, attn_sink, pool_w, pool_scale, ab_w_out,
             c_w_in, sgu_ln_g, sgu_ln_b, sgu_w, sgu_b, c_w_out,
             router_group_w, router_group_b, router_expert_w, router_expert_b,
             exp_w_gate, exp_w_up, exp_w_down, cfg=None):
    bp, sp, d = x_prompt.shape
    bs, ss, _ = x_sample.shape
    seq_lens = [sp] * bp + [ss] * bs
    n_seq = len(seq_lens)
    if cfg is None:
        cfg = _config(seq_lens)
    depth = ada_w.shape[0]
    x = (x_prompt.reshape(bp * sp, d), x_sample.reshape(bs * ss, d))
    metas ={tm: _tile_meta(seq_lens, tm) for tm in
             {cfg["tm"], cfg["tm_combine"], cfg["tm_gmlp"]}}

    n_pad = -(-n_seq // 8) * 8
    c_pad = jnp.zeros((n_pad, d), F32).at[0:n_seq].set(jnp.concatenate([c_prompt, c_sample], axis=0))
    mod_all = _ada_mod(c_pad, ada_w, ada_b)
    cos_t, sin_t = _rope_tables(max(seq_lens))

    expert_w = (exp_w_gate, exp_w_up, exp_w_down)
    w_bf = None
    for l in range(depth):
        mod = mod_all[l, 0:n_seq].reshape(n_seq, 6, d).transpose(1, 0, 2).reshape(6, n_seq, 1, d)
        i = l // 2
        route = _route_params(norm_ffn_g[l], router_group_w[l], router_group_b[l],
                              router_expert_w[l], router_expert_b[l])
        if l % 2 == 0:
            tm = cfg["tm"]
            q, k, v, p = _inproj(x, metas[tm], mod, norm_mix_g[l], ab_w_in[i].astype(BF16),
                                 q_norm_g[i], k_norm_g[i], cos_t, sin_t, tm)
            sink_b = jnp.broadcast_to(
                jnp.repeat(attn_sink[i].reshape(N_KV_HEADS, GQA_GROUP), ATT_BLOCK, axis=1)[:, :, None],
                (N_KV_HEADS, GQA_GROUP * ATT_BLOCK, LANES)).astype(F32)
            a, made = _attention(q, k, v, sink_b, metas[tm], tm, None if w_bf else (l, expert_w))
            w_bf = w_bf or made
            x, *routed = _mix0(x, a, p, metas[tm], mod, pool_w[i].astype(BF16), pool_scale[i],
                               ab_w_out[i].astype(BF16), route, tm)
        else:
            tm = cfg["tm_gmlp"]
            if isinstance(x, tuple):
                x = jnp.concatenate(x, axis=0)
            u, v = _gmlp_in(x,metas[tm], mod, norm_mix_g[l], c_w_in[i].astype(BF16),
                            sgu_ln_g[i], sgu_ln_b[i], tm)
            sgu_b_b = jnp.broadcast_to(sgu_b[i][:, :, None], (SGU_HEADS, CHUNK, LANES)).astype(F32)
            x, *routed = _gmlp_out(x, u, v, metas[tm], mod, sgu_w[i].astype(BF16), sgu_b_b,
                                   c_w_out[i].astype(BF16), route, tm)
        x, w_bf = _moe(x, routed, metas, mod, l, expert_w, w_bf, cfg,
                       split_rows=bp * sp if l == depth - 1 else None)

    y_prompt, y_sample = x
    return (y_prompt.reshape(bp, sp, d), y_sample.reshape(bs, ss, d))


def kernel(x_prompt, x_sample, c_prompt, c_sample, ada_w, ada_b, norm_mix_g, norm_ffn_g, ab_w_in, q_norm_g,
           k_norm_g, attn_sink, pool_w, pool_scale, ab_w_out, c_w_in, sgu_ln_g, sgu_ln_b, sgu_w, sgu_b,
           c_w_out, router_group_w, router_group_b, router_expert_w, router_expert_b, exp_w_gate, exp_w_up,
           exp_w_down):
    return _forward(x_prompt, x_sample, c_prompt, c_sample, ada_w, ada_b, norm_mix_g, norm_ffn_g, ab_w_in,
                    q_norm_g, k_norm_g, attn_sink, pool_w, pool_scale, ab_w_out, c_w_in, sgu_ln_g, sgu_ln_b,
                    sgu_w, sgu_b, c_w_out, router_group_w, router_group_b, router_expert_w, router_expert_b,
                    exp_w_gate, exp_w_up, exp_w_down)
```

```python
import functools

import numpy as np
import jax
import jax.numpy as jnp
from jax import lax
from jax.experimental import pallas as pl
from jax.experimental.pallas import tpu as pltpu

HEAD_DIM = 128
N_HEADS = 8
N_KV_HEADS = 2
GQA_GROUP = N_HEADS // N_KV_HEADS
ATT_BLOCK = 128
ROPE_THETA = 10000.0
Q_W = N_HEADS * HEAD_DIM
KV_W = N_KV_HEADS * HEAD_DIM
POOL_WINDOWS = (2, 4, 8, 16)
POOL_HALO = 8
CHUNK = 128
SGU_HEADS = 8
N_GROUPS = 4
EXP_PER_GROUP = 8
N_EXPERTS = N_GROUPS * EXP_PER_GROUP
EPS = 1e-6

VMEM_LIMIT_BYTES = 56 * 1024 * 1024
LANES = 128
SUBLANES = 8

F32 = jnp.float32
BF16 = jnp.bfloat16
NEG_BIG = -1e30


def _params(sem):
    return pltpu.CompilerParams(dimension_semantics=sem, vmem_limit_bytes=VMEM_LIMIT_BYTES)


def _resident(shape):
    nd = len(shape)
    return pl.BlockSpec(shape, lambda *_: (0,) * nd, pipeline_mode=pl.Buffered(1))


def _rows(tm, width):
    return pl.BlockSpec((tm, width), lambda i, *_: (i, 0))


def _split_rows(tm, width, n_a):
    return [pl.BlockSpec((tm, width), lambda i, *_: (jnp.minimum(i, n_a - 1), 0)),
            pl.BlockSpec((tm, width), lambda i, *_: (jnp.maximum(i - n_a, 0), 0))]


def _as_pair(x, tm):
    if isinstance(x, tuple):
        return x[0], x[1], x[0].shape[0] // tm
    return x, x, x.shape[0] // tm


def _mod_spec(part, d, seq_arg=0):
    return pl.BlockSpec((None, None, 1, d), lambda i, *pf: (part, pf[seq_arg][i], 0, 0))


def _cast_job(weights, layer, steps):
    ops, ins, outs, shapes = [], [], [], []
    for m in weights:
        n_l, e, r, c = m.shape
        rows = e * r
        blk = -(-(-(-rows // steps)) // 16) * 16
        last = -(-rows // blk) - 1
        ops.append(m.reshape(n_l, rows, c))
        ins.append(pl.BlockSpec((None, blk, c), lambda i, *_, last=last: (layer, jnp.minimum(i, last), 0)))
        outs.append(pl.BlockSpec((blk, c), lambda i, *_, last=last: (jnp.minimum(i, last), 0)))
        shapes.append(jax.ShapeDtypeStruct((rows, c), BF16))
    return ops, ins, outs, shapes


def _cast_tiles(w_in, w_out):
    for src, dst in zip(w_in, w_out):
        dst[...] = src[...].astype(BF16)


def _norm_mod(x, g, sc, sh):
    r = lax.rsqrt(jnp.mean(x * x, axis=-1, keepdims=True) + EPS)
    return x * r * g * (1.0 + sc) + sh


def _dot(a, b):
    return jnp.dot(a, b, preferred_element_type=F32)


def _dot_nt(a, b):
    return lax.dot_general(a, b, (((1,), (1,)), ((), ())), preferred_element_type=F32)


def _ada_kernel(c_ref, w_ref, b_ref, o_ref):
    c = c_ref[...]
    cs = c * (1.0 / (1.0 + jnp.exp(-c)))
    o_ref[...] = _dot(cs.astype(BF16), w_ref[...].astype(BF16)) + b_ref[...]


def _ada_mod(c_pad, ada_w, ada_b):
    depth, d, n = ada_w.shape
    tn = 1024
    return pl.pallas_call(
        _ada_kernel,
        grid=(depth, n // tn),
        in_specs=[
            pl.BlockSpec(c_pad.shape, lambda l, j: (0, 0)),
            pl.BlockSpec((None, d, tn), lambda l, j: (l, 0, j)),
            pl.BlockSpec((None, 1, tn), lambda l, j: (l, 0, j)),
        ],
        out_specs=pl.BlockSpec((None, c_pad.shape[0], tn), lambda l, j: (l, 0, j)),
        out_shape=jax.ShapeDtypeStruct((depth, c_pad.shape[0], n), F32),
        compiler_params=_params(("arbitrary", "arbitrary")),
        name="ada_mod",
    )(c_pad, ada_w, ada_b.reshape(depth, 1, n))


def _inproj_kernel(ts, tp, xa_ref, xb_ref, g_ref, sc_ref, sh_ref, w_ref, qg_ref, kg_ref, cos_ref, sin_ref,
                   q_ref, k_ref, v_ref, p_ref, *, n_a):
    x = jnp.where(pl.program_id(0) < n_a, xa_ref[...], xb_ref[...])
    h = _norm_mod(x, g_ref[...], sc_ref[...], sh_ref[...]).astype(BF16)
    cos = cos_ref[...]
    sin = sin_ref[...]

    def head_norm_rope(y, gain):
        r = lax.rsqrt(jnp.mean(y * y, axis=-1, keepdims=True) + EPS)
        y = y * r * gain
        return y * cos + pltpu.roll(y, HEAD_DIM // 2, 1) * sin

    q = _dot(h, w_ref[:, 0:Q_W])
    for hh in range(N_HEADS):
        sl = slice(hh * HEAD_DIM, (hh + 1) * HEAD_DIM)
        q_ref[:, sl] = head_norm_rope(q[:, sl], qg_ref[...]).astype(BF16)
    kv = _dot(h, w_ref[:, Q_W:Q_W + 2 * KV_W])
    for hh in range(N_KV_HEADS):
        sl = slice(hh * HEAD_DIM, (hh + 1) * HEAD_DIM)
        k_ref[:, sl] = head_norm_rope(kv[:, sl], kg_ref[...]).astype(BF16)
    v_ref[...] = kv[:, KV_W:].astype(BF16)
    p_ref[...] = _dot(h, w_ref[:, Q_W + 2 * KV_W:])


def _inproj(x, meta, mod, norm_g, w_in_bf, q_g, k_g, cos_t, sin_t, tm):
    xa, xb, n_a = _as_pair(x, tm)
    d = xa.shape[1]
    n_tiles = meta["seq"].shape[0]
    t = n_tiles * tm
    pool_w = w_in_bf.shape[1] - Q_W - 2 * KV_W
    rope_spec = pl.BlockSpec((tm, HEAD_DIM), lambda i, ts, tp: (tp[i] // tm, 0))
    grid_spec = pltpu.PrefetchScalarGridSpec(
        num_scalar_prefetch=2,
        grid=(n_tiles,),
        in_specs=_split_rows(tm, d, n_a) + [
            _resident((1, d)),
            _mod_spec(1, d),
            _mod_spec(0, d),
            _resident(w_in_bf.shape),
            _resident((1, HEAD_DIM)),
            _resident((1, HEAD_DIM)),
            rope_spec,
            rope_spec,
        ],
        out_specs=[_rows(tm, Q_W), _rows(tm, KV_W), _rows(tm, KV_W), _rows(tm, pool_w)],
    )
    return pl.pallas_call(
        functools.partial(_inproj_kernel, n_a=n_a),
        grid_spec=grid_spec,
        out_shape=[
            jax.ShapeDtypeStruct((t, Q_W), BF16),
            jax.ShapeDtypeStruct((t, KV_W), BF16),
            jax.ShapeDtypeStruct((t, KV_W), BF16),
            jax.ShapeDtypeStruct((t, pool_w), F32),
        ],
        compiler_params=_params(("arbitrary",)),
        name="attn_pool_inproj",
    )(meta["seq"], meta["pos"], xa, xb, norm_g.reshape(1, d), mod, mod, w_in_bf,
      q_g.reshape(1, HEAD_DIM), k_g.reshape(1, HEAD_DIM), cos_t, sin_t)


def _attn_kernel(tp, tl, q_ref, kc_ref, kp_ref, kn_ref, vc_ref, vp_ref, vn_ref, sink_ref, *rest):
    n_w = (len(rest) - 6) // 2
    w_in, o_ref, w_out = rest[:n_w], rest[n_w], rest[n_w + 1:2 * n_w + 1]
    kx, vx, s_scr, p_scr, r_scr = rest[-5:]
    _cast_tiles(w_in, w_out)
    i = pl.program_id(0)
    tm = q_ref.shape[0]
    nb = tm // ATT_BLOCK
    first = tp[i] == 0
    last = tp[i] + tm == tl[i]
    kx[0:ATT_BLOCK] = kp_ref[...]
    kx[ATT_BLOCK:ATT_BLOCK + tm] = kc_ref[...]
    kx[ATT_BLOCK + tm:] = kn_ref[...]
    vx[0:ATT_BLOCK] = vp_ref[...]
    vx[ATT_BLOCK:ATT_BLOCK + tm] = vc_ref[...]
    vx[ATT_BLOCK + tm:] = vn_ref[...]
    win = 3 * ATT_BLOCK
    qi = lax.broadcasted_iota(jnp.int32, (ATT_BLOCK, win), 0)
    kj = lax.broadcasted_iota(jnp.int32, (ATT_BLOCK, win), 1)
    band = (kj >= qi) & (kj <= qi + 2 * ATT_BLOCK)
    scale = HEAD_DIM ** -0.5
    units = [(b, kk) for b in range(nb) for kk in range(N_KV_HEADS)]

    def heads_of(kk):
        return [kk * GQA_GROUP + g for g in range(GQA_GROUP)]

    for n, (b, kk) in enumerate(units):
        rows = slice(b * ATT_BLOCK, (b + 1) * ATT_BLOCK)
        kw = kx[b * ATT_BLOCK:b * ATT_BLOCK + win, kk * HEAD_DIM:(kk + 1) * HEAD_DIM]
        qs = jnp.concatenate([q_ref[rows, hd * HEAD_DIM:(hd + 1) * HEAD_DIM] for hd in heads_of(kk)], axis=0)
        s_scr[n] = _dot_nt(qs, kw)
    for n, (b, kk) in enumerate(units):
        valid = band
        if b == 0:
            valid = valid & (kj >= jnp.where(first, ATT_BLOCK, 0))
        if b == nb - 1:
            valid = valid & (kj < jnp.where(last, 2 * ATT_BLOCK, win))
        bias = jnp.where(valid, 0.0, -jnp.inf).astype(F32)
        s = s_scr[n] * scale
        s = (s.reshape(GQA_GROUP, ATT_BLOCK, win) + bias[None]).reshape(GQA_GROUP * ATT_BLOCK, win)
        sk = sink_ref[kk][:, 0:1]
        m = jnp.maximum(jnp.max(s, axis=-1, keepdims=True), sk)
        p = jnp.exp(s - m)
        denom = jnp.sum(p, axis=-1, keepdims=True) + jnp.exp(sk - m)
        p_scr[n] = p.astype(BF16)
        r_scr[n] = jnp.broadcast_to(1.0 / denom, r_scr.shape[1:])
    for n, (b, kk) in enumerate(units):
        rows = slice(b * ATT_BLOCK, (b + 1) * ATT_BLOCK)
        vw = vx[b * ATT_BLOCK:b * ATT_BLOCK + win, kk * HEAD_DIM:(kk + 1) * HEAD_DIM]
        o = _dot(p_scr[n], vw) * r_scr[n]
        for g, hd in enumerate(heads_of(kk)):
            o_ref[rows, hd * HEAD_DIM:(hd + 1) * HEAD_DIM] = o[g * ATT_BLOCK:(g + 1) * ATT_BLOCK].astype(BF16)


def _attention(q, k, v, sink_b, meta, tm, cast):
    t = q.shape[0]
    n_tiles = t // tm
    weights = cast[1] if cast else ()
    c_ops, c_in, c_out, c_shapes = _cast_job(weights, cast[0] if cast else 0, n_tiles)
    r = tm // ATT_BLOCK
    n_blk = t // ATT_BLOCK
    cur = pl.BlockSpec((tm, KV_W), lambda i, *_: (i, 0))
    prev = pl.BlockSpec((ATT_BLOCK, KV_W), lambda i, *_: (jnp.maximum(i * r - 1, 0), 0))
    nxt = pl.BlockSpec((ATT_BLOCK, KV_W), lambda i, *_: (jnp.minimum((i + 1) * r, n_blk - 1), 0))
    grid_spec = pltpu.PrefetchScalarGridSpec(
        num_scalar_prefetch=2,
        grid=(n_tiles,),
        in_specs=[_rows(tm, Q_W), cur, prev, nxt, cur, prev, nxt, _resident(sink_b.shape)] + c_in,
        out_specs=[_rows(tm, Q_W)] + c_out,
        scratch_shapes=[pltpu.VMEM((tm + 2 * ATT_BLOCK, KV_W), BF16),
                        pltpu.VMEM((tm + 2 * ATT_BLOCK, KV_W), BF16),
                        pltpu.VMEM((r * N_KV_HEADS, GQA_GROUP * ATT_BLOCK, 3 * ATT_BLOCK), F32),
                        pltpu.VMEM((r * N_KV_HEADS, GQA_GROUP * ATT_BLOCK, 3 * ATT_BLOCK), BF16),
                        pltpu.VMEM((r * N_KV_HEADS, GQA_GROUP * ATT_BLOCK, HEAD_DIM), F32)],
    )
    a, *w_bf = pl.pallas_call(
        _attn_kernel,
        grid_spec=grid_spec,
        out_shape=[jax.ShapeDtypeStruct((t, Q_W), BF16)] + c_shapes,
        compiler_params=_params(("arbitrary",)),
        name="banded_attention",
    )(meta["pos"], meta["len"], q, k, k, k, v, v, v, sink_b, *c_ops)
    return a, [w.reshape(m.shape[1:]) for w, m in zip(w_bf, weights)]


def _mix0_kernel(tp, tl, ts, xa_ref, xb_ref, a_ref, pc_ref, pp_ref, pn_ref, gate_ref, pw_ref, ps_ref, wo_ref,
                 *rest, n_a):
    route_in, o_ref = rest[:N_ROUTE_IN], rest[N_ROUTE_IN]
    route_out, (pext, carry) = rest[N_ROUTE_IN + 1:N_ROUTE_IN + 1 + N_ROUTE_OUT], rest[-2:]
    i = pl.program_id(0)
    tm = xa_ref.shape[0]
    first = tp[i] == 0
    last = tp[i] + tm == tl[i]
    pext[0:POOL_HALO] = jnp.where(first, 0.0, pp_ref[...])
    pext[POOL_HALO:POOL_HALO + tm] = pc_ref[...]
    pext[POOL_HALO + tm:] = jnp.where(last, 0.0, pn_ref[...])
    n_g = len(POOL_WINDOWS)
    gw = pc_ref.shape[1] // n_g
    pos = tp[i] + lax.broadcasted_iota(jnp.int32, (tm, gw), 0)
    seq_len = tl[i]
    ms = []
    for g, w in enumerate(POOL_WINDOWS):
        cols = slice(g * gw, (g + 1) * gw)
        acc = pext[POOL_HALO - w // 2:POOL_HALO - w // 2 + tm, cols]
        for off in range(-w // 2 + 1, w // 2):
            acc = acc + pext[POOL_HALO + off:POOL_HALO + off + tm, cols]
        cnt = (jnp.minimum(pos + w // 2, seq_len) - jnp.maximum(pos - w // 2, 0)).astype(F32)
        dlt = acc / cnt - pc_ref[:, cols]
        ms.append((_dot(dlt.astype(BF16), pw_ref[g]) * ps_ref[:, cols]).astype(BF16))
    m = jnp.concatenate(ms, axis=1)
    w_a = a_ref.shape[1]
    mix = _dot(a_ref[...], wo_ref[0:w_a, :]) + _dot(m, wo_ref[w_a:, :])
    x = jnp.where(i < n_a, xa_ref[...], xb_ref[...])
    x_new = x + gate_ref[...] * mix
    o_ref[...] = x_new
    _route_tile(x_new, *route_in, *route_out, carry)


def _mix0(x, a, p, meta, mod, pool_w_bf, pool_scale, w_out_bf, route, tm):
    xa, xb, n_a = _as_pair(x, tm)
    d = xa.shape[1]
    t, pw = p.shape
    n_tiles = t // tm
    r = tm // POOL_HALO
    n_hb = t // POOL_HALO
    prev = pl.BlockSpec((POOL_HALO, pw), lambda i, *_: (jnp.maximum(i * r - 1, 0), 0))
    nxt = pl.BlockSpec((POOL_HALO, pw), lambda i, *_: (jnp.minimum((i + 1) * r, n_hb - 1), 0))
    r_in, r_out, r_shapes, r_scratch = _route_specs(tm, t, d, seq_arg=2)
    grid_spec = pltpu.PrefetchScalarGridSpec(
        num_scalar_prefetch=3,
        grid=(n_tiles,),
        in_specs=_split_rows(tm, d, n_a) + [
            _rows(tm, a.shape[1]), _rows(tm, pw), prev, nxt, _mod_spec(2, d, seq_arg=2),
            _resident(pool_w_bf.shape), _resident((1, pw)), _resident(w_out_bf.shape)] + r_in,
        out_specs=[_rows(tm, d)] + r_out,
        scratch_shapes=[pltpu.VMEM((tm + 2 * POOL_HALO, pw), F32), r_scratch],
    )
    return pl.pallas_call(
        functools.partial(_mix0_kernel, n_a=n_a),
        grid_spec=grid_spec,
        out_shape=[jax.ShapeDtypeStruct((t, d), F32)] + r_shapes,
        compiler_params=_params(("arbitrary",)),
        name="pool_outproj_route",
    )(meta["pos"], meta["len"], meta["seq"], xa, xb, a, p, p, p, mod, pool_w_bf,
      pool_scale.reshape(1, pw), w_out_bf, *_route_operands(route, mod, d))


def _gelu(z):
    return 0.5 * z * (1.0 + lax.erf(z * np.float32(np.sqrt(0.5))))


def _gmlp_in_kernel(ts, x_ref, g_ref, sc_ref, sh_ref, w_ref, lg_ref, lb_ref, u_ref, v_ref):
    h = _norm_mod(x_ref[...], g_ref[...], sc_ref[...], sh_ref[...]).astype(BF16)
    half = u_ref.shape[1]
    u_ref[...] = _gelu(_dot(h, w_ref[:, 0:half]))
    zv = _gelu(_dot(h, w_ref[:, half:]))
    zc = zv - jnp.mean(zv, axis=-1, keepdims=True)
    r = lax.rsqrt(jnp.mean(zc * zc, axis=-1, keepdims=True) + EPS)
    v_ref[...] = (zc * r * lg_ref[...] + lb_ref[...]).astype(BF16)


def _gmlp_in(x, meta, mod, norm_g, w_in_bf, ln_g, ln_b, tm):
    t, d = x.shape
    half = w_in_bf.shape[1] // 2
    grid_spec = pltpu.PrefetchScalarGridSpec(
        num_scalar_prefetch=1,
        grid=(t // tm,),
        in_specs=[_rows(tm, d), _resident((1, d)), _mod_spec(1, d), _mod_spec(0, d),
                  _resident(w_in_bf.shape), _resident((1, half)), _resident((1, half))],
        out_specs=[_rows(tm, half), _rows(tm, half)],
    )
    return pl.pallas_call(
        _gmlp_in_kernel,
        grid_spec=grid_spec,
        out_shape=[jax.ShapeDtypeStruct((t, half), F32), jax.ShapeDtypeStruct((t, half), BF16)],
        compiler_params=_params(("arbitrary",)),
        name="gmlp_in",
    )(meta["seq"], x, norm_g.reshape(1, d), mod, mod, w_in_bf, ln_g.reshape(1, half), ln_b.reshape(1, half))


def _gmlp_out_kernel(ts, x_ref, u_ref, v_ref, gate_ref, sw_ref, sb_ref, wo_ref, *rest):
    route_in, o_ref = rest[:N_ROUTE_IN], rest[N_ROUTE_IN]
    route_out, (gated, carry) = rest[N_ROUTE_IN + 1:N_ROUTE_IN + 1 + N_ROUTE_OUT], rest[-2:]
    tm = x_ref.shape[0]
    hd = u_ref.shape[1] // SGU_HEADS
    for c in range(tm // CHUNK):
        rows = slice(c * CHUNK, (c + 1) * CHUNK)
        for hh in range(SGU_HEADS):
            cols = slice(hh * hd, (hh + 1) * hd)
            s = _dot(sw_ref[hh], v_ref[rows, cols]) + jnp.tile(sb_ref[hh], (1, hd // LANES))
            gated[rows, cols] = (u_ref[rows, cols] * s).astype(BF16)
    x_new = x_ref[...] + gate_ref[...] * _dot(gated[...], wo_ref[...])
    o_ref[...] = x_new
    _route_tile(x_new, *route_in, *route_out, carry)


def _gmlp_out(x, u, v, meta, mod, sgu_w_bf, sgu_b_b, w_out_bf, route, tm):
    t, d = x.shape
    w = u.shape[1]
    r_in, r_out, r_shapes, r_scratch = _route_specs(tm, t, d, seq_arg=0)
    grid_spec = pltpu.PrefetchScalarGridSpec(
        num_scalar_prefetch=1,
        grid=(t // tm,),
        in_specs=[_rows(tm, d), _rows(tm, w), _rows(tm, w), _mod_spec(2, d),
                  _resident(sgu_w_bf.shape), _resident(sgu_b_b.shape), _resident(w_out_bf.shape)] + r_in,
        out_specs=[_rows(tm, d)] + r_out,
        scratch_shapes=[pltpu.VMEM((tm, w), BF16), r_scratch],
    )
    return pl.pallas_call(
        _gmlp_out_kernel,
        grid_spec=grid_spec,
        out_shape=[jax.ShapeDtypeStruct((t, d), F32)] + r_shapes,
        compiler_params=_params(("arbitrary",)),
        name="gmlp_out_route",
    )(meta["seq"], x, u, v, mod, sgu_w_bf, sgu_b_b, w_out_bf, *_route_operands(route, mod, d))


ROUTER_ROWS = 48


def _route_tile(x, g_ref, sc_ref, sh_ref, wr_ref, br_ref, h_ref, ri_ref, rw_ref, cnt_ref, carry):
    i = pl.program_id(0)

    @pl.when(i == 0)
    def _():
        carry[...] = jnp.zeros_like(carry)

    tm = x.shape[0]
    hb = _norm_mod(x, g_ref[...], sc_ref[...], sh_ref[...]).astype(BF16)
    h_ref[...] = _rows_to_tiles(_pack_bf16_pair(hb))
    lg = _dot_nt(wr_ref[...], hb) + jnp.tile(br_ref[...], (1, tm // LANES))
    rows8 = lax.broadcasted_iota(jnp.int32, (EXP_PER_GROUP, tm), 0)

    def first_argmax(vals, vmax):
        return jnp.min(jnp.where(vals == vmax, rows8, EXP_PER_GROUP), axis=0, keepdims=True)

    gl = lg[0:8]
    gmax = jnp.max(gl, axis=0, keepdims=True)
    gidx = first_argmax(gl, gmax)
    g_w = 1.0 / jnp.sum(jnp.exp(gl - gmax), axis=0, keepdims=True)
    esel = jnp.zeros((EXP_PER_GROUP, tm), F32)
    for g in range(N_GROUPS):
        esel = jnp.where(gidx == g, lg[8 + g * EXP_PER_GROUP:8 + (g + 1) * EXP_PER_GROUP], esel)
    emax = jnp.max(esel, axis=0, keepdims=True)
    pe = jnp.exp(esel - emax)
    prob = pe / jnp.sum(pe, axis=0, keepdims=True)
    p1 = jnp.max(prob, axis=0, keepdims=True)
    i1 = first_argmax(prob, p1)
    rest = jnp.where(rows8 == i1, -1.0, prob)
    p2 = jnp.max(rest, axis=0, keepdims=True)
    i2 = first_argmax(rest, p2)
    den = p1 + p2
    w0 = g_w * (p1 / den)
    w1 = g_w * (p2 / den)
    e0 = gidx * EXP_PER_GROUP + i1
    e1 = gidx * EXP_PER_GROUP + i2

    rows_e = lax.broadcasted_iota(jnp.int32, (N_EXPERTS, tm), 0)
    oh0 = rows_e == e0
    oh1 = rows_e == e1
    both = jnp.where(oh0, 1.0, 0.0) + jnp.where(oh1, 1.0, 0.0)
    ri = lax.broadcasted_iota(jnp.int32, (tm, tm), 0)
    ci = lax.broadcasted_iota(jnp.int32, (tm, tm), 1)
    upper = jnp.where(ri < ci, 1.0, 0.0).astype(BF16)
    before = _dot(both.astype(BF16), upper) + carry[:, 0:1]
    r0 = jnp.sum(jnp.where(oh0, before, 0.0), axis=0, keepdims=True)
    r1 = jnp.sum(jnp.where(oh1, before, 0.0), axis=0, keepdims=True)
    new_cnt = carry[...] + jnp.sum(both, axis=1, keepdims=True)
    carry[...] = new_cnt
    cnt_ref[...] = new_cnt
    zi = jnp.zeros((4, tm), jnp.int32)
    ri_ref[...] = jnp.concatenate([e0, e1, r0.astype(jnp.int32), r1.astype(jnp.int32), zi], axis=0)
    rw_ref[...] = jnp.concatenate([w0, w1, jnp.zeros((6, tm), F32)], axis=0)


N_ROUTE_IN, N_ROUTE_OUT = 5, 4


def _route_specs(tm, t, d, seq_arg):
    in_specs = [_resident((1, d)), _mod_spec(4, d, seq_arg), _mod_spec(3, d, seq_arg),
                _resident((ROUTER_ROWS, d)), _resident((ROUTER_ROWS, LANES))]
    assert d // 2 == SUBLANES * LANES
    out_specs = [pl.BlockSpec((tm, SUBLANES, LANES), lambda i, *_: (i, 0, 0)),
                 pl.BlockSpec((8, tm), lambda i, *_: (0, i)),
                 pl.BlockSpec((8, tm), lambda i, *_: (0, i)),
                 pl.BlockSpec((N_EXPERTS, LANES), lambda i, *_: (0, 0))]
    out_shapes = [jax.ShapeDtypeStruct((t, SUBLANES, LANES), jnp.uint32),
                  jax.ShapeDtypeStruct((8, t), jnp.int32),
                  jax.ShapeDtypeStruct((8, t), F32),
                  jax.ShapeDtypeStruct((N_EXPERTS, LANES), F32)]
    return in_specs, out_specs, out_shapes, pltpu.VMEM((N_EXPERTS, LANES), F32)


def _route_operands(route, mod, d):
    norm_g, wr_bf, br_b = route
    return [norm_g.reshape(1, d), mod, mod, wr_bf, br_b]


def _row_copy(src, s_row, dst, d_row, sem):
    return pltpu.make_async_copy(src.at[s_row], dst.at[d_row], sem)


ZERO_ROWS = 32


def _dispatch_kernel(pad_lo, pad_hi, slot_ref, hp_ref, *rest):
    n_w = (len(rest) - 4) // 2
    w_in, xs_hbm, w_out = rest[:n_w], rest[n_w], rest[n_w + 1:2 * n_w + 1]
    zrow, sem, zsem = rest[-3:]
    i = pl.program_id(0)
    ch = slot_ref.shape[2] // 2
    zn = zrow.shape[0]

    @pl.when(i == 0)
    def _():
        zrow[...] = jnp.zeros_like(zrow)

        def per_range(e, c):
            lo, hi = pad_lo[e], pad_hi[e]
            lo_al = jnp.minimum(((lo + zn - 1) // zn) * zn, hi)

            def row(r):
                return _row_copy(zrow, 0, xs_hbm, r, zsem)

            def grp(g):
                return pltpu.make_async_copy(zrow, xs_hbm.at[pl.ds(g * zn, zn)], zsem)

            lax.fori_loop(lo, lo_al, lambda r, c2: (row(r).start(), c2)[1], 0)
            lax.fori_loop(lo_al // zn, hi // zn, lambda g, c2: (grp(g).start(), c2)[1], 0)
            lax.fori_loop(lo, lo_al, lambda r, c2: (row(r).wait(), c2)[1], 0)
            lax.fori_loop(lo_al // zn, hi // zn, lambda g, c2: (grp(g).wait(), c2)[1], 0)
            return c

        lax.fori_loop(0, pad_lo.shape[0], per_range, 0)

    def issue(q, c):
        for u in range(SUBLANES):
            r = q * SUBLANES + u
            for k in range(2):
                _row_copy(hp_ref, r, xs_hbm, slot_ref[0, 0, k * ch + r], sem).start(priority=k)
        return c

    lax.fori_loop(0, ch // SUBLANES, issue, 0)
    _cast_tiles(w_in, w_out)
    for k in range(2):
        pltpu.make_async_copy(hp_ref, xs_hbm.at[pl.ds(0, ch)], sem).wait()


def _dispatch(hp, slots, pad_lo, pad_hi, n_rows, ch, cast=None):
    t = hp.shape[0]
    steps = t // ch
    weights = cast[1] if cast else ()
    w_ops, w_in_specs, w_out_specs, w_out_shapes = _cast_job(weights, cast[0] if cast else 0, steps)
    grid_spec = pltpu.PrefetchScalarGridSpec(
        num_scalar_prefetch=2,
        grid=(steps,),
        in_specs=[
            pl.BlockSpec((1, 1, 2 * ch), lambda i, *_: (i, 0, 0), memory_space=pltpu.SMEM),
            pl.BlockSpec((ch,) + hp.shape[1:], lambda i, *_: (i, 0, 0)),
        ] + w_in_specs,
        out_specs=[pl.BlockSpec(memory_space=pl.ANY)] + w_out_specs,
        scratch_shapes=[pltpu.VMEM((ZERO_ROWS,) + hp.shape[1:], hp.dtype), pltpu.SemaphoreType.DMA,
                        pltpu.SemaphoreType.DMA],
    )
    xs, *w_bf = pl.pallas_call(
        _dispatch_kernel,
        grid_spec=grid_spec,
        out_shape=[jax.ShapeDtypeStruct((n_rows,) + hp.shape[1:], hp.dtype)] + w_out_shapes,
        compiler_params=_params(("arbitrary",)),
        name="moe_dispatch",
    )(pad_lo, pad_hi, slots, hp, *w_ops)
    return xs, [w.reshape(m.shape[1:]) for w, m in zip(w_bf, weights)]


FFN_ROW_STEPS = 4


def _rows_to_tiles(x):
    n = x.shape[0]
    parts = [x[:, c * LANES:(c + 1) * LANES].reshape(n // SUBLANES, SUBLANES, LANES) for c in range(SUBLANES)]
    a = jnp.swapaxes(jnp.stack(parts, axis=1), 1, 2)
    return a.reshape(n, SUBLANES, LANES)


def _tiles_to_rows(a):
    n = a.shape[0]
    a = jnp.swapaxes(a.reshape(n // SUBLANES, SUBLANES, SUBLANES, LANES), 1, 2)
    return jnp.concatenate([a[:, c].reshape(n, LANES) for c in range(SUBLANES)], axis=1)


def _pack_bf16_pair(x):
    half = x.shape[1] // 2
    bits = pltpu.bitcast(x.astype(BF16).astype(F32), jnp.uint32)
    return (bits[:, :half] >> 16) | bits[:, half:]


def _unpack_bf16_pair(xp):
    return pltpu.bitcast(xp << 16, F32), pltpu.bitcast(xp & jnp.uint32(0xFFFF0000), F32)


def _ffn_kernel(be, nv, xs_ref, wg_ref, wu_ref, wd_ref, *rest):
    n_w = (len(rest) - 1) // 2
    w_in, o_ref, w_out = rest[:n_w], rest[n_w], rest[n_w + 1:]
    _cast_tiles(w_in, w_out)
    i = pl.program_id(0)
    rb = o_ref.shape[0]
    step = rb // FFN_ROW_STEPS
    n = nv[i]

    def ffn_rows(rows):
        lo, hi = _unpack_bf16_pair(_tiles_to_rows(xs_ref[0:rows]))
        x = jnp.concatenate([lo.astype(BF16), hi.astype(BF16)], axis=1)
        hg = _dot(x, wg_ref[...])
        hu = _dot(x, wu_ref[...])
        act = (hg * (1.0 / (1.0 + jnp.exp(-hg))) * hu).astype(BF16)
        o_ref[0:rows] = _rows_to_tiles(_pack_bf16_pair(_dot(act, wd_ref[...])))
        if rows < rb:
            o_ref[rows:] = jnp.zeros((rb - rows,) + o_ref.shape[1:], o_ref.dtype)

    @pl.when(n == 0)
    def _():
        o_ref[...] = jnp.zeros_like(o_ref)

    for k in range(1, FFN_ROW_STEPS + 1):
        pl.when((n > (k - 1) * step) & (n <= k * step))(functools.partial(ffn_rows, k * step))


def _expert_ffn(xs, blk_expert, blk_valid, wg, wu, wd, rb, cast=None):
    n_rows = xs.shape[0]
    _, d, ff = wg.shape
    n_blocks = n_rows // rb
    weights = cast[1] if cast else ()
    c_ops, c_in, c_out, c_shapes = _cast_job(weights, cast[0] if cast else 0, n_blocks)
    row_block = pl.BlockSpec((rb,) + xs.shape[1:], lambda i, be, nv: (i, 0, 0))
    grid_spec = pltpu.PrefetchScalarGridSpec(
        num_scalar_prefetch=2,
        grid=(n_blocks,),
        in_specs=[
            row_block,
            pl.BlockSpec((None, d, ff), lambda i, be, nv: (be[i], 0, 0)),
            pl.BlockSpec((None, d, ff), lambda i, be, nv: (be[i], 0, 0)),
            pl.BlockSpec((None, ff, d), lambda i, be, nv: (be[i], 0, 0)),
        ] + c_in,
        out_specs=[row_block] + c_out,
    )
    ys, *w_bf = pl.pallas_call(
        _ffn_kernel,
        grid_spec=grid_spec,
        out_shape=[jax.ShapeDtypeStruct(xs.shape, xs.dtype)] + c_shapes,
        compiler_params=_params(("arbitrary",)),
        name="expert_ffn",
    )(blk_expert, blk_valid, xs, wg, wu, wd, *c_ops)
    return ys, [w.reshape(m.shape[1:]) for w, m in zip(w_bf, weights)]


def _combine_kernel(ts, idx_ref, nxt_ref, x_ref, wt_ref, gate_ref, ys_hbm, *rest, n_a):
    outs, (buf, sem) = rest[:-2], rest[-2:]
    i = pl.program_id(0)
    tm = x_ref.shape[0]
    cur = i % 2

    def fetch(idx, b):
        def issue(q, c):
            for u in range(SUBLANES):
                r = q * SUBLANES + u
                _row_copy(ys_hbm, idx[0, 0, r], buf.at[b], r, sem.at[b]).start(priority=u % 2)
            return c

        lax.fori_loop(0, 2 * tm // SUBLANES, issue, 0)

    @pl.when(i == 0)
    def _():
        fetch(idx_ref, 0)

    @pl.when(i + 1 < pl.num_programs(0))
    def _():
        fetch(nxt_ref, 1 - cur)

    pltpu.make_async_copy(buf.at[cur], buf.at[cur], sem.at[cur]).wait()
    y0 = _unpack_bf16_pair(_tiles_to_rows(buf[cur, 0:tm]))
    y1 = _unpack_bf16_pair(_tiles_to_rows(buf[cur, tm:2 * tm]))
    w0, w1 = wt_ref[:, 0:1], wt_ref[:, 1:2]
    y = jnp.concatenate([w0 * y0[0] + w1 * y1[0], w0 * y0[1] + w1 * y1[1]], axis=1)
    res = x_ref[...] + gate_ref[...] * y
    if n_a is None:
        outs[0][...] = res
    else:
        @pl.when(i < n_a)
        def _():
            outs[0][...] = res

        @pl.when(i >= n_a)
        def _():
            outs[1][...] = res


def _combine(x, ys, slots, wt, meta, mod, tm, split_rows=None):
    t, d = x.shape
    n_tiles = t // tm
    if split_rows is None:
        n_a, out_specs = None, _rows(tm, d)
        out_shape = jax.ShapeDtypeStruct((t, d), F32)
    else:
        n_a, out_specs = split_rows // tm, _split_rows(tm, d, split_rows // tm)
        out_shape = [jax.ShapeDtypeStruct((split_rows, d), F32), jax.ShapeDtypeStruct((t - split_rows, d), F32)]
    gate_spec = pl.BlockSpec((None, None, 1, d), lambda i, ts: (5, ts[i], 0, 0))
    grid_spec = pltpu.PrefetchScalarGridSpec(
        num_scalar_prefetch=1,
        grid=(n_tiles,),
        in_specs=[
            pl.BlockSpec((1, 1, 2 * tm), lambda i, ts: (i, 0, 0), memory_space=pltpu.SMEM),
            pl.BlockSpec((1, 1, 2 * tm), lambda i, ts: (jnp.minimum(i + 1, n_tiles - 1), 0, 0),
                         memory_space=pltpu.SMEM),
            _rows(tm, d),
            _rows(tm, wt.shape[1]),
            gate_spec,
            pl.BlockSpec(memory_space=pl.ANY),
        ],
        out_specs=out_specs,
        scratch_shapes=[pltpu.VMEM((2, 2 * tm) + ys.shape[1:], ys.dtype), pltpu.SemaphoreType.DMA((2,))],
    )
    return pl.pallas_call(
        functools.partial(_combine_kernel, n_a=n_a),
        grid_spec=grid_spec,
        out_shape=out_shape,
        compiler_params=_params(("arbitrary",)),
        name="moe_combine",
    )(meta["seq"], slots, slots, x, wt, mod, ys)


def _tile_meta(seq_lens, tm):
    seq, pos, ln = [], [], []
    for s, n in enumerate(seq_lens):
        assert n % tm == 0
        for k in range(n // tm):
            seq.append(s)
            pos.append(k * tm)
            ln.append(n)
    return {k: jnp.asarray(np.asarray(v, np.int32)) for k, v in (("seq", seq), ("pos", pos), ("len", ln))}


def _rope_tables(s_max):
    half = HEAD_DIM // 2
    inv = ROPE_THETA ** (-jnp.arange(half, dtype=F32) / half)
    ang = jnp.arange(s_max, dtype=F32)[:, None] * inv[None, :]
    cos, sin = jnp.cos(ang), jnp.sin(ang)
    return jnp.concatenate([cos, cos], axis=1), jnp.concatenate([-sin, sin], axis=1)


def _tile_slots(slot, tm):
    t = slot.shape[1]
    return slot.reshape(2, t // tm, tm).transpose(1, 0, 2).reshape(t // tm, 1, 2 * tm)


def _route_params(norm_g, rg_w, rg_b, re_w, re_b):
    d = rg_w.shape[0]
    wr = jnp.zeros((ROUTER_ROWS, d), F32).at[0:N_GROUPS].set(rg_w.T).at[8:8 + N_EXPERTS].set(re_w.T)
    br = jnp.zeros((ROUTER_ROWS,), F32).at[N_GROUPS:8].set(NEG_BIG).at[0:N_GROUPS].set(rg_b)
    br = br.at[8:8 + N_EXPERTS].set(re_b)
    return norm_g, wr.astype(BF16), jnp.broadcast_to(br[:, None], (ROUTER_ROWS, LANES))


def _moe(x, routed, metas, mod, layer, weights, w_bf, cfg, split_rows=None):
    t, d = x.shape
    hp, ri, rw, cnt = routed

    rb = cfg["rb"]
    assert rb % ZERO_ROWS == 0
    expert = ri[0:2]
    rank = ri[2:4]
    counts = cnt[:, 0].astype(jnp.int32)
    nblk = (counts + rb - 1) // rb
    blk_end = jnp.cumsum(nblk)
    row_start = (blk_end - nblk) * rb
    ids = jnp.arange(N_EXPERTS, dtype=jnp.int32)
    start_of = jnp.sum(jnp.where(expert[:, :, None] == ids, row_start, 0), axis=-1)
    slot = start_of + rank
    n_blocks = -(-2 * t // rb) + N_EXPERTS
    n_used = blk_end[-1]
    blk = jnp.arange(n_blocks, dtype=jnp.int32)
    blk_expert = jnp.minimum(jnp.sum(blk[:, None] >= blk_end[None, :], axis=1), N_EXPERTS - 1).astype(jnp.int32)
    last_used = jnp.sum(jnp.where(blk == n_used - 1, blk_expert, 0))
    blk_expert = jnp.where(blk < n_used, blk_expert, last_used)
    onehot = blk_expert[:, None] == ids
    row_end_of = jnp.sum(jnp.where(onehot, row_start + counts, 0), axis=1)
    blk_valid = jnp.where(blk < n_used, jnp.clip(row_end_of - blk * rb, 0, rb), 0).astype(jnp.int32)
    pad_lo = jnp.concatenate([row_start + counts, (n_used * rb).reshape(1)])
    pad_hi = jnp.concatenate([blk_end * rb, jnp.full((1,), n_blocks * rb, jnp.int32)])

    ch = cfg["ch_dispatch"]
    xs, cast_here = _dispatch(hp, _tile_slots(slot, ch), pad_lo.astype(jnp.int32), pad_hi.astype(jnp.int32),
                              n_blocks * rb, ch, None if w_bf else (layer, weights))
    has_next = layer + 1 < weights[0].shape[0]
    ys, w_bf_next = _expert_ffn(xs, blk_expert, blk_valid, *(w_bf or cast_here), rb,
                                cast=(layer + 1, weights) if has_next else None)

    tmc = cfg["tm_combine"]
    wt = rw.T
    x_new = _combine(x, ys, _tile_slots(slot, tmc), wt, metas[tmc], mod, tmc, split_rows)
    return x_new, (w_bf_next or None)


def _config(seq_lens):
    g = int(np.gcd.reduce(np.asarray(seq_lens)))
    tm = min(512, g)
    t = int(np.sum(seq_lens))
    ch = max(c for c in range(SUBLANES, 641, SUBLANES) if t % c == 0)
    return {"tm": tm, "tm_combine": min(512, g), "tm_gmlp": min(512, g), "rb": 512, "ch_dispatch": ch}


def _forward(x_prompt, x_sample, c_prompt, c_sample, ada_w, ada_b, norm_mix_g, norm_ffn_g,
             ab_w_in, q_norm_g, k_norm_g, attn_sink, pool_w, pool_scale, ab_w_out,
             c_w_in, sgu_ln_g, sgu_ln_b, sgu_w, sgu_b, c_w_out,
             router_group_w, router_group_b, router_expert_w, router_expert_b,
             exp_w_gate, exp_w_up, exp_w_down, cfg=None):
    bp, sp, d = x_prompt.shape
    bs, ss, _ = x_sample.shape
    seq_lens = [sp] * bp + [ss] * bs
    n_seq = len(seq_lens)
    if cfg is None:
        cfg = _config(seq_lens)
    depth = ada_w.shape[0]
    x = (x_prompt.reshape(bp * sp, d), x_sample.reshape(bs * ss, d))
    metas ={tm: _tile_meta(seq_lens, tm) for tm in
             {cfg["tm"], cfg["tm_combine"], cfg["tm_gmlp"]}}

    n_pad = -(-n_seq // 8) * 8
    c_pad = jnp.zeros((n_pad, d), F32).at[0:n_seq].set(jnp.concatenate([c_prompt, c_sample], axis=0))
    mod_all = _ada_mod(c_pad, ada_w, ada_b)
    cos_t, sin_t = _rope_tables(max(seq_lens))

    expert_w = (exp_w_gate, exp_w_up, exp_w_down)
    w_bf = None
    for l in range(depth):
        mod = mod_all[l, 0:n_seq].reshape(n_seq, 6, d).transpose(1, 0, 2).reshape(6, n_seq, 1, d)
        i = l // 2
        route = _route_params(norm_ffn_g[l], router_group_w[l], router_group_b[l],
                              router_expert_w[l], router_expert_b[l])
        if l % 2 == 0:
            tm = cfg["tm"]
            q, k, v, p = _inproj(x, metas[tm], mod, norm_mix_g[l], ab_w_in[i].astype(BF16),
                                 q_norm_g[i], k_norm_g[i], cos_t, sin_t, tm)
            sink_b = jnp.broadcast_to(
                jnp.repeat(attn_sink[i].reshape(N_KV_HEADS, GQA_GROUP), ATT_BLOCK, axis=1)[:, :, None],
                (N_KV_HEADS, GQA_GROUP * ATT_BLOCK, LANES)).astype(F32)
            a, made = _attention(q, k, v, sink_b, metas[tm], tm, None if w_bf else (l, expert_w))
            w_bf = w_bf or made
            x, *routed = _mix0(x, a, p, metas[tm], mod, pool_w[i].astype(BF16), pool_scale[i],
                               ab_w_out[i].astype(BF16), route, tm)
        else:
            tm = cfg["tm_gmlp"]
            if isinstance(x, tuple):
                x = jnp.concatenate(x, axis=0)
            u, v = _gmlp_in(x,metas[tm], mod, norm_mix_g[l], c_w_in[i].astype(BF16),
                            sgu_ln_g[i], sgu_ln_b[i], tm)
            sgu_b_b = jnp.broadcast_to(sgu_b[i][:, :, None], (SGU_HEADS, CHUNK, LANES)).astype(F32)
            x, *routed = _gmlp_out(x, u, v, metas[tm], mod, sgu_w[i].astype(BF16), sgu_b_b,
                                   c_w_out[i].astype(BF16), route, tm)
        x, w_bf = _moe(x, routed, metas, mod, l, expert_w, w_bf, cfg,
                       split_rows=bp * sp if l == depth - 1 else None)

    y_prompt, y_sample = x
    return (y_prompt.reshape(bp, sp, d), y_sample.reshape(bs, ss, d))


def kernel(x_prompt, x_sample, c_prompt, c_sample, ada_w, ada_b, norm_mix_g, norm_ffn_g, ab_w_in, q_norm_g,
           k_norm_g, attn_sink, pool_w, pool_scale, ab_w_out, c_w_in, sgu_ln_g, sgu_ln_b, sgu_w, sgu_b,
           c_w_out, router_group_w, router_group_b, router_expert_w, router_expert_b, exp_w_gate, exp_w_up,
           exp_w_down):
    return _forward(x_prompt, x_sample, c_prompt, c_sample, ada_w, ada_b, norm_mix_g, norm_ffn_g, ab_w_in,
                    q_norm_g, k_norm_g, attn_sink, pool_w, pool_scale, ab_w_out, c_w_in, sgu_ln_g, sgu_ln_b,
                    sgu_w, sgu_b, c_w_out, router_group_w, router_group_b, router_expert_w, router_expert_b,
                    exp_w_gate, exp_w_up, exp_w_down)
```
